```python
import jax, jax.numpy as jnp
from jax import lax
import numpy as np

D_MODEL = 1024
BATCH = 8
SEQ = 2048
DEPTH = 4
DEC_BATCH = 128
DEC_SEQ = 4
PAST_LEN = 16384
PAGE_SIZE = 128

N_MIXERS = 3
N_A = (DEPTH + 2) // N_MIXERS
N_B = (DEPTH + 1) // N_MIXERS
N_C = DEPTH // N_MIXERS
CHUNK = 128
A_INNER = 2 * D_MODEL
A_HEADS = 8
A_GROUP = A_INNER // A_HEADS
HEAD_SIZE = 64
B_HEADS = D_MODEL // HEAD_SIZE
DECAY_LORA = 64
ICLR_LORA = 64
GATE_LORA = 128
CONV_W = 3
D_FF = 2816
PLE_DIM = 256
ALPHA = (2 * DEPTH) ** 0.25
BETA = (8 * DEPTH) ** -0.25
LN_EPS = 1e-5
GN_EPS = 64e-5

kernel_name = 'hybrid_gmlp_rwkv7_shortconv_macaron_step'


def layer_norm(x, g, b, eps=LN_EPS):
    xf = x.astype(jnp.float32)
    mu = jnp.mean(xf, -1, keepdims=True)
    var = jnp.mean(jnp.square(xf - mu), -1, keepdims=True)
    return ((xf - mu) * lax.rsqrt(var + eps) * g + b).astype(x.dtype)


def swiglu(x, w_in, w_out):
    gate, up = jnp.split(x @ w_in, 2, axis=-1)
    return (jax.nn.silu(gate) * up) @ w_out


def gmlp_mixer(x, w_in, b_in, ln_g, ln_b, w_s, b_s, w_out):
    bn, t, _ = x.shape
    l = min(t, CHUNK)
    z = jax.nn.gelu(x @ w_in + b_in, approximate=False)
    u, v = jnp.split(z, 2, axis=-1)
    v = layer_norm(v, ln_g, ln_b)
    mask = jnp.tril(jnp.ones((l, l), dtype=bool))
    ws = jnp.where(mask, w_s[:, :l, :l], 0.0).astype(v.dtype)
    vc = v.reshape(bn, t // l, l, A_HEADS, A_GROUP)
    mixed = jnp.einsum('hts,bcshg->bcthg', ws, vc) + b_s[:, :l].T[None, None, :, :, None]
    y = u * mixed.reshape(bn, t, A_INNER).astype(u.dtype)
    return y @ w_out, v


def rwkv7_mixer(x, shift_prev, s0, mu, w_rkv, w0, w1, w2, a0, a1, a2, g1, g2, k_k, k_a, r_k, lnx_g, lnx_b, w_o):
    f32 = jnp.float32
    bn, t, d = x.shape
    x_prev = jnp.concatenate([shift_prev[:, None].astype(x.dtype), x[:, :-1]], axis=1)
    xx = x_prev - x
    xr, xw, xk, xv, xa, xg = [x + xx * mu[i] for i in range(6)]
    rkv = jnp.einsum('pbtd,pde->pbte', jnp.stack([xr, xk, xv]), w_rkv)
    r, k, v = rkv[0], rkv[1], rkv[2]
    w_log = -jax.nn.softplus(-(w0 + jnp.tanh(xw @ w1) @ w2).astype(f32)) - 0.5
    decay = jnp.exp(-jnp.exp(w_log))
    a = jax.nn.sigmoid((a0 + (xa @ a1) @ a2).astype(f32))
    g = jax.nn.sigmoid(xg @ g1) @ g2
    heads = lambda z: z.reshape(bn, t, B_HEADS, HEAD_SIZE)
    kk = heads((k * k_k).astype(f32))
    kk = kk / jnp.maximum(jnp.sqrt(jnp.sum(jnp.square(kk), -1, keepdims=True)), 1e-12)
    kmod = k.astype(f32) * (1.0 + (a - 1.0) * k_a)
    rh, kh, vh, wh, ah = heads(r.astype(f32)), heads(kmod), heads(v.astype(f32)), heads(decay), heads(a)

    def step(s, inp):
        r_t, w_t, k_t, v_t, kk_t, a_t = inp
        sa = jnp.einsum('bhvk,bhk->bhv', s, -kk_t)
        s = s * w_t[:, :, None, :] + sa[..., None] * (kk_t * a_t)[:, :, None, :] + v_t[..., None] * k_t[:, :, None, :]
        return s, jnp.einsum('bhvk,bhk->bhv', s, r_t)

    seq = tuple(jnp.moveaxis(z, 1, 0) for z in (rh, wh, kh, vh, kk, ah))
    s_fin, ys = lax.scan(step, s0.astype(f32), seq)
    ys = jnp.moveaxis(ys, 0, 1)
    m = jnp.mean(ys, -1, keepdims=True)
    var = jnp.mean(jnp.square(ys - m), -1, keepdims=True)
    yn = ((ys - m) * lax.rsqrt(var + GN_EPS)).reshape(bn, t, d) * lnx_g + lnx_b
    bonus = (jnp.sum(rh * kh * r_k, -1, keepdims=True) * vh).reshape(bn, t, d)
    out = ((yn + bonus).astype(x.dtype) * g) @ w_o
    return out, x[:, -1].astype(shift_prev.dtype), s_fin.astype(s0.dtype)


def conv_mixer(x, buf_prev, w_in, conv_w, w_out):
    t = x.shape[1]
    bg, cg, xin = jnp.split(x @ w_in, 3, axis=-1)
    z = cg * xin
    zp = jnp.concatenate([buf_prev.astype(z.dtype), z], axis=1)
    conv = conv_w[0] * zp[:, 0:t] + conv_w[1] * zp[:, 1:t + 1] + conv_w[2] * zp[:, 2:t + 2]
    return (bg * conv) @ w_out, zp[:, -(CONV_W - 1):].astype(buf_prev.dtype)


def run_trunk(x, p, b_wkv, b_shift, c_conv, W, keep_chunk_state):
    new_a_v, new_wkv, new_shift, new_conv = [], [], [], []
    for i in range(DEPTH):
        j = i // N_MIXERS
        kind = i % N_MIXERS
        x = layer_norm(ALPHA * x + 0.5 * swiglu(x, W['ffn_w_in'][i, 0], W['ffn_w_out'][i, 0]), W['ln_g'][i, 0], W['ln_b'][i, 0])
        if kind == 0:
            h, v = gmlp_mixer(x, W['a_w_in'][j], W['a_b_in'][j], W['a_ln_g'][j], W['a_ln_b'][j], W['a_w_s'][j], W['a_b_s'][j], W['a_w_out'][j])
            if keep_chunk_state:
                new_a_v.append(v)
        elif kind == 1:
            h, sh, s = rwkv7_mixer(x, b_shift[j], b_wkv[j], W['b_mu'][j], W['b_w_rkv'][j], W['b_w0'][j], W['b_w1'][j], W['b_w2'][j],
                                   W['b_a0'][j], W['b_a1'][j], W['b_a2'][j], W['b_g1'][j], W['b_g2'][j], W['b_k_k'][j], W['b_k_a'][j],
                                   W['b_r_k'][j], W['b_lnx_g'][j], W['b_lnx_b'][j], W['b_w_o'][j])
            new_shift.append(sh)
            new_wkv.append(s)
        else:
            h, buf = conv_mixer(x, c_conv[j], W['c_w_in'][j], W['c_conv_w'][j], W['c_w_out'][j])
            new_conv.append(buf)
        x = layer_norm(ALPHA * x + h, W['ln_g'][i, 1], W['ln_b'][i, 1])
        x = layer_norm(ALPHA * x + 0.5 * swiglu(x, W['ffn_w_in'][i, 1], W['ffn_w_out'][i, 1]), W['ln_g'][i, 2], W['ln_b'][i, 2])
        x = x + jax.nn.sigmoid(x @ W['ple_w_gate'][i]) * (p[i].astype(x.dtype) @ W['ple_w_proj'][i])
    a_state = jnp.stack(new_a_v) if keep_chunk_state else None
    return x, a_state, jnp.stack(new_wkv), jnp.stack(new_shift), jnp.stack(new_conv)


def setup_inputs(seed: int = 0) -> dict:
    key = jax.random.key(seed)
    kit = iter(jax.random.split(key, 64))
    D = D_MODEL

    def nrm(shape, scale):
        return scale * jax.random.normal(next(kit), shape, jnp.float32)

    return {
        'x_prompt': nrm((BATCH, SEQ, D), 1.0),
        'x_sample': nrm((DEC_BATCH, DEC_SEQ, D), 1.0),
        'state_b_wkv': nrm((N_B, DEC_BATCH, B_HEADS, HEAD_SIZE, HEAD_SIZE), 0.5),
        'state_b_shift': nrm((N_B, DEC_BATCH, D), 1.0),
        'state_c_conv': nrm((N_C, DEC_BATCH, CONV_W - 1, D), 1.0),
        'p_prompt': nrm((DEPTH, BATCH, SEQ, PLE_DIM), 1.0),
        'p_sample': nrm((DEPTH, DEC_BATCH, DEC_SEQ, PLE_DIM), 1.0),
        'ln_g': 1.0 + nrm((DEPTH, 3, D), 0.02),
        'ln_b': nrm((DEPTH, 3, D), 0.02),
        'ffn_w_in': nrm((DEPTH, 2, D, 2 * D_FF), D ** -0.5),
        'ffn_w_out': nrm((DEPTH, 2, D_FF, D), BETA * D_FF ** -0.5),
        'ple_w_gate': nrm((DEPTH, D, D), D ** -0.5),
        'ple_w_proj': nrm((DEPTH, PLE_DIM, D), BETA * PLE_DIM ** -0.5),
        'a_w_in': nrm((N_A, D, 2 * A_INNER), D ** -0.5),
        'a_b_in': nrm((N_A, 2 * A_INNER), 0.02),
        'a_ln_g': 1.0 + nrm((N_A, A_INNER), 0.02),
        'a_ln_b': nrm((N_A, A_INNER), 0.02),
        'a_w_s': nrm((N_A, A_HEADS, CHUNK, CHUNK), CHUNK ** -0.5),
        'a_b_s': 1.0 + nrm((N_A, A_HEADS, CHUNK), 0.02),
        'a_w_out': nrm((N_A, A_INNER, D), BETA * A_INNER ** -0.5),
        'b_mu': jax.random.uniform(next(kit), (N_B, 6, D), jnp.float32),
        'b_w_rkv': nrm((N_B, 3, D, D), D ** -0.5),
        'b_w0': nrm((N_B, D), 0.5) - 1.0,
        'b_w1': nrm((N_B, D, DECAY_LORA), D ** -0.5),
        'b_w2': nrm((N_B, DECAY_LORA, D), 0.1 * DECAY_LORA ** -0.5),
        'b_a0': nrm((N_B, D), 0.1),
        'b_a1': nrm((N_B, D, ICLR_LORA), D ** -0.5),
        'b_a2': nrm((N_B, ICLR_LORA, D), 0.1 * ICLR_LORA ** -0.5),
        'b_g1': nrm((N_B, D, GATE_LORA), D ** -0.5),
        'b_g2': nrm((N_B, GATE_LORA, D), GATE_LORA ** -0.5),
        'b_k_k': 0.85 + nrm((N_B, D), 0.02),
        'b_k_a': 1.0 + nrm((N_B, D), 0.02),
        'b_r_k': nrm((N_B, B_HEADS, HEAD_SIZE), 0.1),
        'b_lnx_g': 1.0 + nrm((N_B, D), 0.02),
        'b_lnx_b': nrm((N_B, D), 0.02),
        'b_w_o': nrm((N_B, D, D), BETA * D ** -0.5),
        'c_w_in': nrm((N_C, D, 3 * D), D ** -0.5),
        'c_conv_w': nrm((N_C, CONV_W, D), CONV_W ** -0.5),
        'c_w_out': nrm((N_C, D, D), BETA * D ** -0.5),
    }


def reference(x_prompt, x_sample, state_b_wkv, state_b_shift, state_c_conv, p_prompt, p_sample,
              ln_g, ln_b, ffn_w_in, ffn_w_out, ple_w_gate, ple_w_proj,
              a_w_in, a_b_in, a_ln_g, a_ln_b, a_w_s, a_b_s, a_w_out,
              b_mu, b_w_rkv, b_w0, b_w1, b_w2, b_a0, b_a1, b_a2, b_g1, b_g2, b_k_k, b_k_a, b_r_k,
              b_lnx_g, b_lnx_b, b_w_o, c_w_in, c_conv_w, c_w_out):
    W = dict(ln_g=ln_g, ln_b=ln_b, ffn_w_in=ffn_w_in, ffn_w_out=ffn_w_out, ple_w_gate=ple_w_gate, ple_w_proj=ple_w_proj,
             a_w_in=a_w_in, a_b_in=a_b_in, a_ln_g=a_ln_g, a_ln_b=a_ln_b, a_w_s=a_w_s, a_b_s=a_b_s, a_w_out=a_w_out,
             b_mu=b_mu, b_w_rkv=b_w_rkv, b_w0=b_w0, b_w1=b_w1, b_w2=b_w2, b_a0=b_a0, b_a1=b_a1, b_a2=b_a2,
             b_g1=b_g1, b_g2=b_g2, b_k_k=b_k_k, b_k_a=b_k_a, b_r_k=b_r_k, b_lnx_g=b_lnx_g, b_lnx_b=b_lnx_b, b_w_o=b_w_o,
             c_w_in=c_w_in, c_conv_w=c_conv_w, c_w_out=c_w_out)
    bp = x_prompt.shape[0]
    zero_wkv = jnp.zeros((N_B, bp) + state_b_wkv.shape[2:], state_b_wkv.dtype)
    zero_shift = jnp.zeros((N_B, bp, D_MODEL), state_b_shift.dtype)
    zero_conv = jnp.zeros((N_C, bp, CONV_W - 1, D_MODEL), state_c_conv.dtype)
    y_prompt, _, wkv_p, shift_p, conv_p = run_trunk(x_prompt, p_prompt, zero_wkv, zero_shift, zero_conv, W, False)
    y_sample, a_v_s, wkv_s, shift_s, conv_s = run_trunk(x_sample, p_sample, state_b_wkv, state_b_shift, state_c_conv, W, True)
    return (y_prompt, y_sample, a_v_s, wkv_p, shift_p, conv_p, wkv_s, shift_s, conv_s)
```

```python
import functools
import math

import jax
import jax.numpy as jnp
from jax import lax
from jax.experimental import pallas as pl
from jax.experimental.pallas import tpu as pltpu

F32 = jnp.float32
BF16 = jnp.bfloat16

D_MODEL = 1024
DEPTH = 4
N_MIXERS = 3
CHUNK = 128
A_INNER = 2 * D_MODEL
A_HEADS = 8
A_GROUP = A_INNER // A_HEADS
HEAD_SIZE = 64
B_HEADS = D_MODEL // HEAD_SIZE
CONV_W = 3
D_FF = 2816
PLE_DIM = 256
ALPHA = (2 * DEPTH) ** 0.25
LN_EPS = 1e-5
GN_EPS = 64e-5

LANES = 128
ROW_TILE = 512
SCAN_CHUNK = 64
HEAD_PAIRS = B_HEADS // 2
VMEM_LIMIT = 56 * 1024 * 1024

NN = (((1,), (0,)), ((), ()))
NT = (((1,), (1,)), ((), ()))
TN = (((0,), (0,)), ((), ()))


def _dot(a, b, dims=NN):
    return lax.dot_general(a, b, dims, preferred_element_type=F32)


def _split2(x):
    hi = x.astype(BF16)
    lo = (x - hi.astype(F32)).astype(BF16)
    return hi, lo


def _split3(x):
    hi = x.astype(BF16)
    r1 = x - hi.astype(F32)
    mid = r1.astype(BF16)
    lo = (r1 - mid.astype(F32)).astype(BF16)
    return hi, mid, lo


def _dot3(a, b, dims=NN):
    ah, al = _split2(a)
    bh, bl = _split2(b)
    return _dot(ah, bh, dims) + (_dot(ah, bl, dims) + _dot(al, bh, dims))


def _layer_norm(x, g, b, eps=LN_EPS):
    mu = jnp.mean(x, axis=-1, keepdims=True)
    xc = x - mu
    var = jnp.mean(xc * xc, axis=-1, keepdims=True)
    return xc * lax.rsqrt(var + eps) * g + b


def _sigmoid(x):
    return 1.0 / (1.0 + jnp.exp(-x))


def _const_spec(shape):
    nd = len(shape)
    return pl.BlockSpec(shape, lambda *_: (0,) * nd, pipeline_mode=pl.Buffered(1))


def _row_spec(width, tm=ROW_TILE):
    return pl.BlockSpec((tm, width), lambda i: (i, 0))


def _params(sem):
    return pltpu.CompilerParams(dimension_semantics=sem, vmem_limit_bytes=VMEM_LIMIT)


FF_SPLIT = 2
FF_BLOCK = D_FF // FF_SPLIT


def _ffn_kernel(with_ple, x_ref, wi_ref, wo_ref, g_ref, b_ref, *rest):
    if with_ple:
        p_ref, wg_ref, wp_ref, o_ref = rest
    else:
        (o_ref,) = rest
    x = x_ref[...]
    xb = x.astype(BF16)
    acc = None
    for c in range(FF_SPLIT):
        lo = c * FF_BLOCK
        gate = _dot(xb, wi_ref[:, lo:lo + FF_BLOCK])
        up = _dot(xb, wi_ref[:, D_FF + lo:D_FF + lo + FF_BLOCK])
        act = (gate * _sigmoid(gate) * up).astype(BF16)
        part = _dot(act, wo_ref[lo:lo + FF_BLOCK, :])
        acc = part if acc is None else acc + part
    y = _layer_norm(ALPHA * x + 0.5 * acc, g_ref[...], b_ref[...])
    if with_ple:
        gate = _sigmoid(_dot(y.astype(BF16), wg_ref[...]))
        y = y + gate * _dot(p_ref[...].astype(BF16), wp_ref[...])
    o_ref[...] = y


def _ffn(x, wi, wo, g, b, ple=None):
    n = x.shape[0]
    tm = min(ROW_TILE, n)
    args = [x, wi, wo, g, b]
    specs = [_row_spec(D_MODEL, tm), _const_spec(wi.shape), _const_spec(wo.shape),
             _const_spec(g.shape), _const_spec(b.shape)]
    if ple is not None:
        p, wg, wp = ple
        args += [p, wg, wp]
        specs += [_row_spec(PLE_DIM, tm), _const_spec(wg.shape), _const_spec(wp.shape)]
    return pl.pallas_call(
        functools.partial(_ffn_kernel, ple is not None),
        grid=(n // tm,),
        in_specs=specs,
        out_specs=_row_spec(D_MODEL, tm),
        out_shape=jax.ShapeDtypeStruct((n, D_MODEL), F32),
        compiler_params=_params(("parallel",)),
        name="ffn_ple" if ple is not None else "ffn",
    )(*args)


def _gmlp_kernel(keep_v, x_ref, wi_ref, bi_ref, lg_ref, lb_ref, ws_ref, bs_ref, wo_ref,
                 g_ref, b_ref, *rest):
    if keep_v:
        o_ref, v_ref, y_ref = rest
    else:
        o_ref, y_ref = rest
    x = x_ref[...]
    tm = x.shape[0]
    z = _dot(x.astype(BF16), wi_ref[...]) + bi_ref[...]
    z = 0.5 * z * (1.0 + lax.erf(z * (1.0 / math.sqrt(2.0))))
    u = z[:, :A_INNER]
    v = _layer_norm(z[:, A_INNER:], lg_ref[...], lb_ref[...])
    if keep_v:
        v_ref[...] = v
    vb = v.astype(BF16)
    for c in range(tm // CHUNK):
        rows = slice(c * CHUNK, (c + 1) * CHUNK)
        for h in range(A_HEADS):
            cols = slice(h * A_GROUP, (h + 1) * A_GROUP)
            mixed = _dot(ws_ref[h], vb[rows, cols]) + bs_ref[:, h:h + 1]
            y_ref[rows, cols] = (u[rows, cols] * mixed).astype(BF16)
    out = _dot(y_ref[...], wo_ref[...])
    o_ref[...] = _layer_norm(ALPHA * x + out, g_ref[...], b_ref[...])


def _gmlp(x, wi, bi, lg, lb, ws, bs, wo, g, b, keep_v):
    n = x.shape[0]
    tm = min(ROW_TILE, n)
    consts = [wi, bi, lg, lb, ws, bs, wo, g, b]
    out_shape = [jax.ShapeDtypeStruct((n, D_MODEL), F32)]
    out_specs = [_row_spec(D_MODEL, tm)]
    if keep_v:
        out_shape.append(jax.ShapeDtypeStruct((n, A_INNER), F32))
        out_specs.append(_row_spec(A_INNER, tm))
    res = pl.pallas_call(
        functools.partial(_gmlp_kernel, keep_v),
        grid=(n // tm,),
        in_specs=[_row_spec(D_MODEL, tm)] + [_const_spec(c.shape) for c in consts],
        out_specs=out_specs,
        out_shape=out_shape,
        scratch_shapes=[pltpu.VMEM((tm, A_INNER), BF16)],
        compiler_params=_params(("parallel",)),
        name="gmlp",
    )(x, *consts)
    return (res[0], res[1]) if keep_v else (res[0], None)


def _conv_tail(x, bg, conv, wo_ref, g_ref, b_ref, o_ref):
    out = _dot((bg * conv).astype(BF16), wo_ref[...])
    o_ref[...] = _layer_norm(ALPHA * x + out, g_ref[...], b_ref[...])


def _conv_long_kernel(tiles_per_seq, x_ref, buf_ref, wi_ref, cw_ref, wo_ref, g_ref, b_ref,
                      o_ref, tail_ref, carry_ref):
    x = x_ref[...]
    tm = x.shape[0]
    h3 = _dot(x.astype(BF16), wi_ref[...])
    bg = h3[:, :D_MODEL]
    z = h3[:, D_MODEL:2 * D_MODEL] * h3[:, 2 * D_MODEL:]

    @pl.when(pl.program_id(0) % tiles_per_seq == 0)
    def _():
        carry_ref[...] = buf_ref[0]

    row = lax.broadcasted_iota(jnp.int32, (tm, D_MODEL), 0)
    prev1 = carry_ref[7:8, :]
    prev2 = carry_ref[6:7, :]
    z1 = jnp.where(row == 0, prev1, pltpu.roll(z, 1, 0))
    z2 = jnp.where(row == 0, prev2, jnp.where(row == 1, prev1, pltpu.roll(z, 2, 0)))
    conv = cw_ref[0:1, :] * z2 + cw_ref[1:2, :] * z1 + cw_ref[2:3, :] * z
    tail = z[tm - 8:, :]
    carry_ref[...] = tail
    tail_ref[0] = tail
    _conv_tail(x, bg, conv, wo_ref, g_ref, b_ref, o_ref)


def _conv_short_kernel(seq, x_ref, h1_ref, h2_ref, wi_ref, cw_ref, wo_ref, g_ref, b_ref,
                       o_ref, z_ref):
    x = x_ref[...]
    tm = x.shape[0]
    h3 = _dot(x.astype(BF16), wi_ref[...])
    bg = h3[:, :D_MODEL]
    z = h3[:, D_MODEL:2 * D_MODEL] * h3[:, 2 * D_MODEL:]
    t = lax.broadcasted_iota(jnp.int32, (tm, D_MODEL), 0) % seq
    z1 = jnp.where(t >= 1, pltpu.roll(z, 1, 0), h1_ref[...])
    z2 = jnp.where(t >= 2, pltpu.roll(z, 2, 0), h2_ref[...])
    conv = cw_ref[0:1, :] * z2 + cw_ref[1:2, :] * z1 + cw_ref[2:3, :] * z
    z_ref[...] = z
    _conv_tail(x, bg, conv, wo_ref, g_ref, b_ref, o_ref)


def _conv_mixer(x, buf_prev, seq, wi, cw, wo, g, b):
    n = x.shape[0]
    bn = n // seq
    tm = min(ROW_TILE, n)
    consts = [wi, cw, wo, g, b]
    const_specs = [_const_spec(c.shape) for c in consts]
    if seq >= tm:
        tiles_per_seq = seq // tm
        buf8 = jnp.concatenate([jnp.zeros((bn, 6, D_MODEL), F32), buf_prev], axis=1)
        out, tails = pl.pallas_call(
            functools.partial(_conv_long_kernel, tiles_per_seq),
            grid=(n // tm,),
            in_specs=[_row_spec(D_MODEL, tm),
                      pl.BlockSpec((1, 8, D_MODEL), lambda i: (i // tiles_per_seq, 0, 0))] + const_specs,
            out_specs=[_row_spec(D_MODEL, tm), pl.BlockSpec((1, 8, D_MODEL), lambda i: (i, 0, 0))],
            out_shape=[jax.ShapeDtypeStruct((n, D_MODEL), F32),
                       jax.ShapeDtypeStruct((n // tm, 8, D_MODEL), F32)],
            scratch_shapes=[pltpu.VMEM((8, D_MODEL), F32)],
            compiler_params=_params(("arbitrary",)),
            name="conv_long",
        )(x, buf8, *consts)
        new_buf = tails[tiles_per_seq - 1::tiles_per_seq, 6:8, :]
        return out, new_buf
    assert tm % seq == 0 and seq >= CONV_W - 1
    t = jnp.arange(n) % seq
    last = jnp.repeat(buf_prev[:, 1, :], seq, axis=0)
    first = jnp.repeat(buf_prev[:, 0, :], seq, axis=0)
    h1 = jnp.where((t == 0)[:, None], last, 0.0)
    h2 = jnp.where((t == 0)[:, None], first, jnp.where((t == 1)[:, None], last, 0.0))
    out, z = pl.pallas_call(
        functools.partial(_conv_short_kernel, seq),
        grid=(n // tm,),
        in_specs=[_row_spec(D_MODEL, tm)] * 3 + const_specs,
        out_specs=[_row_spec(D_MODEL, tm)] * 2,
        out_shape=[jax.ShapeDtypeStruct((n, D_MODEL), F32)] * 2,
        compiler_params=_params(("parallel",)),
        name="conv_short",
    )(x, h1, h2, *consts)
    new_buf = z.reshape(bn, seq, D_MODEL)[:, seq - (CONV_W - 1):, :]
    return out, new_buf


def _head_sum_bcast(x, e_ref, et_ref):
    hi, lo = _split2(x)
    s = _dot(hi, e_ref[...]) + _dot(lo, e_ref[...])
    s_hi, s_mid, s_lo = _split3(s)
    return _dot(s_hi, et_ref[...]) + (_dot(s_mid, et_ref[...]) + _dot(s_lo, et_ref[...]))


def _rwkv_proj_kernel(x_ref, xp_ref, mu_ref, wrkv_ref, w0_ref, w1_ref, w2_ref, a0_ref, a1_ref,
                      a2_ref, g1_ref, g2_ref, kk_ref, ka_ref, rk_ref, e_ref, et_ref,
                      r_out, lw_out, k_out, v_out, c_out, b_out, bonus_out, g_out):
    x = x_ref[...]
    xx = xp_ref[...] - x
    mix = lambda i: (x + xx * mu_ref[i:i + 1, :]).astype(BF16)
    r = _dot(mix(0), wrkv_ref[0])
    k = _dot(mix(2), wrkv_ref[1])
    v = _dot(mix(3), wrkv_ref[2])
    zw = w0_ref[...] + _dot(jnp.tanh(_dot(mix(1), w1_ref[...])).astype(BF16), w2_ref[...])
    lw_out[...] = -_sigmoid(zw) * math.exp(-0.5)
    a = _sigmoid(a0_ref[...] + _dot(_dot(mix(4), a1_ref[...]).astype(BF16), a2_ref[...]))
    g_out[...] = _dot(_sigmoid(_dot(mix(5), g1_ref[...])).astype(BF16), g2_ref[...])
    kk = k * kk_ref[...]
    norm = jnp.sqrt(_head_sum_bcast(kk * kk, e_ref, et_ref))
    c = kk / jnp.maximum(norm, 1e-12)
    kmod = k * (1.0 + (a - 1.0) * ka_ref[...])
    r_out[...] = r
    k_out[...] = kmod
    v_out[...] = v
    c_out[...] = c
    b_out[...] = c * a
    bonus_out[...] = _head_sum_bcast(r * kmod * rk_ref[...], e_ref, et_ref) * v


def _rwkv_proj(x, xprev, consts):
    n = x.shape[0]
    tm = min(ROW_TILE // 2, n)
    return pl.pallas_call(
        _rwkv_proj_kernel,
        grid=(n // tm,),
        in_specs=[_row_spec(D_MODEL, tm)] * 2 + [_const_spec(c.shape) for c in consts],
        out_specs=[_row_spec(D_MODEL, tm)] * 8,
        out_shape=[jax.ShapeDtypeStruct((n, D_MODEL), F32)] * 8,
        compiler_params=_params(("parallel",)),
        name="rwkv_proj",
    )(x, xprev, *consts)


def _pair_rows(x, lane_lo):
    return jnp.concatenate([jnp.where(lane_lo, x, 0.0), jnp.where(lane_lo, 0.0, x)], axis=0)


def _rwkv_scan_kernel(r_ref, lw_ref, k_ref, v_ref, c_ref, b_ref, s0_ref, y_ref, sfin_ref, s_ref):
    C = SCAN_CHUNK
    j = pl.program_id(1)

    @pl.when(j == 0)
    def _():
        s_ref[...] = s0_ref[0]

    row = lax.broadcasted_iota(jnp.int32, (C, 2 * C), 0)
    col = lax.broadcasted_iota(jnp.int32, (C, 2 * C), 1) % C
    strict = row > col
    incl = row >= col
    lane_lo = lax.broadcasted_iota(jnp.int32, (C, LANES), 1) < HEAD_SIZE
    lane_lo2 = lax.broadcasted_iota(jnp.int32, (2 * C, LANES), 1) < HEAD_SIZE
    sq_row = lax.broadcasted_iota(jnp.int32, (LANES, LANES), 0) < HEAD_SIZE
    sq_col = lax.broadcasted_iota(jnp.int32, (LANES, LANES), 1) < HEAD_SIZE
    diag_blocks = sq_row == sq_col
    tri = (lax.broadcasted_iota(jnp.int32, (C, C), 0)
           >= lax.broadcasted_iota(jnp.int32, (C, C), 1)).astype(BF16)

    def pair_mul(m, x):
        return _dot3(m, _pair_rows(x, lane_lo))

    for p in range(HEAD_PAIRS):
        sl = slice(p * LANES, (p + 1) * LANES)
        r, lw, k, v, c, b = (ref[:, sl] for ref in (r_ref, lw_ref, k_ref, v_ref, c_ref, b_ref))
        s_prev = s_ref[p]
        l_hi, l_mid, l_lo = _split3(lw)
        cum = _dot(tri, l_hi) + (_dot(tri, l_mid) + _dot(tri, l_lo))
        p_inc = jnp.exp(cum)
        p_inv = jnp.exp(-cum)
        p_exc = jnp.exp(cum - lw)
        q = jnp.concatenate([c * p_exc, r * p_inc], axis=0)
        bt = b * p_inv
        kt = k * p_inv
        kb = jnp.concatenate([_pair_rows(bt, lane_lo), _pair_rows(kt, lane_lo)], axis=0)
        gram = _dot3(q, kb, NT)
        l_cb = jnp.where(strict, gram[:C, :2 * C], 0.0)
        l_ck = jnp.where(strict, gram[:C, 2 * C:], 0.0)
        a_rb = jnp.where(incl, gram[C:, :2 * C], 0.0)
        a_rk = jnp.where(incl, gram[C:, 2 * C:], 0.0)
        qs = _dot3(q, s_prev, NT)
        u = -(qs[:C] + pair_mul(l_ck, v))
        m = l_cb
        u = u - pair_mul(m, u)
        for _ in range(int(math.log2(C)) - 1):
            m = _dot3(m, _pair_rows(m, lane_lo))
            u = u + pair_mul(m, u)
        uv = jnp.concatenate([_pair_rows(u, lane_lo), _pair_rows(v, lane_lo)], axis=0)
        y = qs[C:] + _dot3(jnp.concatenate([a_rb, a_rk], axis=1), uv)
        y_ref[:, sl] = y
        upd = _dot3(jnp.concatenate([u, v], axis=0), jnp.concatenate([bt, kt], axis=0), TN)
        s_new = (s_prev + jnp.where(diag_blocks, upd, 0.0)) * p_inc[C - 1:C, :]
        s_ref[p] = s_new

    @pl.when(j == pl.num_programs(1) - 1)
    def _():
        sfin_ref[0] = s_ref[...]


def _rwkv_scan(r, lw, k, v, c, b, s0_bd, bn, seq):
    n_chunks = seq // SCAN_CHUNK
    row_spec = pl.BlockSpec((SCAN_CHUNK, D_MODEL), lambda i, j: (i * n_chunks + j, 0))
    st_spec = pl.BlockSpec((1, HEAD_PAIRS, LANES, LANES), lambda i, j: (i, 0, 0, 0))
    return pl.pallas_call(
        _rwkv_scan_kernel,
        grid=(bn, n_chunks),
        in_specs=[row_spec] * 6 + [st_spec],
        out_specs=[row_spec, st_spec],
        out_shape=[jax.ShapeDtypeStruct((bn * seq, D_MODEL), F32),
                   jax.ShapeDtypeStruct((bn, HEAD_PAIRS, LANES, LANES), F32)],
        scratch_shapes=[pltpu.VMEM((HEAD_PAIRS, LANES, LANES), F32)],
        compiler_params=_params(("parallel", "arbitrary")),
        name="rwkv_scan",
    )(r, lw, k, v, c, b, s0_bd)


def _rwkv_post_kernel(x_ref, y_ref, bonus_ref, gate_ref, lg_ref, lb_ref, wo_ref, e_ref, et_ref,
                      g_ref, b_ref, o_ref):
    ys = y_ref[...]
    inv = 1.0 / HEAD_SIZE
    m = _head_sum_bcast(ys, e_ref, et_ref) * inv
    yc = ys - m
    var = _head_sum_bcast(yc * yc, e_ref, et_ref) * inv
    yn = yc * lax.rsqrt(var + GN_EPS) * lg_ref[...] + lb_ref[...]
    out = _dot(((yn + bonus_ref[...]) * gate_ref[...]).astype(BF16), wo_ref[...])
    o_ref[...] = _layer_norm(ALPHA * x_ref[...] + out, g_ref[...], b_ref[...])


def _rwkv_post(x, ys, bonus, gate, consts):
    n = x.shape[0]
    tm = min(ROW_TILE, n)
    return pl.pallas_call(
        _rwkv_post_kernel,
        grid=(n // tm,),
        in_specs=[_row_spec(D_MODEL, tm)] * 4 + [_const_spec(c.shape) for c in consts],
        out_specs=_row_spec(D_MODEL, tm),
        out_shape=jax.ShapeDtypeStruct((n, D_MODEL), F32),
        compiler_params=_params(("parallel",)),
        name="rwkv_post",
    )(x, ys, bonus, gate, *consts)


def _to_pair_blocks(s):
    bn = s.shape[0]
    s = s.reshape(bn, HEAD_PAIRS, 2, HEAD_SIZE, HEAD_SIZE)
    zero = jnp.zeros_like(s[:, :, 0])
    top = jnp.concatenate([s[:, :, 0], zero], axis=-1)
    bot = jnp.concatenate([zero, s[:, :, 1]], axis=-1)
    return jnp.concatenate([top, bot], axis=-2)


def _from_pair_blocks(s_bd):
    bn = s_bd.shape[0]
    s = jnp.stack([s_bd[:, :, :HEAD_SIZE, :HEAD_SIZE], s_bd[:, :, HEAD_SIZE:, HEAD_SIZE:]], axis=2)
    return s.reshape(bn, B_HEADS, HEAD_SIZE, HEAD_SIZE)


def _rwkv_mixer(x, shift_prev, s0, seq, w, g, b):
    n = x.shape[0]
    bn = n // seq
    x3 = x.reshape(bn, seq, D_MODEL)
    xprev = jnp.concatenate([shift_prev[:, None, :], x3[:, :-1, :]], axis=1).reshape(n, D_MODEL)
    r, lw, k, v, c, bb, bonus, gate = _rwkv_proj(x, xprev, w["proj"])
    pad = (-seq) % SCAN_CHUNK
    seq_p = seq + pad
    if pad:
        pad_rows = lambda z: jnp.pad(z.reshape(bn, seq, D_MODEL), ((0, 0), (0, pad), (0, 0))).reshape(bn * seq_p, D_MODEL)
        r, lw, k, v, c, bb = (pad_rows(z) for z in (r, lw, k, v, c, bb))
    ys, s_bd = _rwkv_scan(r, lw, k, v, c, bb, _to_pair_blocks(s0), bn, seq_p)
    if pad:
        ys = ys.reshape(bn, seq_p, D_MODEL)[:, :seq, :].reshape(n, D_MODEL)
    out = _rwkv_post(x, ys, bonus, gate, w["post"] + [g, b])
    return out, x3[:, -1, :], _from_pair_blocks(s_bd)


def _row(v):
    return v.reshape(1, -1)


def _run_trunk(x3, p4, b_wkv, b_shift, c_conv, W, keep_v):
    bn, seq, _ = x3.shape
    n = bn * seq
    x = x3.reshape(n, D_MODEL)
    new_v, new_wkv, new_shift, new_conv = [], [], [], []
    for i in range(DEPTH):
        j, kind = divmod(i, N_MIXERS)
        ln_g = lambda s: _row(W["ln_g"][i, s])
        ln_b = lambda s: _row(W["ln_b"][i, s])
        x = _ffn(x, W["ffn_w_in"][i][0], W["ffn_w_out"][i][0], ln_g(0), ln_b(0))
        if kind == 0:
            mix = W["a_mix_long"][j] if seq >= CHUNK else W["a_mix_short"][j]
            x, v = _gmlp(x, W["a_w_in"][j], _row(W["a_b_in"][j]), _row(W["a_ln_g"][j]), _row(W["a_ln_b"][j]),
                         mix[0], mix[1], W["a_w_out"][j], ln_g(1), ln_b(1), keep_v)
            if keep_v:
                new_v.append(v.reshape(bn, seq, A_INNER))
        elif kind == 1:
            x, sh, s = _rwkv_mixer(x, b_shift[j], b_wkv[j], seq, W["b"][j], ln_g(1), ln_b(1))
            new_shift.append(sh)
            new_wkv.append(s)
        else:
            x, buf = _conv_mixer(x, c_conv[j], seq, W["c_w_in"][j], W["c_conv_w"][j], W["c_w_out"][j],
                                 ln_g(1), ln_b(1))
            new_conv.append(buf)
        x = _ffn(x, W["ffn_w_in"][i][1], W["ffn_w_out"][i][1], ln_g(2), ln_b(2),
                 ple=(p4[i].reshape(n, PLE_DIM), W["ple_w_gate"][i], W["ple_w_proj"][i]))
    a_state = jnp.stack(new_v) if keep_v else None
    return x.reshape(bn, seq, D_MODEL), a_state, jnp.stack(new_wkv), jnp.stack(new_shift), jnp.stack(new_conv)


def _gmlp_mix_mats(w_s, b_s, seq):
    l = min(seq, CHUNK)
    ws = jnp.where(jnp.tril(jnp.ones((l, l), dtype=bool)), w_s[:, :l, :l], 0.0)
    reps = CHUNK // l
    if reps > 1:
        eye = jnp.eye(reps, dtype=F32)
        ws = jnp.einsum("ab,hts->hatbs", eye, ws).reshape(A_HEADS, CHUNK, CHUNK)
    bias = jnp.tile(b_s[:, :l].T, (reps, 1))
    return ws.astype(BF16), bias


def kernel(x_prompt, x_sample, state_b_wkv, state_b_shift, state_c_conv, p_prompt, p_sample, ln_g, ln_b, ffn_w_in, ffn_w_out, ple_w_gate, ple_w_proj, a_w_in, a_b_in, a_ln_g, a_ln_b, a_w_s, a_b_s, a_w_out, b_mu, b_w_rkv, b_w0, b_w1, b_w2, b_a0, b_a1, b_a2, b_g1, b_g2, b_k_k, b_k_a, b_r_k, b_lnx_g, b_lnx_b, b_w_o, c_w_in, c_conv_w, c_w_out):
    bf = lambda w: w.astype(BF16)
    head_of_lane = jnp.arange(D_MODEL) // HEAD_SIZE
    e = (head_of_lane[:, None] == jnp.arange(LANES)[None, :]).astype(BF16)
    et = e.T
    n_b = b_mu.shape[0]
    n_a = a_w_in.shape[0]
    W = dict(
        ln_g=ln_g, ln_b=ln_b,
        ffn_w_in=bf(ffn_w_in), ffn_w_out=bf(ffn_w_out),
        ple_w_gate=bf(ple_w_gate), ple_w_proj=bf(ple_w_proj),
        a_w_in=bf(a_w_in), a_b_in=a_b_in, a_ln_g=a_ln_g, a_ln_b=a_ln_b, a_w_out=bf(a_w_out),
        a_mix_long=[_gmlp_mix_mats(a_w_s[j], a_b_s[j], x_prompt.shape[1]) for j in range(n_a)],
        a_mix_short=[_gmlp_mix_mats(a_w_s[j], a_b_s[j], x_sample.shape[1]) for j in range(n_a)],
        c_w_in=bf(c_w_in), c_conv_w=c_conv_w, c_w_out=bf(c_w_out),
        b=[dict(
            proj=[b_mu[j], bf(b_w_rkv[j]), _row(b_w0[j]), bf(b_w1[j]), bf(b_w2[j]), _row(b_a0[j]),
                  bf(b_a1[j]), bf(b_a2[j]), bf(b_g1[j]), bf(b_g2[j]), _row(b_k_k[j]), _row(b_k_a[j]),
                  _row(b_r_k[j]), e, et],
            post=[_row(b_lnx_g[j]), _row(b_lnx_b[j]), bf(b_w_o[j]), e, et],
        ) for j in range(n_b)],
    )
    bp = x_prompt.shape[0]
    n_c = c_w_in.shape[0]
    zero_wkv = jnp.zeros((n_b, bp) + state_b_wkv.shape[2:], state_b_wkv.dtype)
    zero_shift = jnp.zeros((n_b, bp, D_MODEL), state_b_shift.dtype)
    zero_conv = jnp.zeros((n_c, bp, CONV_W - 1, D_MODEL), state_c_conv.dtype)
    y_p, _, wkv_p, shift_p, conv_p = _run_trunk(x_prompt, p_prompt, zero_wkv, zero_shift, zero_conv, W, False)
    y_s, a_v_s, wkv_s, shift_s, conv_s = _run_trunk(x_sample, p_sample, state_b_wkv, state_b_shift, state_c_conv, W, True)
    return (y_p, y_s, a_v_s, wkv_p, shift_p, conv_p, wkv_s, shift_s, conv_s)
```

```python
import functools
import math

import jax
import jax.numpy as jnp
from jax import lax
from jax.experimental import pallas as pl
from jax.experimental.pallas import tpu as pltpu

F32 = jnp.float32
BF16 = jnp.bfloat16

D_MODEL = 1024
DEPTH = 4
N_MIXERS = 3
CHUNK = 128
A_INNER = 2 * D_MODEL
A_HEADS = 8
A_GROUP = A_INNER // A_HEADS
HEAD_SIZE = 64
B_HEADS = D_MODEL // HEAD_SIZE
CONV_W = 3
D_FF = 2816
PLE_DIM = 256
ALPHA = (2 * DEPTH) ** 0.25
LN_EPS = 1e-5
GN_EPS = 64e-5

LANES = 128
ROW_TILE = 512
SCAN_CHUNK = 64
HEAD_PAIRS = B_HEADS // 2
VMEM_LIMIT = 56 * 1024 * 1024

NN = (((1,), (0,)), ((), ()))
NT = (((1,), (1,)), ((), ()))
TN = (((0,), (0,)), ((), ()))


def _dot(a, b, dims=NN):
    return lax.dot_general(a, b, dims, preferred_element_type=F32)


def _split2(x):
    hi = x.astype(BF16)
    lo = (x - hi.astype(F32)).astype(BF16)
    return hi, lo


def _split3(x):
    hi = x.astype(BF16)
    r1 = x - hi.astype(F32)
    mid = r1.astype(BF16)
    lo = (r1 - mid.astype(F32)).astype(BF16)
    return hi, mid, lo


def _layer_norm(x, g, b, eps=LN_EPS):
    mu = jnp.mean(x, axis=-1, keepdims=True)
    xc = x - mu
    var = jnp.mean(xc * xc, axis=-1, keepdims=True)
    return xc * lax.rsqrt(var + eps) * g + b


def _sigmoid(x):
    return 1.0 / (1.0 + jnp.exp(-x))


def _const_spec(shape):
    nd = len(shape)
    return pl.BlockSpec(shape, lambda *_: (0,) * nd, pipeline_mode=pl.Buffered(1))


def _row_spec(width, tm=ROW_TILE):
    return pl.BlockSpec((tm, width), lambda i: (i, 0))


def _params(sem):
    return pltpu.CompilerParams(dimension_semantics=sem, vmem_limit_bytes=VMEM_LIMIT)


FF_SPLIT = 2
FF_BLOCK = D_FF // FF_SPLIT


def _ffn_kernel(with_ple, x_ref, wi_ref, wo_ref, g_ref, b_ref, *rest):
    if with_ple:
        p_ref, wg_ref, wp_ref, o_ref = rest
    else:
        (o_ref,) = rest
    x = x_ref[...]
    xb = x.astype(BF16)
    acc = None
    for c in range(FF_SPLIT):
        lo = c * FF_BLOCK
        gate = _dot(xb, wi_ref[:, lo:lo + FF_BLOCK])
        up = _dot(xb, wi_ref[:, D_FF + lo:D_FF + lo + FF_BLOCK])
        act = (gate * _sigmoid(gate) * up).astype(BF16)
        part = _dot(act, wo_ref[lo:lo + FF_BLOCK, :])
        acc = part if acc is None else acc + part
    y = _layer_norm(ALPHA * x + 0.5 * acc, g_ref[...], b_ref[...])
    if with_ple:
        gate = _sigmoid(_dot(y.astype(BF16), wg_ref[...]))
        y = y + gate * _dot(p_ref[...].astype(BF16), wp_ref[...])
    o_ref[...] = y


def _ffn(x, wi, wo, g, b, ple=None):
    n = x.shape[0]
    tm = min(ROW_TILE, n)
    args = [x, wi, wo, g, b]
    specs = [_row_spec(D_MODEL, tm), _const_spec(wi.shape), _const_spec(wo.shape),
             _const_spec(g.shape), _const_spec(b.shape)]
    if ple is not None:
        p, wg, wp = ple
        args += [p, wg, wp]
        specs += [_row_spec(PLE_DIM, tm), _const_spec(wg.shape), _const_spec(wp.shape)]
    return pl.pallas_call(
        functools.partial(_ffn_kernel, ple is not None),
        grid=(n // tm,),
        in_specs=specs,
        out_specs=_row_spec(D_MODEL, tm),
        out_shape=jax.ShapeDtypeStruct((n, D_MODEL), F32),
        compiler_params=_params(("parallel",)),
        name="ffn_ple" if ple is not None else "ffn",
    )(*args)


def _gmlp_kernel(keep_v, x_ref, wi_ref, bi_ref, lg_ref, lb_ref, ws_ref, bs_ref, wo_ref,
                 g_ref, b_ref, *rest):
    if keep_v:
        o_ref, v_ref, y_ref = rest
    else:
        o_ref, y_ref = rest
    x = x_ref[...]
    tm = x.shape[0]
    z = _dot(x.astype(BF16), wi_ref[...]) + bi_ref[...]
    z = 0.5 * z * (1.0 + lax.erf(z * (1.0 / math.sqrt(2.0))))
    u = z[:, :A_INNER]
    v = _layer_norm(z[:, A_INNER:], lg_ref[...], lb_ref[...])
    if keep_v:
        v_ref[...] = v
    vb = v.astype(BF16)
    for c in range(tm // CHUNK):
        rows = slice(c * CHUNK, (c + 1) * CHUNK)
        for h in range(A_HEADS):
            cols = slice(h * A_GROUP, (h + 1) * A_GROUP)
            mixed = _dot(ws_ref[h], vb[rows, cols]) + bs_ref[:, h:h + 1]
            y_ref[rows, cols] = (u[rows, cols] * mixed).astype(BF16)
    out = _dot(y_ref[...], wo_ref[...])
    o_ref[...] = _layer_norm(ALPHA * x + out, g_ref[...], b_ref[...])


def _gmlp(x, wi, bi, lg, lb, ws, bs, wo, g, b, keep_v):
    n = x.shape[0]
    tm = min(ROW_TILE, n)
    consts = [wi, bi, lg, lb, ws, bs, wo, g, b]
    out_shape = [jax.ShapeDtypeStruct((n, D_MODEL), F32)]
    out_specs = [_row_spec(D_MODEL, tm)]
    if keep_v:
        out_shape.append(jax.ShapeDtypeStruct((n, A_INNER), F32))
        out_specs.append(_row_spec(A_INNER, tm))
    res = pl.pallas_call(
        functools.partial(_gmlp_kernel, keep_v),
        grid=(n // tm,),
        in_specs=[_row_spec(D_MODEL, tm)] + [_const_spec(c.shape) for c in consts],
        out_specs=out_specs,
        out_shape=out_shape,
        scratch_shapes=[pltpu.VMEM((tm, A_INNER), BF16)],
        compiler_params=_params(("parallel",)),
        name="gmlp",
    )(x, *consts)
    return (res[0], res[1]) if keep_v else (res[0], None)


def _conv_tail(x, bg, conv, wo_ref, g_ref, b_ref, o_ref):
    out = _dot((bg * conv).astype(BF16), wo_ref[...])
    o_ref[...] = _layer_norm(ALPHA * x + out, g_ref[...], b_ref[...])


def _conv_long_kernel(tiles_per_seq, x_ref, buf_ref, wi_ref, cw_ref, wo_ref, g_ref, b_ref,
                      o_ref, tail_ref, carry_ref):
    x = x_ref[...]
    tm = x.shape[0]
    h3 = _dot(x.astype(BF16), wi_ref[...])
    bg = h3[:, :D_MODEL]
    z = h3[:, D_MODEL:2 * D_MODEL] * h3[:, 2 * D_MODEL:]

    @pl.when(pl.program_id(0) % tiles_per_seq == 0)
    def _():
        carry_ref[...] = buf_ref[0]

    row = lax.broadcasted_iota(jnp.int32, (tm, D_MODEL), 0)
    prev1 = carry_ref[7:8, :]
    prev2 = carry_ref[6:7, :]
    z1 = jnp.where(row == 0, prev1, pltpu.roll(z, 1, 0))
    z2 = jnp.where(row == 0, prev2, jnp.where(row == 1, prev1, pltpu.roll(z, 2, 0)))
    conv = cw_ref[0:1, :] * z2 + cw_ref[1:2, :] * z1 + cw_ref[2:3, :] * z
    tail = z[tm - 8:, :]
    carry_ref[...] = tail
    tail_ref[0] = tail
    _conv_tail(x, bg, conv, wo_ref, g_ref, b_ref, o_ref)


def _conv_short_kernel(seq, x_ref, h1_ref, h2_ref, wi_ref, cw_ref, wo_ref, g_ref, b_ref,
                       o_ref, z_ref):
    x = x_ref[...]
    tm = x.shape[0]
    h3 = _dot(x.astype(BF16), wi_ref[...])
    bg = h3[:, :D_MODEL]
    z = h3[:, D_MODEL:2 * D_MODEL] * h3[:, 2 * D_MODEL:]
    t = lax.broadcasted_iota(jnp.int32, (tm, D_MODEL), 0) % seq
    z1 = jnp.where(t >= 1, pltpu.roll(z, 1, 0), h1_ref[...])
    z2 = jnp.where(t >= 2, pltpu.roll(z, 2, 0), h2_ref[...])
    conv = cw_ref[0:1, :] * z2 + cw_ref[1:2, :] * z1 + cw_ref[2:3, :] * z
    z_ref[...] = z
    _conv_tail(x, bg, conv, wo_ref, g_ref, b_ref, o_ref)


def _conv_mixer(x, buf_prev, seq, wi, cw, wo, g, b):
    n = x.shape[0]
    bn = n // seq
    tm = min(ROW_TILE, n)
    consts = [wi, cw, wo, g, b]
    const_specs = [_const_spec(c.shape) for c in consts]
    if seq >= tm:
        tiles_per_seq = seq // tm
        buf8 = jnp.concatenate([jnp.zeros((bn, 6, D_MODEL), F32), buf_prev], axis=1)
        out, tails = pl.pallas_call(
            functools.partial(_conv_long_kernel, tiles_per_seq),
            grid=(n // tm,),
            in_specs=[_row_spec(D_MODEL, tm),
                      pl.BlockSpec((1, 8, D_MODEL), lambda i: (i // tiles_per_seq, 0, 0))] + const_specs,
            out_specs=[_row_spec(D_MODEL, tm), pl.BlockSpec((1, 8, D_MODEL), lambda i: (i, 0, 0))],
            out_shape=[jax.ShapeDtypeStruct((n, D_MODEL), F32),
                       jax.ShapeDtypeStruct((n // tm, 8, D_MODEL), F32)],
            scratch_shapes=[pltpu.VMEM((8, D_MODEL), F32)],
            compiler_params=_params(("arbitrary",)),
            name="conv_long",
        )(x, buf8, *consts)
        new_buf = tails[tiles_per_seq - 1::tiles_per_seq, 6:8, :]
        return out, new_buf
    assert tm % seq == 0 and seq >= CONV_W - 1
    t = jnp.arange(n) % seq
    last = jnp.repeat(buf_prev[:, 1, :], seq, axis=0)
    first = jnp.repeat(buf_prev[:, 0, :], seq, axis=0)
    h1 = jnp.where((t == 0)[:, None], last, 0.0)
    h2 = jnp.where((t == 0)[:, None], first, jnp.where((t == 1)[:, None], last, 0.0))
    out, z = pl.pallas_call(
        functools.partial(_conv_short_kernel, seq),
        grid=(n // tm,),
        in_specs=[_row_spec(D_MODEL, tm)] * 3 + const_specs,
        out_specs=[_row_spec(D_MODEL, tm)] * 2,
        out_shape=[jax.ShapeDtypeStruct((n, D_MODEL), F32)] * 2,
        compiler_params=_params(("parallel",)),
        name="conv_short",
    )(x, h1, h2, *consts)
    new_buf = z.reshape(bn, seq, D_MODEL)[:, seq - (CONV_W - 1):, :]
    return out, new_buf


def _head_sum_bcast(x, e_ref, et_ref):
    hi, lo = _split2(x)
    s = _dot(hi, e_ref[...]) + _dot(lo, e_ref[...])
    s_hi, s_mid, s_lo = _split3(s)
    return _dot(s_hi, et_ref[...]) + (_dot(s_mid, et_ref[...]) + _dot(s_lo, et_ref[...]))


def _rwkv_proj_kernel(x_ref, xp_ref, mu_ref, wrkv_ref, w0_ref, w1_ref, w2_ref, a0_ref, a1_ref,
                      a2_ref, g1_ref, g2_ref, kk_ref, ka_ref, rk_ref, e_ref, et_ref,
                      r_out, lw_out, k_out, v_out, c_out, b_out, bonus_out, g_out):
    x = x_ref[...]
    xx = xp_ref[...] - x
    mix = lambda i: (x + xx * mu_ref[i:i + 1, :]).astype(BF16)
    r = _dot(mix(0), wrkv_ref[0])
    k = _dot(mix(2), wrkv_ref[1])
    v = _dot(mix(3), wrkv_ref[2])
    zw = w0_ref[...] + _dot(jnp.tanh(_dot(mix(1), w1_ref[...])).astype(BF16), w2_ref[...])
    lw_out[...] = -_sigmoid(zw) * math.exp(-0.5)
    a = _sigmoid(a0_ref[...] + _dot(_dot(mix(4), a1_ref[...]).astype(BF16), a2_ref[...]))
    g_out[...] = _dot(_sigmoid(_dot(mix(5), g1_ref[...])).astype(BF16), g2_ref[...])
    kk = k * kk_ref[...]
    norm = jnp.sqrt(_head_sum_bcast(kk * kk, e_ref, et_ref))
    c = kk / jnp.maximum(norm, 1e-12)
    kmod = k * (1.0 + (a - 1.0) * ka_ref[...])
    r_out[...] = r
    k_out[...] = kmod
    v_out[...] = v
    c_out[...] = c
    b_out[...] = c * a
    bonus_out[...] = _head_sum_bcast(r * kmod * rk_ref[...], e_ref, et_ref) * v


def _rwkv_proj(x, xprev, consts):
    n = x.shape[0]
    tm = min(ROW_TILE // 2, n)
    return pl.pallas_call(
        _rwkv_proj_kernel,
        grid=(n // tm,),
        in_specs=[_row_spec(D_MODEL, tm)] * 2 + [_const_spec(c.shape) for c in consts],
        out_specs=[_row_spec(D_MODEL, tm)] * 8,
        out_shape=[jax.ShapeDtypeStruct((n, D_MODEL), F32)] * 8,
        compiler_params=_params(("parallel",)),
        name="rwkv_proj",
    )(x, xprev, *consts)


def _cat(parts, axis):
    return tuple(jnp.concatenate([p[i] for p in parts], axis=axis) for i in range(2))


def _dot3s(a, b, dims=NN):
    (ah, al), (bh, bl) = a, b
    return _dot(ah, bh, dims) + (_dot(ah, bl, dims) + _dot(al, bh, dims))


def _pair_rows(xs, lane_lo):
    out = []
    for t in xs:
        zero = jnp.zeros_like(t)
        out.append(jnp.concatenate([jnp.where(lane_lo, t, zero), jnp.where(lane_lo, zero, t)], axis=0))
    return tuple(out)


def _diag_blocks_mask():
    sq_row = lax.broadcasted_iota(jnp.int32, (LANES, LANES), 0) < HEAD_SIZE
    sq_col = lax.broadcasted_iota(jnp.int32, (LANES, LANES), 1) < HEAD_SIZE
    return sq_row == sq_col


def _scan_chunk(refs, y_ref, sub, qs_fn, state_fn):
    C = SCAN_CHUNK
    pairs = range(HEAD_PAIRS)
    shift = int(math.log2(sub))
    row = lax.broadcasted_iota(jnp.int32, (C, 2 * C), 0)
    col = lax.broadcasted_iota(jnp.int32, (C, 2 * C), 1) & (C - 1)
    same = (row >> shift) == (col >> shift)
    strict = same & (row > col)
    incl = same & (row >= col)
    lane_lo = lax.broadcasted_iota(jnp.int32, (C, LANES), 1) < HEAD_SIZE
    r2 = lax.broadcasted_iota(jnp.int32, (C, C), 0)
    c2 = lax.broadcasted_iota(jnp.int32, (C, C), 1)
    tri = jnp.where(((r2 >> shift) == (c2 >> shift)) & (r2 >= c2), 1.0, 0.0).astype(BF16)
    lanes = [slice(p * LANES, (p + 1) * LANES) for p in pairs]
    r, lw, k, v, c, b = ([ref[:, s] for s in lanes] for ref in refs)

    cum = []
    for p in pairs:
        l_hi, l_mid, l_lo = _split3(lw[p])
        cum.append(_dot(tri, l_hi) + (_dot(tri, l_mid) + _dot(tri, l_lo)))
    p_inc = [jnp.exp(cum[p]) for p in pairs]
    p_inv = [jnp.exp(-cum[p]) for p in pairs]
    p_exc = [jnp.exp(cum[p] - lw[p]) for p in pairs]
    q = [_split2(jnp.concatenate([c[p] * p_exc[p], r[p] * p_inc[p]], axis=0)) for p in pairs]
    bt = [_split2(b[p] * p_inv[p]) for p in pairs]
    kt = [_split2(k[p] * p_inv[p]) for p in pairs]
    vs = [_split2(v[p]) for p in pairs]
    kb = [_cat([_pair_rows(bt[p], lane_lo), _pair_rows(kt[p], lane_lo)], 0) for p in pairs]
    gram = [_dot3s(q[p], kb[p], NT) for p in pairs]
    l_cb = [jnp.where(strict, gram[p][:C, :2 * C], 0.0) for p in pairs]
    l_ck = [jnp.where(strict, gram[p][:C, 2 * C:], 0.0) for p in pairs]
    a_rb = [jnp.where(incl, gram[p][C:, :2 * C], 0.0) for p in pairs]
    a_rk = [jnp.where(incl, gram[p][C:, 2 * C:], 0.0) for p in pairs]
    qs = qs_fn(q)
    v_rows = [_pair_rows(vs[p], lane_lo) for p in pairs]
    u = [-(qs[p][:C] + _dot3s(_split2(l_ck[p]), v_rows[p])) for p in pairs]
    m = [_split2(l_cb[p]) for p in pairs]
    u = [u[p] - _dot3s(m[p], _pair_rows(_split2(u[p]), lane_lo)) for p in pairs]
    for _ in range(shift - 1):
        m = [_split2(_dot3s(m[p], _pair_rows(m[p], lane_lo))) for p in pairs]
        u = [u[p] + _dot3s(m[p], _pair_rows(_split2(u[p]), lane_lo)) for p in pairs]
    us = [_split2(u[p]) for p in pairs]
    for p in pairs:
        a = _split2(jnp.concatenate([a_rb[p], a_rk[p]], axis=1))
        uv_rows = _cat([_pair_rows(us[p], lane_lo), v_rows[p]], 0)
        y_ref[:, lanes[p]] = qs[p][C:] + _dot3s(a, uv_rows)
    state_fn(u, v, us, vs, bt, kt, p_inc)


def _rwkv_scan_kernel(r_ref, lw_ref, k_ref, v_ref, c_ref, b_ref, s0_ref, y_ref, sfin_ref, s_ref):
    C = SCAN_CHUNK
    j = pl.program_id(1)

    @pl.when(j == 0)
    def _():
        s_ref[...] = s0_ref[0]

    diag = _diag_blocks_mask()

    def qs_fn(q):
        return [_dot3s(q[p], _split2(s_ref[p]), NT) for p in range(HEAD_PAIRS)]

    def state_fn(u, v, us, vs, bt, kt, p_inc):
        for p in range(HEAD_PAIRS):
            upd = _dot3s(_cat([us[p], vs[p]], 0), _cat([bt[p], kt[p]], 0), TN)
            s_ref[p] = (s_ref[p] + jnp.where(diag, upd, 0.0)) * p_inc[p][C - 1:C, :]

    _scan_chunk((r_ref, lw_ref, k_ref, v_ref, c_ref, b_ref), y_ref, C, qs_fn, state_fn)

    @pl.when(j == pl.num_programs(1) - 1)
    def _():
        sfin_ref[0] = s_ref[...]


def _rwkv_scan(r, lw, k, v, c, b, s0_bd, bn, seq):
    n_chunks = seq // SCAN_CHUNK
    row_spec = pl.BlockSpec((SCAN_CHUNK, D_MODEL), lambda i, j: (i * n_chunks + j, 0))
    st_spec = pl.BlockSpec((1, HEAD_PAIRS, LANES, LANES), lambda i, j: (i, 0, 0, 0))
    return pl.pallas_call(
        _rwkv_scan_kernel,
        grid=(bn, n_chunks),
        in_specs=[row_spec] * 6 + [st_spec],
        out_specs=[row_spec, st_spec],
        out_shape=[jax.ShapeDtypeStruct((bn * seq, D_MODEL), F32),
                   jax.ShapeDtypeStruct((bn, HEAD_PAIRS, LANES, LANES), F32)],
        scratch_shapes=[pltpu.VMEM((HEAD_PAIRS, LANES, LANES), F32)],
        compiler_params=_params(("parallel", "arbitrary")),
        name="rwkv_scan",
    )(r, lw, k, v, c, b, s0_bd)


def _rwkv_scan_short_kernel(sub, r_ref, lw_ref, k_ref, v_ref, c_ref, b_ref, s0_ref, y_ref, sfin_ref):
    C = SCAN_CHUNK
    nb = C // sub
    shift = int(math.log2(sub))
    diag = _diag_blocks_mask()
    row_seq = (lax.broadcasted_iota(jnp.int32, (2 * C, LANES), 0) & (C - 1)) >> shift
    lane_seq = (lax.broadcasted_iota(jnp.int32, (LANES, 2 * C), 1) & (C - 1)) >> shift

    def qs_fn(q):
        out = []
        for p in range(HEAD_PAIRS):
            s_stack = _cat([_split2(s0_ref[i, p]) for i in range(nb)], 1)
            q_wide = tuple(jnp.concatenate([jnp.where(row_seq == i, t, jnp.zeros_like(t)) for i in range(nb)],
                                           axis=1) for t in q[p])
            out.append(_dot3s(q_wide, s_stack, NT))
        return out

    def state_fn(u, v, us, vs, bt, kt, p_inc):
        for p in range(HEAD_PAIRS):
            uvt = jnp.concatenate([u[p], v[p]], axis=0).T
            stack = _split2(jnp.concatenate([jnp.where(lane_seq == i, uvt, 0.0) for i in range(nb)], axis=0))
            upd = _dot3s(stack, _cat([bt[p], kt[p]], 0))
            for i in range(nb):
                last = i * sub + sub - 1
                sfin_ref[i, p] = ((s0_ref[i, p] + jnp.where(diag, upd[i * LANES:(i + 1) * LANES], 0.0))
                                  * p_inc[p][last:last + 1, :])

    _scan_chunk((r_ref, lw_ref, k_ref, v_ref, c_ref, b_ref), y_ref, sub, qs_fn, state_fn)


def _rwkv_scan_short(r, lw, k, v, c, b, s0_bd, bn, seq):
    nb = SCAN_CHUNK // seq
    row_spec = pl.BlockSpec((SCAN_CHUNK, D_MODEL), lambda i: (i, 0))
    st_spec = pl.BlockSpec((nb, HEAD_PAIRS, LANES, LANES), lambda i: (i, 0, 0, 0))
    return pl.pallas_call(
        functools.partial(_rwkv_scan_short_kernel, seq),
        grid=(bn // nb,),
        in_specs=[row_spec] * 6 + [st_spec],
        out_specs=[row_spec, st_spec],
        out_shape=[jax.ShapeDtypeStruct((bn * seq, D_MODEL), F32),
                   jax.ShapeDtypeStruct((bn, HEAD_PAIRS, LANES, LANES), F32)],
        compiler_params=_params(("parallel",)),
        name="rwkv_scan_short",
    )(r, lw, k, v, c, b, s0_bd)


def _rwkv_post_kernel(x_ref, y_ref, bonus_ref, gate_ref, lg_ref, lb_ref, wo_ref, e_ref, et_ref,
                      g_ref, b_ref, o_ref):
    ys = y_ref[...]
    inv = 1.0 / HEAD_SIZE
    m = _head_sum_bcast(ys, e_ref, et_ref) * inv
    yc = ys - m
    var = _head_sum_bcast(yc * yc, e_ref, et_ref) * inv
    yn = yc * lax.rsqrt(var + GN_EPS) * lg_ref[...] + lb_ref[...]
    out = _dot(((yn + bonus_ref[...]) * gate_ref[...]).astype(BF16), wo_ref[...])
    o_ref[...] = _layer_norm(ALPHA * x_ref[...] + out, g_ref[...], b_ref[...])


def _rwkv_post(x, ys, bonus, gate, consts):
    n = x.shape[0]
    tm = min(ROW_TILE, n)
    return pl.pallas_call(
        _rwkv_post_kernel,
        grid=(n // tm,),
        in_specs=[_row_spec(D_MODEL, tm)] * 4 + [_const_spec(c.shape) for c in consts],
        out_specs=_row_spec(D_MODEL, tm),
        out_shape=jax.ShapeDtypeStruct((n, D_MODEL), F32),
        compiler_params=_params(("parallel",)),
        name="rwkv_post",
    )(x, ys, bonus, gate, *consts)


def _to_pair_blocks(s):
    bn = s.shape[0]
    s = s.reshape(bn, HEAD_PAIRS, 2, HEAD_SIZE, HEAD_SIZE)
    zero = jnp.zeros_like(s[:, :, 0])
    top = jnp.concatenate([s[:, :, 0], zero], axis=-1)
    bot = jnp.concatenate([zero, s[:, :, 1]], axis=-1)
    return jnp.concatenate([top, bot], axis=-2)


def _from_pair_blocks(s_bd):
    bn = s_bd.shape[0]
    s = jnp.stack([s_bd[:, :, :HEAD_SIZE, :HEAD_SIZE], s_bd[:, :, HEAD_SIZE:, HEAD_SIZE:]], axis=2)
    return s.reshape(bn, B_HEADS, HEAD_SIZE, HEAD_SIZE)


def _rwkv_mixer(x, shift_prev, s0, seq, w, g, b):
    n = x.shape[0]
    bn = n // seq
    x3 = x.reshape(bn, seq, D_MODEL)
    xprev = jnp.concatenate([shift_prev[:, None, :], x3[:, :-1, :]], axis=1).reshape(n, D_MODEL)
    r, lw, k, v, c, bb, bonus, gate = _rwkv_proj(x, xprev, w["proj"])
    if seq % SCAN_CHUNK == 0:
        scan = _rwkv_scan
    else:
        assert SCAN_CHUNK % seq == 0 and seq & (seq - 1) == 0 and n % SCAN_CHUNK == 0
        scan = _rwkv_scan_short
    ys, s_bd = scan(r, lw, k, v, c, bb, _to_pair_blocks(s0), bn, seq)
    out = _rwkv_post(x, ys, bonus, gate, w["post"] + [g, b])
    return out, x3[:, -1, :], _from_pair_blocks(s_bd)


def _row(v):
    return v.reshape(1, -1)


def _run_trunk(x3, p4, b_wkv, b_shift, c_conv, W, keep_v):
    bn, seq, _ = x3.shape
    n = bn * seq
    x = x3.reshape(n, D_MODEL)
    new_v, new_wkv, new_shift, new_conv = [], [], [], []
    for i in range(DEPTH):
        j, kind = divmod(i, N_MIXERS)
        ln_g = lambda s: _row(W["ln_g"][i, s])
        ln_b = lambda s: _row(W["ln_b"][i, s])
        x = _ffn(x, W["ffn_w_in"][i][0], W["ffn_w_out"][i][0], ln_g(0), ln_b(0))
        if kind == 0:
            mix = W["a_mix_long"][j] if seq >= CHUNK else W["a_mix_short"][j]
            x, v = _gmlp(x, W["a_w_in"][j], _row(W["a_b_in"][j]), _row(W["a_ln_g"][j]), _row(W["a_ln_b"][j]),
                         mix[0], mix[1], W["a_w_out"][j], ln_g(1), ln_b(1), keep_v)
            if keep_v:
                new_v.append(v.reshape(bn, seq, A_INNER))
        elif kind == 1:
            x, sh, s = _rwkv_mixer(x, b_shift[j], b_wkv[j], seq, W["b"][j], ln_g(1), ln_b(1))
            new_shift.append(sh)
            new_wkv.append(s)
        else:
            x, buf = _conv_mixer(x, c_conv[j], seq, W["c_w_in"][j], W["c_conv_w"][j], W["c_w_out"][j],
                                 ln_g(1), ln_b(1))
            new_conv.append(buf)
        x = _ffn(x, W["ffn_w_in"][i][1], W["ffn_w_out"][i][1], ln_g(2), ln_b(2),
                 ple=(p4[i].reshape(n, PLE_DIM), W["ple_w_gate"][i], W["ple_w_proj"][i]))
    a_state = jnp.stack(new_v) if keep_v else None
    return x.reshape(bn, seq, D_MODEL), a_state, jnp.stack(new_wkv), jnp.stack(new_shift), jnp.stack(new_conv)


def _gmlp_mix_mats(w_s, b_s, seq):
    l = min(seq, CHUNK)
    ws = jnp.where(jnp.tril(jnp.ones((l, l), dtype=bool)), w_s[:, :l, :l], 0.0)
    reps = CHUNK // l
    if reps > 1:
        eye = jnp.eye(reps, dtype=F32)
        ws = jnp.einsum("ab,hts->hatbs", eye, ws).reshape(A_HEADS, CHUNK, CHUNK)
    bias = jnp.tile(b_s[:, :l].T, (reps, 1))
    return ws.astype(BF16), bias


def kernel(x_prompt, x_sample, state_b_wkv, state_b_shift, state_c_conv, p_prompt, p_sample, ln_g, ln_b, ffn_w_in, ffn_w_out, ple_w_gate, ple_w_proj, a_w_in, a_b_in, a_ln_g, a_ln_b, a_w_s, a_b_s, a_w_out, b_mu, b_w_rkv, b_w0, b_w1, b_w2, b_a0, b_a1, b_a2, b_g1, b_g2, b_k_k, b_k_a, b_r_k, b_lnx_g, b_lnx_b, b_w_o, c_w_in, c_conv_w, c_w_out):
    bf = lambda w: w.astype(BF16)
    head_of_lane = jnp.arange(D_MODEL) // HEAD_SIZE
    e = (head_of_lane[:, None] == jnp.arange(LANES)[None, :]).astype(BF16)
    et = e.T
    n_b = b_mu.shape[0]
    n_a = a_w_in.shape[0]
    W = dict(
        ln_g=ln_g, ln_b=ln_b,
        ffn_w_in=bf(ffn_w_in), ffn_w_out=bf(ffn_w_out),
        ple_w_gate=bf(ple_w_gate), ple_w_proj=bf(ple_w_proj),
        a_w_in=bf(a_w_in), a_b_in=a_b_in, a_ln_g=a_ln_g, a_ln_b=a_ln_b, a_w_out=bf(a_w_out),
        a_mix_long=[_gmlp_mix_mats(a_w_s[j], a_b_s[j], x_prompt.shape[1]) for j in range(n_a)],
        a_mix_short=[_gmlp_mix_mats(a_w_s[j], a_b_s[j], x_sample.shape[1]) for j in range(n_a)],
        c_w_in=bf(c_w_in), c_conv_w=c_conv_w, c_w_out=bf(c_w_out),
        b=[dict(
            proj=[b_mu[j], bf(b_w_rkv[j]), _row(b_w0[j]), bf(b_w1[j]), bf(b_w2[j]), _row(b_a0[j]),
                  bf(b_a1[j]), bf(b_a2[j]), bf(b_g1[j]), bf(b_g2[j]), _row(b_k_k[j]), _row(b_k_a[j]),
                  _row(b_r_k[j]), e, et],
            post=[_row(b_lnx_g[j]), _row(b_lnx_b[j]), bf(b_w_o[j]), e, et],
        ) for j in range(n_b)],
    )
    bp = x_prompt.shape[0]
    n_c = c_w_in.shape[0]
    zero_wkv = jnp.zeros((n_b, bp) + state_b_wkv.shape[2:], state_b_wkv.dtype)
    zero_shift = jnp.zeros((n_b, bp, D_MODEL), state_b_shift.dtype)
    zero_conv = jnp.zeros((n_c, bp, CONV_W - 1, D_MODEL), state_c_conv.dtype)
    y_p, _, wkv_p, shift_p, conv_p = _run_trunk(x_prompt, p_prompt, zero_wkv, zero_shift, zero_conv, W, False)
    y_s, a_v_s, wkv_s, shift_s, conv_s = _run_trunk(x_sample, p_sample, state_b_wkv, state_b_shift, state_c_conv, W, True)
    return (y_p, y_s, a_v_s, wkv_p, shift_p, conv_p, wkv_s, shift_s, conv_s)
```

```python
import collections
import functools
import math

import jax
import jax.numpy as jnp
from jax import lax
from jax.experimental import pallas as pl
from jax.experimental.pallas import tpu as pltpu

F32 = jnp.float32
BF16 = jnp.bfloat16

D_MODEL = 1024
DEPTH = 4
N_MIXERS = 3
CHUNK = 128
A_INNER = 2 * D_MODEL
A_HEADS = 8
A_GROUP = A_INNER // A_HEADS
HEAD_SIZE = 64
B_HEADS = D_MODEL // HEAD_SIZE
CONV_W = 3
D_FF = 2816
PLE_DIM = 256
ALPHA = (2 * DEPTH) ** 0.25
LN_EPS = 1e-5
GN_EPS = 64e-5

LANES = 128
MXU_DIM = 256
ROW_TILE = 512
SCAN_CHUNK = 64
HEAD_PAIRS = B_HEADS // 2
VMEM_LIMIT = 56 * 1024 * 1024

NN = (((1,), (0,)), ((), ()))
NT = (((1,), (1,)), ((), ()))
TN = (((0,), (0,)), ((), ()))


def _dot(a, b, dims=NN):
    return lax.dot_general(a, b, dims, preferred_element_type=F32)


def _split2(x):
    hi = x.astype(BF16)
    lo = (x - hi.astype(F32)).astype(BF16)
    return hi, lo


def _split3(x):
    hi = x.astype(BF16)
    r1 = x - hi.astype(F32)
    mid = r1.astype(BF16)
    lo = (r1 - mid.astype(F32)).astype(BF16)
    return hi, mid, lo


def _layer_norm(x, g, b, eps=LN_EPS):
    mu = jnp.mean(x, axis=-1, keepdims=True)
    xc = x - mu
    var = jnp.mean(xc * xc, axis=-1, keepdims=True)
    return xc * lax.rsqrt(var + eps) * g + b


def _sigmoid(x):
    return 1.0 / (1.0 + jnp.exp(-x))


class _Sel(collections.namedtuple("_Sel", ["arr", "idx"])):
    def const_spec(self):
        k = len(self.idx)
        rest = self.arr.shape[k:]
        idx = self.idx
        return pl.BlockSpec((None,) * k + rest, lambda *_: idx + (0,) * len(rest),
                            pipeline_mode=pl.Buffered(1))

    def row_spec(self, tm):
        k = len(self.idx)
        idx = self.idx
        return pl.BlockSpec((None,) * k + (tm, self.arr.shape[-1]), lambda i: idx + (i, 0))


def _sel(arr, *idx):
    return _Sel(arr, tuple(idx))


def _row_spec(width, tm=ROW_TILE):
    return pl.BlockSpec((tm, width), lambda i: (i, 0))


def _params(sem):
    return pltpu.CompilerParams(dimension_semantics=sem, vmem_limit_bytes=VMEM_LIMIT)


FF_BLOCKS = ((0, 6 * MXU_DIM), (6 * MXU_DIM, D_FF))
assert D_FF % MXU_DIM == 0


def _ffn_kernel(with_ple, x_ref, wi_ref, wo_ref, g_ref, b_ref, *rest):
    if with_ple:
        p_ref, wg_ref, wp_ref, o_ref = rest
    else:
        (o_ref,) = rest
    x = x_ref[...]
    xb = x.astype(BF16)
    acc = None
    for lo, hi in FF_BLOCKS:
        gate = _dot(xb, wi_ref[:, lo:hi])
        up = _dot(xb, wi_ref[:, D_FF + lo:D_FF + hi])
        act = (gate * _sigmoid(gate) * up).astype(BF16)
        part = _dot(act, wo_ref[lo:hi, :])
        acc = part if acc is None else acc + part
    y = _layer_norm(ALPHA * x + 0.5 * acc, g_ref[...], b_ref[...])
    if with_ple:
        gate = _sigmoid(_dot(y.astype(BF16), wg_ref[...]))
        y = y + gate * _dot(p_ref[...].astype(BF16), wp_ref[...])
    o_ref[...] = y


def _ffn(x, wi, wo, g, b, ple=None):
    n = x.shape[0]
    tm = min(ROW_TILE, n)
    consts = [wi, wo, g, b]
    args = [x] + [c.arr for c in consts]
    specs = [_row_spec(D_MODEL, tm)] + [c.const_spec() for c in consts]
    if ple is not None:
        p, wg, wp = ple
        args += [p.arr, wg.arr, wp.arr]
        specs += [p.row_spec(tm), wg.const_spec(), wp.const_spec()]
    return pl.pallas_call(
        functools.partial(_ffn_kernel, ple is not None),
        grid=(n // tm,),
        in_specs=specs,
        out_specs=_row_spec(D_MODEL, tm),
        out_shape=jax.ShapeDtypeStruct((n, D_MODEL), F32),
        compiler_params=_params(("parallel",)),
        name="ffn_ple" if ple is not None else "ffn",
    )(*args)


def _gmlp_kernel(keep_v, x_ref, wi_ref, bi_ref, lg_ref, lb_ref, ws_ref, bs_ref, wo_ref,
                 g_ref, b_ref, *rest):
    if keep_v:
        o_ref, v_ref, y_ref = rest
    else:
        o_ref, y_ref = rest
    x = x_ref[...]
    tm = x.shape[0]
    z = _dot(x.astype(BF16), wi_ref[...]) + bi_ref[...]
    z = 0.5 * z * (1.0 + lax.erf(z * (1.0 / math.sqrt(2.0))))
    u = z[:, :A_INNER]
    v = _layer_norm(z[:, A_INNER:], lg_ref[...], lb_ref[...])
    if keep_v:
        v_ref[...] = v
    vb = v.astype(BF16)
    for c in range(tm // CHUNK):
        rows = slice(c * CHUNK, (c + 1) * CHUNK)
        for h in range(A_HEADS):
            cols = slice(h * A_GROUP, (h + 1) * A_GROUP)
            mixed = _dot(ws_ref[h], vb[rows, cols]) + bs_ref[:, h:h + 1]
            y_ref[rows, cols] = (u[rows, cols] * mixed).astype(BF16)
    out = _dot(y_ref[...], wo_ref[...])
    o_ref[...] = _layer_norm(ALPHA * x + out, g_ref[...], b_ref[...])


def _gmlp(x, consts, keep_v):
    n = x.shape[0]
    tm = min(ROW_TILE, n)
    out_shape = [jax.ShapeDtypeStruct((n, D_MODEL), F32)]
    out_specs = [_row_spec(D_MODEL, tm)]
    if keep_v:
        out_shape.append(jax.ShapeDtypeStruct((n, A_INNER), F32))
        out_specs.append(_row_spec(A_INNER, tm))
    res = pl.pallas_call(
        functools.partial(_gmlp_kernel, keep_v),
        grid=(n // tm,),
        in_specs=[_row_spec(D_MODEL, tm)] + [c.const_spec() for c in consts],
        out_specs=out_specs,
        out_shape=out_shape,
        scratch_shapes=[pltpu.VMEM((tm, A_INNER), BF16)],
        compiler_params=_params(("parallel",)),
        name="gmlp",
    )(x, *[c.arr for c in consts])
    return (res[0], res[1]) if keep_v else (res[0], None)


def _conv_tail(x, bg, conv, wo_ref, g_ref, b_ref, o_ref):
    out = _dot((bg * conv).astype(BF16), wo_ref[...])
    o_ref[...] = _layer_norm(ALPHA * x + out, g_ref[...], b_ref[...])


def _conv_long_kernel(tiles_per_seq, x_ref, buf_ref, wi_ref, cw_ref, wo_ref, g_ref, b_ref,
                      o_ref, tail_ref, carry_ref):
    x = x_ref[...]
    tm = x.shape[0]
    h3 = _dot(x.astype(BF16), wi_ref[...])
    bg = h3[:, :D_MODEL]
    z = h3[:, D_MODEL:2 * D_MODEL] * h3[:, 2 * D_MODEL:]

    @pl.when(pl.program_id(0) % tiles_per_seq == 0)
    def _():
        carry_ref[...] = buf_ref[0]

    row = lax.broadcasted_iota(jnp.int32, (tm, D_MODEL), 0)
    prev1 = carry_ref[7:8, :]
    prev2 = carry_ref[6:7, :]
    z1 = jnp.where(row == 0, prev1, pltpu.roll(z, 1, 0))
    z2 = jnp.where(row == 0, prev2, jnp.where(row == 1, prev1, pltpu.roll(z, 2, 0)))
    conv = cw_ref[0:1, :] * z2 + cw_ref[1:2, :] * z1 + cw_ref[2:3, :] * z
    tail = z[tm - 8:, :]
    carry_ref[...] = tail
    tail_ref[0] = tail
    _conv_tail(x, bg, conv, wo_ref, g_ref, b_ref, o_ref)


def _conv_short_kernel(seq, x_ref, h1_ref, h2_ref, wi_ref, cw_ref, wo_ref, g_ref, b_ref,
                       o_ref, z_ref):
    x = x_ref[...]
    tm = x.shape[0]
    h3 = _dot(x.astype(BF16), wi_ref[...])
    bg = h3[:, :D_MODEL]
    z = h3[:, D_MODEL:2 * D_MODEL] * h3[:, 2 * D_MODEL:]
    t = lax.broadcasted_iota(jnp.int32, (tm, D_MODEL), 0) % seq
    z1 = jnp.where(t >= 1, pltpu.roll(z, 1, 0), h1_ref[...])
    z2 = jnp.where(t >= 2, pltpu.roll(z, 2, 0), h2_ref[...])
    conv = cw_ref[0:1, :] * z2 + cw_ref[1:2, :] * z1 + cw_ref[2:3, :] * z
    z_ref[...] = z
    _conv_tail(x, bg, conv, wo_ref, g_ref, b_ref, o_ref)


def _first_rows(state_rows, seq):
    bn = state_rows.shape[0]
    out = jnp.zeros((bn, seq, D_MODEL), F32).at[:, 0, :].set(state_rows)
    return out.reshape(bn * seq, D_MODEL)


def _conv_mixer(x, buf_prev, seq, consts):
    n = x.shape[0]
    bn = n // seq
    tm = min(ROW_TILE, n)
    const_specs = [c.const_spec() for c in consts]
    const_args = [c.arr for c in consts]
    if seq >= tm:
        tiles_per_seq = seq // tm
        buf8 = jnp.concatenate([jnp.zeros((bn, 6, D_MODEL), F32), buf_prev], axis=1)
        out, tails = pl.pallas_call(
            functools.partial(_conv_long_kernel, tiles_per_seq),
            grid=(n // tm,),
            in_specs=[_row_spec(D_MODEL, tm),
                      pl.BlockSpec((1, 8, D_MODEL), lambda i: (i // tiles_per_seq, 0, 0))] + const_specs,
            out_specs=[_row_spec(D_MODEL, tm), pl.BlockSpec((1, 8, D_MODEL), lambda i: (i, 0, 0))],
            out_shape=[jax.ShapeDtypeStruct((n, D_MODEL), F32),
                       jax.ShapeDtypeStruct((n // tm, 8, D_MODEL), F32)],
            scratch_shapes=[pltpu.VMEM((8, D_MODEL), F32)],
            compiler_params=_params(("arbitrary",)),
            name="conv_long",
        )(x, buf8, *const_args)
        new_buf = tails[tiles_per_seq - 1::tiles_per_seq, 6:8, :]
        return out, new_buf
    assert tm % seq == 0 and seq >= CONV_W - 1
    h1 = _first_rows(buf_prev[:, 1, :], seq)
    h2 = _first_rows(buf_prev[:, 0, :], seq) + jnp.roll(h1, 1, axis=0)
    out, z = pl.pallas_call(
        functools.partial(_conv_short_kernel, seq),
        grid=(n // tm,),
        in_specs=[_row_spec(D_MODEL, tm)] * 3 + const_specs,
        out_specs=[_row_spec(D_MODEL, tm)] * 2,
        out_shape=[jax.ShapeDtypeStruct((n, D_MODEL), F32)] * 2,
        compiler_params=_params(("parallel",)),
        name="conv_short",
    )(x, h1, h2, *const_args)
    new_buf = z.reshape(bn, seq, D_MODEL)[:, seq - (CONV_W - 1):, :]
    return out, new_buf


def _head_sum_bcast(x, e_ref, et_ref):
    s = _dot(x.astype(BF16), e_ref[...])
    s_hi, s_lo = _split2(s)
    return _dot(s_hi, et_ref[...]) + _dot(s_lo, et_ref[...])


def _rwkv_proj_body(x, xprev, mu_ref, wrkv_ref, w0_ref, w1_ref, w2_ref, a0_ref, a1_ref,
                    a2_ref, g1_ref, g2_ref, kk_ref, ka_ref, rk_ref, e_ref, et_ref,
                    r_out, lw_out, k_out, v_out, c_out, b_out, bonus_out, g_out):
    xx = xprev - x
    mix = lambda i: (x + xx * mu_ref[i:i + 1, :]).astype(BF16)
    r = _dot(mix(0), wrkv_ref[0])
    k = _dot(mix(2), wrkv_ref[1])
    v = _dot(mix(3), wrkv_ref[2])
    zw = w0_ref[...] + _dot(jnp.tanh(_dot(mix(1), w1_ref[...])).astype(BF16), w2_ref[...])
    lw_out[...] = -_sigmoid(zw) * math.exp(-0.5)
    a = _sigmoid(a0_ref[...] + _dot(_dot(mix(4), a1_ref[...]).astype(BF16), a2_ref[...]))
    g_out[...] = _dot(_sigmoid(_dot(mix(5), g1_ref[...])).astype(BF16), g2_ref[...])
    kk = k * kk_ref[...]
    norm = jnp.sqrt(_head_sum_bcast(kk * kk, e_ref, et_ref))
    c = kk / jnp.maximum(norm, 1e-12)
    kmod = k * (1.0 + (a - 1.0) * ka_ref[...])
    r_out[...] = r
    k_out[...] = kmod
    v_out[...] = v
    c_out[...] = c
    b_out[...] = c * a
    bonus_out[...] = _head_sum_bcast(r * kmod * rk_ref[...], e_ref, et_ref) * v


def _rwkv_proj_long_kernel(tiles_per_seq, x_ref, shift_ref, *rest):
    carry_ref = rest[-1]
    x = x_ref[...]
    tm = x.shape[0]

    @pl.when(pl.program_id(0) % tiles_per_seq == 0)
    def _():
        carry_ref[...] = shift_ref[...]

    row = lax.broadcasted_iota(jnp.int32, (tm, D_MODEL), 0)
    xprev = jnp.where(row == 0, carry_ref[...], pltpu.roll(x, 1, 0))
    carry_ref[...] = x[tm - 1:tm, :]
    _rwkv_proj_body(x, xprev, *rest[:-1])


def _rwkv_proj_short_kernel(seq, x_ref, h1_ref, *rest):
    x = x_ref[...]
    t = lax.broadcasted_iota(jnp.int32, x.shape, 0) % seq
    xprev = jnp.where(t >= 1, pltpu.roll(x, 1, 0), h1_ref[...])
    _rwkv_proj_body(x, xprev, *rest)


def _rwkv_proj(x, shift_prev, seq, consts):
    n = x.shape[0]
    tm = min(ROW_TILE // 2, n)
    common = dict(
        grid=(n // tm,),
        out_specs=[_row_spec(D_MODEL, tm)] * 8,
        out_shape=[jax.ShapeDtypeStruct((n, D_MODEL), F32)] * 8,
    )
    const_specs = [c.const_spec() for c in consts]
    const_args = [c.arr for c in consts]
    if seq >= tm:
        tiles_per_seq = seq // tm
        return pl.pallas_call(
            functools.partial(_rwkv_proj_long_kernel, tiles_per_seq),
            in_specs=[_row_spec(D_MODEL, tm),
                      pl.BlockSpec((None, 1, D_MODEL), lambda i: (i // tiles_per_seq, 0, 0))] + const_specs,
            scratch_shapes=[pltpu.VMEM((1, D_MODEL), F32)],
            compiler_params=_params(("arbitrary",)),
            name="rwkv_proj_long", **common,
        )(x, shift_prev[:, None, :], *const_args)
    assert tm % seq == 0
    return pl.pallas_call(
        functools.partial(_rwkv_proj_short_kernel, seq),
        in_specs=[_row_spec(D_MODEL, tm)] * 2 + const_specs,
        compiler_params=_params(("parallel",)),
        name="rwkv_proj_short", **common,
    )(x, _first_rows(shift_prev, seq), *const_args)


def _pair_rows(x, lane_lo):
    zero = jnp.zeros_like(x)
    return jnp.concatenate([jnp.where(lane_lo, x, zero), jnp.where(lane_lo, zero, x)], axis=0)


def _block_diag(a, b):
    zero = jnp.zeros_like(a)
    return jnp.concatenate([jnp.concatenate([a, zero], axis=1), jnp.concatenate([zero, b], axis=1)], axis=0)


def _scan_chunk(refs, y_ref, sub, qs_fn, state_fn):
    C = SCAN_CHUNK
    pairs = range(HEAD_PAIRS)
    shift = int(math.log2(sub))
    row = lax.broadcasted_iota(jnp.int32, (C, 2 * C), 0)
    col = lax.broadcasted_iota(jnp.int32, (C, 2 * C), 1) & (C - 1)
    same = (row >> shift) == (col >> shift)
    strict = same & (row > col)
    incl = same & (row >= col)
    lane_lo = lax.broadcasted_iota(jnp.int32, (C, LANES), 1) < HEAD_SIZE
    r2 = lax.broadcasted_iota(jnp.int32, (C, C), 0)
    c2 = lax.broadcasted_iota(jnp.int32, (C, C), 1)
    tri = jnp.where(((r2 >> shift) == (c2 >> shift)) & (r2 >= c2), 1.0, 0.0).astype(BF16)
    lanes = [slice(p * LANES, (p + 1) * LANES) for p in pairs]
    r, lw, k, v, c, b = ([ref[:, s] for s in lanes] for ref in refs)

    cum = []
    for p in pairs:
        l_hi, l_mid, l_lo = _split3(lw[p])
        cum.append(_dot(tri, l_hi) + (_dot(tri, l_mid) + _dot(tri, l_lo)))
    p_inc = [jnp.exp(cum[p]) for p in pairs]
    p_inv = [jnp.exp(-cum[p]) for p in pairs]
    p_exc = [jnp.exp(cum[p] - lw[p]) for p in pairs]
    q = [jnp.concatenate([c[p] * p_exc[p], r[p] * p_inc[p]], axis=0).astype(BF16) for p in pairs]
    bt = [(b[p] * p_inv[p]).astype(BF16) for p in pairs]
    kt = [(k[p] * p_inv[p]).astype(BF16) for p in pairs]
    vb = [v[p].astype(BF16) for p in pairs]
    kb = [jnp.concatenate([_pair_rows(bt[p], lane_lo), _pair_rows(kt[p], lane_lo)], axis=0) for p in pairs]
    gram = [_dot(q[p], kb[p], NT) for p in pairs]
    l_cb = [jnp.where(strict, gram[p][:C, :2 * C], 0.0) for p in pairs]
    l_ck = [jnp.where(strict, gram[p][:C, 2 * C:], 0.0).astype(BF16) for p in pairs]
    a_rb = [jnp.where(incl, gram[p][C:, :2 * C], 0.0) for p in pairs]
    a_rk = [jnp.where(incl, gram[p][C:, 2 * C:], 0.0) for p in pairs]
    qs = qs_fn(q)
    v_rows = [_pair_rows(vb[p], lane_lo) for p in pairs]
    u = [-(qs[p][:C] + _dot(l_ck[p], v_rows[p])) for p in pairs]
    m = [l_cb[p].astype(BF16) for p in pairs]
    u = [u[p] - _dot(m[p], _pair_rows(u[p].astype(BF16), lane_lo)) for p in pairs]
    for _ in range(shift - 1):
        m = [_dot(m[p], _pair_rows(m[p], lane_lo)).astype(BF16) for p in pairs]
        u = [u[p] + _dot(m[p], _pair_rows(u[p].astype(BF16), lane_lo)) for p in pairs]
    ub = [u[p].astype(BF16) for p in pairs]
    for p in pairs:
        a = jnp.concatenate([a_rb[p], a_rk[p]], axis=1).astype(BF16)
        uv_rows = jnp.concatenate([_pair_rows(ub[p], lane_lo), v_rows[p]], axis=0)
        y_ref[:, lanes[p]] = qs[p][C:] + _dot(a, uv_rows)
    state_fn(u, v, ub, vb, bt, kt, p_inc)


def _diag_blocks_mask():
    sq_row = lax.broadcasted_iota(jnp.int32, (LANES, LANES), 0) < HEAD_SIZE
    sq_col = lax.broadcasted_iota(jnp.int32, (LANES, LANES), 1) < HEAD_SIZE
    return sq_row == sq_col


def _store_pair_state(out_ref, i, p, s_pair):
    out_ref[i, 2 * p] = s_pair[:HEAD_SIZE, :HEAD_SIZE]
    out_ref[i, 2 * p + 1] = s_pair[HEAD_SIZE:, HEAD_SIZE:]


def _rwkv_scan_kernel(r_ref, lw_ref, k_ref, v_ref, c_ref, b_ref, s0_ref, y_ref, sfin_ref, s_ref):
    C = SCAN_CHUNK
    j = pl.program_id(1)

    @pl.when(j == 0)
    def _():
        for p in range(HEAD_PAIRS):
            s_ref[p] = _block_diag(s0_ref[0, 2 * p], s0_ref[0, 2 * p + 1])

    diag = _diag_blocks_mask()

    def qs_fn(q):
        return [_dot(q[p], s_ref[p].astype(BF16), NT) for p in range(HEAD_PAIRS)]

    def state_fn(u, v, ub, vb, bt, kt, p_inc):
        for p in range(HEAD_PAIRS):
            upd = _dot(jnp.concatenate([ub[p], vb[p]], axis=0), jnp.concatenate([bt[p], kt[p]], axis=0), TN)
            s_ref[p] = (s_ref[p] + jnp.where(diag, upd, 0.0)) * p_inc[p][C - 1:C, :]

    _scan_chunk((r_ref, lw_ref, k_ref, v_ref, c_ref, b_ref), y_ref, C, qs_fn, state_fn)

    @pl.when(j == pl.num_programs(1) - 1)
    def _():
        for p in range(HEAD_PAIRS):
            _store_pair_state(sfin_ref, 0, p, s_ref[p])


def _rwkv_scan(r, lw, k, v, c, b, s0, bn, seq):
    n_chunks = seq // SCAN_CHUNK
    row_spec = pl.BlockSpec((SCAN_CHUNK, D_MODEL), lambda i, j: (i * n_chunks + j, 0))
    st_spec = pl.BlockSpec((1, B_HEADS, HEAD_SIZE, HEAD_SIZE), lambda i, j: (i, 0, 0, 0))
    return pl.pallas_call(
        _rwkv_scan_kernel,
        grid=(bn, n_chunks),
        in_specs=[row_spec] * 6 + [st_spec],
        out_specs=[row_spec, st_spec],
        out_shape=[jax.ShapeDtypeStruct((bn * seq, D_MODEL), F32),
                   jax.ShapeDtypeStruct((bn, B_HEADS, HEAD_SIZE, HEAD_SIZE), F32)],
        scratch_shapes=[pltpu.VMEM((HEAD_PAIRS, LANES, LANES), F32)],
        compiler_params=_params(("parallel", "arbitrary")),
        name="rwkv_scan",
    )(r, lw, k, v, c, b, s0)


def _rwkv_scan_short_kernel(sub, r_ref, lw_ref, k_ref, v_ref, c_ref, b_ref, s0_ref, y_ref, sfin_ref):
    C = SCAN_CHUNK
    nb = C // sub
    shift = int(math.log2(sub))
    diag = _diag_blocks_mask()
    row_seq = (lax.broadcasted_iota(jnp.int32, (2 * C, LANES), 0) & (C - 1)) >> shift
    lane_seq = (lax.broadcasted_iota(jnp.int32, (LANES, 2 * C), 1) & (C - 1)) >> shift
    s_prev = {}

    def qs_fn(q):
        out = []
        for p in range(HEAD_PAIRS):
            for i in range(nb):
                s_prev[i, p] = _block_diag(s0_ref[i, 2 * p], s0_ref[i, 2 * p + 1])
            s_stack = jnp.concatenate([s_prev[i, p].astype(BF16) for i in range(nb)], axis=1)
            zero = jnp.zeros_like(q[p])
            q_wide = jnp.concatenate([jnp.where(row_seq == i, q[p], zero) for i in range(nb)], axis=1)
            out.append(_dot(q_wide, s_stack, NT))
        return out

    def state_fn(u, v, ub, vb, bt, kt, p_inc):
        for p in range(HEAD_PAIRS):
            uvt = jnp.concatenate([u[p], v[p]], axis=0).T
            stack = jnp.concatenate([jnp.where(lane_seq == i, uvt, 0.0) for i in range(nb)], axis=0)
            upd = _dot(stack.astype(BF16), jnp.concatenate([bt[p], kt[p]], axis=0))
            for i in range(nb):
                last = i * sub + sub - 1
                s_new = ((s_prev[i, p] + jnp.where(diag, upd[i * LANES:(i + 1) * LANES], 0.0))
                         * p_inc[p][last:last + 1, :])
                _store_pair_state(sfin_ref, i, p, s_new)

    _scan_chunk((r_ref, lw_ref, k_ref, v_ref, c_ref, b_ref), y_ref, sub, qs_fn, state_fn)


def _rwkv_scan_short(r, lw, k, v, c, b, s0, bn, seq):
    nb = SCAN_CHUNK // seq
    row_spec = pl.BlockSpec((SCAN_CHUNK, D_MODEL), lambda i: (i, 0))
    st_spec = pl.BlockSpec((nb, B_HEADS, HEAD_SIZE, HEAD_SIZE), lambda i: (i, 0, 0, 0))
    return pl.pallas_call(
        functools.partial(_rwkv_scan_short_kernel, seq),
        grid=(bn // nb,),
        in_specs=[row_spec] * 6 + [st_spec],
        out_specs=[row_spec, st_spec],
        out_shape=[jax.ShapeDtypeStruct((bn * seq, D_MODEL), F32),
                   jax.ShapeDtypeStruct((bn, B_HEADS, HEAD_SIZE, HEAD_SIZE), F32)],
        compiler_params=_params(("parallel",)),
        name="rwkv_scan_short",
    )(r, lw, k, v, c, b, s0)


def _rwkv_post_kernel(x_ref, y_ref, bonus_ref, gate_ref, lg_ref, lb_ref, wo_ref, e_ref, et_ref,
                      g_ref, b_ref, o_ref):
    ys = y_ref[...]
    inv = 1.0 / HEAD_SIZE
    m = _head_sum_bcast(ys, e_ref, et_ref) * inv
    yc = ys - m
    var = _head_sum_bcast(yc * yc, e_ref, et_ref) * inv
    yn = yc * lax.rsqrt(var + GN_EPS) * lg_ref[...] + lb_ref[...]
    out = _dot(((yn + bonus_ref[...]) * gate_ref[...]).astype(BF16), wo_ref[...])
    o_ref[...] = _layer_norm(ALPHA * x_ref[...] + out, g_ref[...], b_ref[...])


def _rwkv_post(x, ys, bonus, gate, consts):
    n = x.shape[0]
    tm = min(ROW_TILE, n)
    return pl.pallas_call(
        _rwkv_post_kernel,
        grid=(n // tm,),
        in_specs=[_row_spec(D_MODEL, tm)] * 4 + [c.const_spec() for c in consts],
        out_specs=_row_spec(D_MODEL, tm),
        out_shape=jax.ShapeDtypeStruct((n, D_MODEL), F32),
        compiler_params=_params(("parallel",)),
        name="rwkv_post",
    )(x, ys, bonus, gate, *[c.arr for c in consts])


def _rwkv_mixer(x, shift_prev, s0, seq, w, g, b):
    n = x.shape[0]
    bn = n // seq
    r, lw, k, v, c, bb, bonus, gate = _rwkv_proj(x, shift_prev, seq, w["proj"])
    if seq % SCAN_CHUNK == 0:
        scan = _rwkv_scan
    else:
        assert SCAN_CHUNK % seq == 0 and seq & (seq - 1) == 0 and n % SCAN_CHUNK == 0
        scan = _rwkv_scan_short
    ys, s_new = scan(r, lw, k, v, c, bb, s0, bn, seq)
    out = _rwkv_post(x, ys, bonus, gate, w["post"] + [g, b])
    return out, x.reshape(bn, seq, D_MODEL)[:, -1, :], s_new


def _run_trunk(x3, p4, b_wkv, b_shift, c_conv, W, keep_v):
    bn, seq, _ = x3.shape
    n = bn * seq
    x = x3.reshape(n, D_MODEL)
    p = p4.reshape(DEPTH, n, PLE_DIM)
    new_v, new_wkv, new_shift, new_conv = [], [], [], []
    for i in range(DEPTH):
        j, kind = divmod(i, N_MIXERS)
        ln_g = lambda s: _sel(W["ln_g"], i, s)
        ln_b = lambda s: _sel(W["ln_b"], i, s)
        x = _ffn(x, _sel(W["ffn_w_in"], i, 0), _sel(W["ffn_w_out"], i, 0), ln_g(0), ln_b(0))
        if kind == 0:
            mix = W["a_mix_long"] if seq >= CHUNK else W["a_mix_short"]
            consts = [_sel(W[name], j) for name in ("a_w_in", "a_b_in", "a_ln_g", "a_ln_b")]
            consts += [_sel(mix[0], j), _sel(mix[1], j), _sel(W["a_w_out"], j), ln_g(1), ln_b(1)]
            x, v = _gmlp(x, consts, keep_v)
            if keep_v:
                new_v.append(v.reshape(bn, seq, A_INNER))
        elif kind == 1:
            w = dict(proj=[_sel(a, j) for a in W["b_proj"]] + [_sel(W["e"]), _sel(W["et"])],
                     post=[_sel(a, j) for a in W["b_post"]] + [_sel(W["e"]), _sel(W["et"])])
            x, sh, s = _rwkv_mixer(x, b_shift[j], b_wkv[j], seq, w, ln_g(1), ln_b(1))
            new_shift.append(sh)
            new_wkv.append(s)
        else:
            consts = [_sel(W[name], j) for name in ("c_w_in", "c_conv_w", "c_w_out")] + [ln_g(1), ln_b(1)]
            x, buf = _conv_mixer(x, c_conv[j], seq, consts)
            new_conv.append(buf)
        x = _ffn(x, _sel(W["ffn_w_in"], i, 1), _sel(W["ffn_w_out"], i, 1), ln_g(2), ln_b(2),
                 ple=(_sel(p, i), _sel(W["ple_w_gate"], i), _sel(W["ple_w_proj"], i)))
    a_state = jnp.stack(new_v) if keep_v else None
    return x.reshape(bn, seq, D_MODEL), a_state, jnp.stack(new_wkv), jnp.stack(new_shift), jnp.stack(new_conv)


def _gmlp_mix_mats(w_s, b_s, seq):
    l = min(seq, CHUNK)
    ws = jnp.where(jnp.tril(jnp.ones((l, l), dtype=bool)), w_s[..., :l, :l], 0.0)
    reps = CHUNK // l
    if reps > 1:
        eye = jnp.eye(reps, dtype=F32)
        ws = jnp.einsum("ab,jhts->jhatbs", eye, ws).reshape(-1, A_HEADS, CHUNK, CHUNK)
    bias = jnp.tile(jnp.swapaxes(b_s[..., :l], -1, -2), (1, reps, 1))
    return ws.astype(BF16), bias


def kernel(x_prompt, x_sample, state_b_wkv, state_b_shift, state_c_conv, p_prompt, p_sample, ln_g, ln_b, ffn_w_in, ffn_w_out, ple_w_gate, ple_w_proj, a_w_in, a_b_in, a_ln_g, a_ln_b, a_w_s, a_b_s, a_w_out, b_mu, b_w_rkv, b_w0, b_w1, b_w2, b_a0, b_a1, b_a2, b_g1, b_g2, b_k_k, b_k_a, b_r_k, b_lnx_g, b_lnx_b, b_w_o, c_w_in, c_conv_w, c_w_out):
    bf = lambda w: w.astype(BF16)
    row = lambda w: w.reshape(w.shape[0], 1, -1)
    head_of_lane = jnp.arange(D_MODEL) // HEAD_SIZE
    e = (head_of_lane[:, None] == jnp.arange(LANES)[None, :]).astype(BF16)
    W = dict(
        ln_g=ln_g[:, :, None, :], ln_b=ln_b[:, :, None, :],
        ffn_w_in=bf(ffn_w_in), ffn_w_out=bf(ffn_w_out),
        ple_w_gate=bf(ple_w_gate), ple_w_proj=bf(ple_w_proj),
        a_w_in=bf(a_w_in), a_b_in=row(a_b_in), a_ln_g=row(a_ln_g), a_ln_b=row(a_ln_b), a_w_out=bf(a_w_out),
        a_mix_long=_gmlp_mix_mats(a_w_s, a_b_s, x_prompt.shape[1]),
        a_mix_short=_gmlp_mix_mats(a_w_s, a_b_s, x_sample.shape[1]),
        c_w_in=bf(c_w_in), c_conv_w=c_conv_w, c_w_out=bf(c_w_out),
        b_proj=[b_mu, bf(b_w_rkv), row(b_w0), bf(b_w1), bf(b_w2), row(b_a0), bf(b_a1), bf(b_a2),
                bf(b_g1), bf(b_g2), row(b_k_k), row(b_k_a), row(b_r_k)],
        b_post=[row(b_lnx_g), row(b_lnx_b), bf(b_w_o)],
        e=e, et=e.T,
    )
    bp = x_prompt.shape[0]
    n_b = b_mu.shape[0]
    n_c = c_w_in.shape[0]
    zero_wkv = jnp.zeros((n_b, bp) + state_b_wkv.shape[2:], state_b_wkv.dtype)
    zero_shift = jnp.zeros((n_b, bp, D_MODEL), state_b_shift.dtype)
    zero_conv = jnp.zeros((n_c, bp, CONV_W - 1, D_MODEL), state_c_conv.dtype)
    y_p, _, wkv_p, shift_p, conv_p = _run_trunk(x_prompt, p_prompt, zero_wkv, zero_shift, zero_conv, W, False)
    y_s, a_v_s, wkv_s, shift_s, conv_s = _run_trunk(x_sample, p_sample, state_b_wkv, state_b_shift, state_c_conv, W, True)
    return (y_p, y_s, a_v_s, wkv_p, shift_p, conv_p, wkv_s, shift_s, conv_s)
```

```python
import collections
import functools
import math

import jax
import jax.numpy as jnp
from jax import lax
from jax.experimental import pallas as pl
from jax.experimental.pallas import tpu as pltpu

F32 = jnp.float32
BF16 = jnp.bfloat16

D_MODEL = 1024
DEPTH = 4
N_MIXERS = 3
CHUNK = 128
A_INNER = 2 * D_MODEL
A_HEADS = 8
A_GROUP = A_INNER // A_HEADS
HEAD_SIZE = 64
B_HEADS = D_MODEL // HEAD_SIZE
CONV_W = 3
D_FF = 2816
PLE_DIM = 256
ALPHA = (2 * DEPTH) ** 0.25
LN_EPS = 1e-5
GN_EPS = 64e-5

LANES = 128
MXU_DIM = 256
ROW_TILE = 512
SCAN_CHUNK = 64
HEAD_PAIRS = B_HEADS // 2
VMEM_LIMIT = 56 * 1024 * 1024

NN = (((1,), (0,)), ((), ()))
NT = (((1,), (1,)), ((), ()))
TN = (((0,), (0,)), ((), ()))


def _dot(a, b, dims=NN):
    return lax.dot_general(a, b, dims, preferred_element_type=F32)


def _split2(x):
    hi = x.astype(BF16)
    lo = (x - hi.astype(F32)).astype(BF16)
    return hi, lo


def _split3(x):
    hi = x.astype(BF16)
    r1 = x - hi.astype(F32)
    mid = r1.astype(BF16)
    lo = (r1 - mid.astype(F32)).astype(BF16)
    return hi, mid, lo


def _layer_norm(x, g, b, eps=LN_EPS):
    mu = jnp.mean(x, axis=-1, keepdims=True)
    xc = x - mu
    var = jnp.mean(xc * xc, axis=-1, keepdims=True)
    return xc * lax.rsqrt(var + eps) * g + b


def _sigmoid(x):
    return 1.0 / (1.0 + jnp.exp(-x))


class _Sel(collections.namedtuple("_Sel", ["arr", "idx"])):
    def const_spec(self):
        k = len(self.idx)
        rest = self.arr.shape[k:]
        idx = self.idx
        return pl.BlockSpec((None,) * k + rest, lambda *_: idx + (0,) * len(rest),
                            pipeline_mode=pl.Buffered(1))

    def row_spec(self, tm):
        k = len(self.idx)
        idx = self.idx
        return pl.BlockSpec((None,) * k + (tm, self.arr.shape[-1]), lambda i: idx + (i, 0))


def _sel(arr, *idx):
    return _Sel(arr, tuple(idx))


def _row_spec(width, tm=ROW_TILE):
    return pl.BlockSpec((tm, width), lambda i: (i, 0))


def _params(sem):
    return pltpu.CompilerParams(dimension_semantics=sem, vmem_limit_bytes=VMEM_LIMIT)


FF_BLOCKS = tuple((lo, min(lo + 3 * MXU_DIM, D_FF)) for lo in range(0, D_FF, 3 * MXU_DIM))
assert D_FF % MXU_DIM == 0
FFN_TILE = 2 * ROW_TILE


def _ffn_kernel(with_ple, x_ref, wi_ref, wo_ref, g_ref, b_ref, *rest):
    if with_ple:
        p_ref, wg_ref, wp_ref, o_ref = rest
    else:
        (o_ref,) = rest
    tm = x_ref.shape[0]
    hm = min(ROW_TILE, tm)
    qm = hm // len(FF_BLOCKS)

    def tail(acc_ref_rows, rows):
        x = x_ref[rows, :]
        y = _layer_norm(ALPHA * x + 0.5 * acc_ref_rows, g_ref[...], b_ref[...])
        if with_ple:
            gate = _sigmoid(_dot(y.astype(BF16), wg_ref[...]))
            y = y + gate * _dot(p_ref[rows, :].astype(BF16), wp_ref[...])
        o_ref[rows, :] = y

    prev = None
    for h in range(tm // hm):
        xb = x_ref[h * hm:(h + 1) * hm, :].astype(BF16)
        acc = None
        for bi, (lo, hi) in enumerate(FF_BLOCKS):
            gate = _dot(xb, wi_ref[:, lo:hi])
            if prev is not None:
                anchor = pltpu.bitcast(gate[0:1, 0:1], jnp.uint32)
                zero = pltpu.bitcast((anchor >> 16) >> 16, F32)
                off, pacc = prev
                tail(pacc[bi * qm:(bi + 1) * qm] + zero, slice(off + bi * qm, off + (bi + 1) * qm))
            up = _dot(xb, wi_ref[:, D_FF + lo:D_FF + hi])
            act = (gate * _sigmoid(gate) * up).astype(BF16)
            part = _dot(act, wo_ref[lo:hi, :])
            acc = part if acc is None else acc + part
        prev = (h * hm, acc)
    off, pacc = prev
    tail(pacc, slice(off, off + hm))


def _ffn(x, wi, wo, g, b, ple=None):
    n = x.shape[0]
    tm = min(FFN_TILE, n)
    consts = [wi, wo, g, b]
    args = [x] + [c.arr for c in consts]
    specs = [_row_spec(D_MODEL, tm)] + [c.const_spec() for c in consts]
    if ple is not None:
        p, wg, wp = ple
        args += [p.arr, wg.arr, wp.arr]
        specs += [p.row_spec(tm), wg.const_spec(), wp.const_spec()]
    return pl.pallas_call(
        functools.partial(_ffn_kernel, ple is not None),
        grid=(n // tm,),
        in_specs=specs,
        out_specs=_row_spec(D_MODEL, tm),
        out_shape=jax.ShapeDtypeStruct((n, D_MODEL), F32),
        compiler_params=_params(("parallel",)),
        name="ffn_ple" if ple is not None else "ffn",
    )(*args)


def _gmlp_kernel(keep_v, x_ref, wi_ref, bi_ref, lg_ref, lb_ref, ws_ref, bs_ref, wo_ref,
                 g_ref, b_ref, *rest):
    if keep_v:
        o_ref, v_ref, y_ref = rest
    else:
        o_ref, y_ref = rest
    tm = x_ref.shape[0]
    hm = tm // 2
    halves = [slice(h * hm, (h + 1) * hm) for h in range(2)]
    xs = [x_ref[rows, :] for rows in halves]
    zs = [_dot(x.astype(BF16), wi_ref[...]) + bi_ref[...] for x in xs]
    us, vbs = [], []
    for rows, z in zip(halves, zs):
        z = 0.5 * z * (1.0 + lax.erf(z * (1.0 / math.sqrt(2.0))))
        v = _layer_norm(z[:, A_INNER:], lg_ref[...], lb_ref[...])
        if keep_v:
            v_ref[rows, :] = v
        us.append(z[:, :A_INNER])
        vbs.append(v.astype(BF16))
    for half, u, vb in zip(halves, us, vbs):
        for c in range(hm // CHUNK):
            rows = slice(c * CHUNK, (c + 1) * CHUNK)
            out_rows = slice(half.start + c * CHUNK, half.start + (c + 1) * CHUNK)
            for h in range(A_HEADS):
                cols = slice(h * A_GROUP, (h + 1) * A_GROUP)
                mixed = _dot(ws_ref[h], vb[rows, cols]) + bs_ref[:, h:h + 1]
                y_ref[out_rows, cols] = (u[rows, cols] * mixed).astype(BF16)
    for rows, x in zip(halves, xs):
        out = _dot(y_ref[rows, :], wo_ref[...])
        o_ref[rows, :] = _layer_norm(ALPHA * x + out, g_ref[...], b_ref[...])


def _gmlp(x, consts, keep_v):
    n = x.shape[0]
    tm = min(ROW_TILE, n)
    out_shape = [jax.ShapeDtypeStruct((n, D_MODEL), F32)]
    out_specs = [_row_spec(D_MODEL, tm)]
    if keep_v:
        out_shape.append(jax.ShapeDtypeStruct((n, A_INNER), F32))
        out_specs.append(_row_spec(A_INNER, tm))
    res = pl.pallas_call(
        functools.partial(_gmlp_kernel, keep_v),
        grid=(n // tm,),
        in_specs=[_row_spec(D_MODEL, tm)] + [c.const_spec() for c in consts],
        out_specs=out_specs,
        out_shape=out_shape,
        scratch_shapes=[pltpu.VMEM((tm, A_INNER), BF16)],
        compiler_params=_params(("parallel",)),
        name="gmlp",
    )(x, *[c.arr for c in consts])
    return (res[0], res[1]) if keep_v else (res[0], None)


def _conv_tail(x, bg, conv, wo_ref, g_ref, b_ref, o_ref):
    out = _dot((bg * conv).astype(BF16), wo_ref[...])
    o_ref[...] = _layer_norm(ALPHA * x + out, g_ref[...], b_ref[...])


def _conv_long_kernel(tiles_per_seq, x_ref, buf_ref, wi_ref, cw_ref, wo_ref, g_ref, b_ref,
                      o_ref, tail_ref, carry_ref):
    x = x_ref[...]
    tm = x.shape[0]
    h3 = _dot(x.astype(BF16), wi_ref[...])
    bg = h3[:, :D_MODEL]
    z = h3[:, D_MODEL:2 * D_MODEL] * h3[:, 2 * D_MODEL:]

    @pl.when(pl.program_id(0) % tiles_per_seq == 0)
    def _():
        carry_ref[...] = buf_ref[0]

    row = lax.broadcasted_iota(jnp.int32, (tm, D_MODEL), 0)
    prev1 = carry_ref[7:8, :]
    prev2 = carry_ref[6:7, :]
    z1 = jnp.where(row == 0, prev1, pltpu.roll(z, 1, 0))
    z2 = jnp.where(row == 0, prev2, jnp.where(row == 1, prev1, pltpu.roll(z, 2, 0)))
    conv = cw_ref[0:1, :] * z2 + cw_ref[1:2, :] * z1 + cw_ref[2:3, :] * z
    tail = z[tm - 8:, :]
    carry_ref[...] = tail
    tail_ref[0] = tail
    _conv_tail(x, bg, conv, wo_ref, g_ref, b_ref, o_ref)


def _conv_short_kernel(seq, x_ref, h1_ref, h2_ref, wi_ref, cw_ref, wo_ref, g_ref, b_ref,
                       o_ref, z_ref):
    x = x_ref[...]
    tm = x.shape[0]
    h3 = _dot(x.astype(BF16), wi_ref[...])
    bg = h3[:, :D_MODEL]
    z = h3[:, D_MODEL:2 * D_MODEL] * h3[:, 2 * D_MODEL:]
    t = lax.broadcasted_iota(jnp.int32, (tm, D_MODEL), 0) % seq
    z1 = jnp.where(t >= 1, pltpu.roll(z, 1, 0), h1_ref[...])
    z2 = jnp.where(t >= 2, pltpu.roll(z, 2, 0), h2_ref[...])
    conv = cw_ref[0:1, :] * z2 + cw_ref[1:2, :] * z1 + cw_ref[2:3, :] * z
    z_ref[...] = z
    _conv_tail(x, bg, conv, wo_ref, g_ref, b_ref, o_ref)


def _first_rows(state_rows, seq):
    bn = state_rows.shape[0]
    out = jnp.zeros((bn, seq, D_MODEL), F32).at[:, 0, :].set(state_rows)
    return out.reshape(bn * seq, D_MODEL)


def _conv_mixer(x, buf_prev, seq, consts):
    n = x.shape[0]
    bn = n // seq
    tm = min(ROW_TILE, n)
    const_specs = [c.const_spec() for c in consts]
    const_args = [c.arr for c in consts]
    if seq >= tm:
        tiles_per_seq = seq // tm
        buf8 = jnp.concatenate([jnp.zeros((bn, 6, D_MODEL), F32), buf_prev], axis=1)
        out, tails = pl.pallas_call(
            functools.partial(_conv_long_kernel, tiles_per_seq),
            grid=(n // tm,),
            in_specs=[_row_spec(D_MODEL, tm),
                      pl.BlockSpec((1, 8, D_MODEL), lambda i: (i // tiles_per_seq, 0, 0))] + const_specs,
            out_specs=[_row_spec(D_MODEL, tm), pl.BlockSpec((1, 8, D_MODEL), lambda i: (i, 0, 0))],
            out_shape=[jax.ShapeDtypeStruct((n, D_MODEL), F32),
                       jax.ShapeDtypeStruct((n // tm, 8, D_MODEL), F32)],
            scratch_shapes=[pltpu.VMEM((8, D_MODEL), F32)],
            compiler_params=_params(("arbitrary",)),
            name="conv_long",
        )(x, buf8, *const_args)
        new_buf = tails[tiles_per_seq - 1::tiles_per_seq, 6:8, :]
        return out, new_buf
    assert tm % seq == 0 and seq >= CONV_W - 1
    h1 = _first_rows(buf_prev[:, 1, :], seq)
    h2 = _first_rows(buf_prev[:, 0, :], seq) + jnp.roll(h1, 1, axis=0)
    out, z = pl.pallas_call(
        functools.partial(_conv_short_kernel, seq),
        grid=(n // tm,),
        in_specs=[_row_spec(D_MODEL, tm)] * 3 + const_specs,
        out_specs=[_row_spec(D_MODEL, tm)] * 2,
        out_shape=[jax.ShapeDtypeStruct((n, D_MODEL), F32)] * 2,
        compiler_params=_params(("parallel",)),
        name="conv_short",
    )(x, h1, h2, *const_args)
    new_buf = z.reshape(bn, seq, D_MODEL)[:, seq - (CONV_W - 1):, :]
    return out, new_buf


def _head_sum_bcast(x, e_ref, et_ref):
    s = _dot(x.astype(BF16), e_ref[...])
    s_hi, s_lo = _split2(s)
    return _dot(s_hi, et_ref[...]) + _dot(s_lo, et_ref[...])


def _rwkv_proj_body(x, xprev, mu_ref, wrkv_ref, w0_ref, w1_ref, w2_ref, a0_ref, a1_ref,
                    a2_ref, g1_ref, g2_ref, kk_ref, ka_ref, rk_ref, e_ref, et_ref,
                    r_out, lw_out, k_out, v_out, c_out, b_out, bonus_out, g_out):
    xx = xprev - x
    mix = lambda i: (x + xx * mu_ref[i:i + 1, :]).astype(BF16)
    r = _dot(mix(0), wrkv_ref[0])
    k = _dot(mix(2), wrkv_ref[1])
    v = _dot(mix(3), wrkv_ref[2])
    zw = w0_ref[...] + _dot(jnp.tanh(_dot(mix(1), w1_ref[...])).astype(BF16), w2_ref[...])
    lw_out[...] = -_sigmoid(zw) * math.exp(-0.5)
    a = _sigmoid(a0_ref[...] + _dot(_dot(mix(4), a1_ref[...]).astype(BF16), a2_ref[...]))
    g_out[...] = _dot(_sigmoid(_dot(mix(5), g1_ref[...])).astype(BF16), g2_ref[...])
    kk = k * kk_ref[...]
    norm = jnp.sqrt(_head_sum_bcast(kk * kk, e_ref, et_ref))
    c = kk / jnp.maximum(norm, 1e-12)
    kmod = k * (1.0 + (a - 1.0) * ka_ref[...])
    r_out[...] = r
    k_out[...] = kmod
    v_out[...] = v
    c_out[...] = c
    b_out[...] = c * a
    bonus_out[...] = _head_sum_bcast(r * kmod * rk_ref[...], e_ref, et_ref) * v


def _rwkv_proj_long_kernel(tiles_per_seq, x_ref, shift_ref, *rest):
    carry_ref = rest[-1]
    x = x_ref[...]
    tm = x.shape[0]

    @pl.when(pl.program_id(0) % tiles_per_seq == 0)
    def _():
        carry_ref[...] = shift_ref[...]

    row = lax.broadcasted_iota(jnp.int32, (tm, D_MODEL), 0)
    xprev = jnp.where(row == 0, carry_ref[...], pltpu.roll(x, 1, 0))
    carry_ref[...] = x[tm - 1:tm, :]
    _rwkv_proj_body(x, xprev, *rest[:-1])


def _rwkv_proj_short_kernel(seq, x_ref, h1_ref, *rest):
    x = x_ref[...]
    t = lax.broadcasted_iota(jnp.int32, x.shape, 0) % seq
    xprev = jnp.where(t >= 1, pltpu.roll(x, 1, 0), h1_ref[...])
    _rwkv_proj_body(x, xprev, *rest)


def _rwkv_proj(x, shift_prev, seq, consts):
    n = x.shape[0]
    tm = min(ROW_TILE // 2, n)
    common = dict(
        grid=(n // tm,),
        out_specs=[_row_spec(D_MODEL, tm)] * 8,
        out_shape=[jax.ShapeDtypeStruct((n, D_MODEL), F32)] * 8,
    )
    const_specs = [c.const_spec() for c in consts]
    const_args = [c.arr for c in consts]
    if seq >= tm:
        tiles_per_seq = seq // tm
        return pl.pallas_call(
            functools.partial(_rwkv_proj_long_kernel, tiles_per_seq),
            in_specs=[_row_spec(D_MODEL, tm),
                      pl.BlockSpec((None, 1, D_MODEL), lambda i: (i // tiles_per_seq, 0, 0))] + const_specs,
            scratch_shapes=[pltpu.VMEM((1, D_MODEL), F32)],
            compiler_params=_params(("arbitrary",)),
            name="rwkv_proj_long", **common,
        )(x, shift_prev[:, None, :], *const_args)
    assert tm % seq == 0
    return pl.pallas_call(
        functools.partial(_rwkv_proj_short_kernel, seq),
        in_specs=[_row_spec(D_MODEL, tm)] * 2 + const_specs,
        compiler_params=_params(("parallel",)),
        name="rwkv_proj_short", **common,
    )(x, _first_rows(shift_prev, seq), *const_args)


def _pair_rows(x, lane_lo):
    zero = jnp.zeros_like(x)
    return jnp.concatenate([jnp.where(lane_lo, x, zero), jnp.where(lane_lo, zero, x)], axis=0)


def _block_diag(a, b):
    zero = jnp.zeros_like(a)
    return jnp.concatenate([jnp.concatenate([a, zero], axis=1), jnp.concatenate([zero, b], axis=1)], axis=0)


def _scan_chunk(load, store_y, n_chains, sub, qs_fn, state_fn):
    C = SCAN_CHUNK
    pairs = range(n_chains)
    shift = int(math.log2(sub))
    row = lax.broadcasted_iota(jnp.int32, (C, 2 * C), 0)
    col = lax.broadcasted_iota(jnp.int32, (C, 2 * C), 1) & (C - 1)
    same = (row >> shift) == (col >> shift)
    strict = same & (row > col)
    incl = same & (row >= col)
    lane_lo = lax.broadcasted_iota(jnp.int32, (C, LANES), 1) < HEAD_SIZE
    r2 = lax.broadcasted_iota(jnp.int32, (C, C), 0)
    c2 = lax.broadcasted_iota(jnp.int32, (C, C), 1)
    tri = jnp.where(((r2 >> shift) == (c2 >> shift)) & (r2 >= c2), 1.0, 0.0).astype(BF16)
    r, lw, k, v, c, b = zip(*[load(p) for p in pairs])

    cum = []
    for p in pairs:
        l_hi, l_mid, l_lo = _split3(lw[p])
        cum.append(_dot(tri, l_hi) + (_dot(tri, l_mid) + _dot(tri, l_lo)))
    p_inc = [jnp.exp(cum[p]) for p in pairs]
    p_inv = [jnp.exp(-cum[p]) for p in pairs]
    p_exc = [jnp.exp(cum[p] - lw[p]) for p in pairs]
    q = [jnp.concatenate([c[p] * p_exc[p], r[p] * p_inc[p]], axis=0).astype(BF16) for p in pairs]
    bt = [(b[p] * p_inv[p]).astype(BF16) for p in pairs]
    kt = [(k[p] * p_inv[p]).astype(BF16) for p in pairs]
    vb = [v[p].astype(BF16) for p in pairs]
    kb = [jnp.concatenate([_pair_rows(bt[p], lane_lo), _pair_rows(kt[p], lane_lo)], axis=0) for p in pairs]
    gram = [_dot(q[p], kb[p], NT) for p in pairs]
    l_cb = [jnp.where(strict, gram[p][:C, :2 * C], 0.0) for p in pairs]
    l_ck = [jnp.where(strict, gram[p][:C, 2 * C:], 0.0).astype(BF16) for p in pairs]
    a_rb = [jnp.where(incl, gram[p][C:, :2 * C], 0.0) for p in pairs]
    a_rk = [jnp.where(incl, gram[p][C:, 2 * C:], 0.0) for p in pairs]
    qs = qs_fn(q)
    v_rows = [_pair_rows(vb[p], lane_lo) for p in pairs]
    u = [-(qs[p][:C] + _dot(l_ck[p], v_rows[p])) for p in pairs]
    m = [l_cb[p].astype(BF16) for p in pairs]
    u = [u[p] - _dot(m[p], _pair_rows(u[p].astype(BF16), lane_lo)) for p in pairs]
    for _ in range(shift - 1):
        m = [_dot(m[p], _pair_rows(m[p], lane_lo)).astype(BF16) for p in pairs]
        u = [u[p] + _dot(m[p], _pair_rows(u[p].astype(BF16), lane_lo)) for p in pairs]
    ub = [u[p].astype(BF16) for p in pairs]
    for p in pairs:
        a = jnp.concatenate([a_rb[p], a_rk[p]], axis=1).astype(BF16)
        uv_rows = jnp.concatenate([_pair_rows(ub[p], lane_lo), v_rows[p]], axis=0)
        store_y(p, qs[p][C:] + _dot(a, uv_rows))
    state_fn(u, v, ub, vb, bt, kt, p_inc)


def _pair_lanes(p):
    return slice(p * LANES, (p + 1) * LANES)


def _diag_blocks_mask():
    sq_row = lax.broadcasted_iota(jnp.int32, (LANES, LANES), 0) < HEAD_SIZE
    sq_col = lax.broadcasted_iota(jnp.int32, (LANES, LANES), 1) < HEAD_SIZE
    return sq_row == sq_col


def _store_pair_state(out_ref, i, p, s_pair):
    out_ref[i, 2 * p] = s_pair[:HEAD_SIZE, :HEAD_SIZE]
    out_ref[i, 2 * p + 1] = s_pair[HEAD_SIZE:, HEAD_SIZE:]


SCAN_SEQS = 2


def _rwkv_scan_kernel(r_ref, lw_ref, k_ref, v_ref, c_ref, b_ref, s0_ref, y_ref, sfin_ref, s_ref):
    C = SCAN_CHUNK
    j = pl.program_id(1)
    chains = [(s, p) for s in range(SCAN_SEQS) for p in range(HEAD_PAIRS)]

    @pl.when(j == 0)
    def _():
        for s, p in chains:
            s_ref[s, p] = _block_diag(s0_ref[s, 2 * p], s0_ref[s, 2 * p + 1])

    diag = _diag_blocks_mask()
    refs = (r_ref, lw_ref, k_ref, v_ref, c_ref, b_ref)

    def load(ch):
        s, p = chains[ch]
        return tuple(ref[s, :, _pair_lanes(p)] for ref in refs)

    def store_y(ch, y):
        s, p = chains[ch]
        y_ref[s, :, _pair_lanes(p)] = y

    def qs_fn(q):
        return [_dot(q[ch], s_ref[s, p].astype(BF16), NT) for ch, (s, p) in enumerate(chains)]

    def state_fn(u, v, ub, vb, bt, kt, p_inc):
        for ch, (s, p) in enumerate(chains):
            upd = _dot(jnp.concatenate([ub[ch], vb[ch]], axis=0), jnp.concatenate([bt[ch], kt[ch]], axis=0), TN)
            s_ref[s, p] = (s_ref[s, p] + jnp.where(diag, upd, 0.0)) * p_inc[ch][C - 1:C, :]

    _scan_chunk(load, store_y, len(chains), C, qs_fn, state_fn)

    @pl.when(j == pl.num_programs(1) - 1)
    def _():
        for s, p in chains:
            _store_pair_state(sfin_ref, s, p, s_ref[s, p])


def _rwkv_scan(r, lw, k, v, c, b, s0, bn, seq):
    assert bn % SCAN_SEQS == 0
    row_spec = pl.BlockSpec((SCAN_SEQS, SCAN_CHUNK, D_MODEL), lambda i, j: (i, j, 0))
    st_spec = pl.BlockSpec((SCAN_SEQS, B_HEADS, HEAD_SIZE, HEAD_SIZE), lambda i, j: (i, 0, 0, 0))
    as_seqs = lambda z: z.reshape(bn, seq, D_MODEL)
    ys, s_new = pl.pallas_call(
        _rwkv_scan_kernel,
        grid=(bn // SCAN_SEQS, seq // SCAN_CHUNK),
        in_specs=[row_spec] * 6 + [st_spec],
        out_specs=[row_spec, st_spec],
        out_shape=[jax.ShapeDtypeStruct((bn, seq, D_MODEL), F32),
                   jax.ShapeDtypeStruct((bn, B_HEADS, HEAD_SIZE, HEAD_SIZE), F32)],
        scratch_shapes=[pltpu.VMEM((SCAN_SEQS, HEAD_PAIRS, LANES, LANES), F32)],
        compiler_params=_params(("parallel", "arbitrary")),
        name="rwkv_scan",
    )(*(as_seqs(z) for z in (r, lw, k, v, c, b)), s0)
    return ys.reshape(bn * seq, D_MODEL), s_new


def _rwkv_scan_short_kernel(sub, r_ref, lw_ref, k_ref, v_ref, c_ref, b_ref, s0_ref, y_ref, sfin_ref):
    C = SCAN_CHUNK
    nb = C // sub
    shift = int(math.log2(sub))
    diag = _diag_blocks_mask()
    row_seq = (lax.broadcasted_iota(jnp.int32, (2 * C, LANES), 0) & (C - 1)) >> shift
    lane_seq = (lax.broadcasted_iota(jnp.int32, (LANES, 2 * C), 1) & (C - 1)) >> shift
    s_prev = {}

    def qs_fn(q):
        out = []
        for p in range(HEAD_PAIRS):
            for i in range(nb):
                s_prev[i, p] = _block_diag(s0_ref[i, 2 * p], s0_ref[i, 2 * p + 1])
            s_stack = jnp.concatenate([s_prev[i, p].astype(BF16) for i in range(nb)], axis=1)
            zero = jnp.zeros_like(q[p])
            q_wide = jnp.concatenate([jnp.where(row_seq == i, q[p], zero) for i in range(nb)], axis=1)
            out.append(_dot(q_wide, s_stack, NT))
        return out

    def state_fn(u, v, ub, vb, bt, kt, p_inc):
        for p in range(HEAD_PAIRS):
            uvt = jnp.concatenate([u[p], v[p]], axis=0).T
            stack = jnp.concatenate([jnp.where(lane_seq == i, uvt, 0.0) for i in range(nb)], axis=0)
            upd = _dot(stack.astype(BF16), jnp.concatenate([bt[p], kt[p]], axis=0))
            for i in range(nb):
                last = i * sub + sub - 1
                s_new = ((s_prev[i, p] + jnp.where(diag, upd[i * LANES:(i + 1) * LANES], 0.0))
                         * p_inc[p][last:last + 1, :])
                _store_pair_state(sfin_ref, i, p, s_new)

    refs = (r_ref, lw_ref, k_ref, v_ref, c_ref, b_ref)

    def load(p):
        return tuple(ref[:, _pair_lanes(p)] for ref in refs)

    def store_y(p, y):
        y_ref[:, _pair_lanes(p)] = y

    _scan_chunk(load, store_y, HEAD_PAIRS, sub, qs_fn, state_fn)


def _rwkv_scan_short(r, lw, k, v, c, b, s0, bn, seq):
    nb = SCAN_CHUNK // seq
    row_spec = pl.BlockSpec((SCAN_CHUNK, D_MODEL), lambda i: (i, 0))
    st_spec = pl.BlockSpec((nb, B_HEADS, HEAD_SIZE, HEAD_SIZE), lambda i: (i, 0, 0, 0))
    return pl.pallas_call(
        functools.partial(_rwkv_scan_short_kernel, seq),
        grid=(bn // nb,),
        in_specs=[row_spec] * 6 + [st_spec],
        out_specs=[row_spec, st_spec],
        out_shape=[jax.ShapeDtypeStruct((bn * seq, D_MODEL), F32),
                   jax.ShapeDtypeStruct((bn, B_HEADS, HEAD_SIZE, HEAD_SIZE), F32)],
        compiler_params=_params(("parallel",)),
        name="rwkv_scan_short",
    )(r, lw, k, v, c, b, s0)


def _rwkv_post_kernel(x_ref, y_ref, bonus_ref, gate_ref, lg_ref, lb_ref, wo_ref, e_ref, et_ref,
                      g_ref, b_ref, o_ref):
    ys = y_ref[...]
    inv = 1.0 / HEAD_SIZE
    m = _head_sum_bcast(ys, e_ref, et_ref) * inv
    yc = ys - m
    var = _head_sum_bcast(yc * yc, e_ref, et_ref) * inv
    yn = yc * lax.rsqrt(var + GN_EPS) * lg_ref[...] + lb_ref[...]
    out = _dot(((yn + bonus_ref[...]) * gate_ref[...]).astype(BF16), wo_ref[...])
    o_ref[...] = _layer_norm(ALPHA * x_ref[...] + out, g_ref[...], b_ref[...])


def _rwkv_post(x, ys, bonus, gate, consts):
    n = x.shape[0]
    tm = min(ROW_TILE, n)
    return pl.pallas_call(
        _rwkv_post_kernel,
        grid=(n // tm,),
        in_specs=[_row_spec(D_MODEL, tm)] * 4 + [c.const_spec() for c in consts],
        out_specs=_row_spec(D_MODEL, tm),
        out_shape=jax.ShapeDtypeStruct((n, D_MODEL), F32),
        compiler_params=_params(("parallel",)),
        name="rwkv_post",
    )(x, ys, bonus, gate, *[c.arr for c in consts])


def _rwkv_mixer(x, shift_prev, s0, seq, w, g, b):
    n = x.shape[0]
    bn = n // seq
    r, lw, k, v, c, bb, bonus, gate = _rwkv_proj(x, shift_prev, seq, w["proj"])
    if seq % SCAN_CHUNK == 0:
        scan = _rwkv_scan
    else:
        assert SCAN_CHUNK % seq == 0 and seq & (seq - 1) == 0 and n % SCAN_CHUNK == 0
        scan = _rwkv_scan_short
    ys, s_new = scan(r, lw, k, v, c, bb, s0, bn, seq)
    out = _rwkv_post(x, ys, bonus, gate, w["post"] + [g, b])
    return out, x.reshape(bn, seq, D_MODEL)[:, -1, :], s_new


def _run_trunk(x3, p4, b_wkv, b_shift, c_conv, W, keep_v):
    bn, seq, _ = x3.shape
    n = bn * seq
    x = x3.reshape(n, D_MODEL)
    p = p4.reshape(DEPTH, n, PLE_DIM)
    new_v, new_wkv, new_shift, new_conv = [], [], [], []
    for i in range(DEPTH):
        j, kind = divmod(i, N_MIXERS)
        ln_g = lambda s: _sel(W["ln_g"], i, s)
        ln_b = lambda s: _sel(W["ln_b"], i, s)
        x = _ffn(x, _sel(W["ffn_w_in"], i, 0), _sel(W["ffn_w_out"], i, 0), ln_g(0), ln_b(0))
        if kind == 0:
            mix = W["a_mix_long"] if seq >= CHUNK else W["a_mix_short"]
            consts = [_sel(W[name], j) for name in ("a_w_in", "a_b_in", "a_ln_g", "a_ln_b")]
            consts += [_sel(mix[0], j), _sel(mix[1], j), _sel(W["a_w_out"], j), ln_g(1), ln_b(1)]
            x, v = _gmlp(x, consts, keep_v)
            if keep_v:
                new_v.append(v.reshape(bn, seq, A_INNER))
        elif kind == 1:
            w = dict(proj=[_sel(a, j) for a in W["b_proj"]] + [_sel(W["e"]), _sel(W["et"])],
                     post=[_sel(a, j) for a in W["b_post"]] + [_sel(W["e"]), _sel(W["et"])])
            x, sh, s = _rwkv_mixer(x, b_shift[j], b_wkv[j], seq, w, ln_g(1), ln_b(1))
            new_shift.append(sh)
            new_wkv.append(s)
        else:
            consts = [_sel(W[name], j) for name in ("c_w_in", "c_conv_w", "c_w_out")] + [ln_g(1), ln_b(1)]
            x, buf = _conv_mixer(x, c_conv[j], seq, consts)
            new_conv.append(buf)
        x = _ffn(x, _sel(W["ffn_w_in"], i, 1), _sel(W["ffn_w_out"], i, 1), ln_g(2), ln_b(2),
                 ple=(_sel(p, i), _sel(W["ple_w_gate"], i), _sel(W["ple_w_proj"], i)))
    a_state = jnp.stack(new_v) if keep_v else None
    return x.reshape(bn, seq, D_MODEL), a_state, jnp.stack(new_wkv), jnp.stack(new_shift), jnp.stack(new_conv)


def _gmlp_mix_mats(w_s, b_s, seq):
    l = min(seq, CHUNK)
    ws = jnp.where(jnp.tril(jnp.ones((l, l), dtype=bool)), w_s[..., :l, :l], 0.0)
    reps = CHUNK // l
    if reps > 1:
        eye = jnp.eye(reps, dtype=F32)
        ws = jnp.einsum("ab,jhts->jhatbs", eye, ws).reshape(-1, A_HEADS, CHUNK, CHUNK)
    bias = jnp.tile(jnp.swapaxes(b_s[..., :l], -1, -2), (1, reps, 1))
    return ws.astype(BF16), bias


def kernel(x_prompt, x_sample, state_b_wkv, state_b_shift, state_c_conv, p_prompt, p_sample, ln_g, ln_b, ffn_w_in, ffn_w_out, ple_w_gate, ple_w_proj, a_w_in, a_b_in, a_ln_g, a_ln_b, a_w_s, a_b_s, a_w_out, b_mu, b_w_rkv, b_w0, b_w1, b_w2, b_a0, b_a1, b_a2, b_g1, b_g2, b_k_k, b_k_a, b_r_k, b_lnx_g, b_lnx_b, b_w_o, c_w_in, c_conv_w, c_w_out):
    bf = lambda w: w.astype(BF16)
    row = lambda w: w.reshape(w.shape[0], 1, -1)
    head_of_lane = jnp.arange(D_MODEL) // HEAD_SIZE
    e = (head_of_lane[:, None] == jnp.arange(LANES)[None, :]).astype(BF16)
    W = dict(
        ln_g=ln_g[:, :, None, :], ln_b=ln_b[:, :, None, :],
        ffn_w_in=bf(ffn_w_in), ffn_w_out=bf(ffn_w_out),
        ple_w_gate=bf(ple_w_gate), ple_w_proj=bf(ple_w_proj),
        a_w_in=bf(a_w_in), a_b_in=row(a_b_in), a_ln_g=row(a_ln_g), a_ln_b=row(a_ln_b), a_w_out=bf(a_w_out),
        a_mix_long=_gmlp_mix_mats(a_w_s, a_b_s, x_prompt.shape[1]),
        a_mix_short=_gmlp_mix_mats(a_w_s, a_b_s, x_sample.shape[1]),
        c_w_in=bf(c_w_in), c_conv_w=c_conv_w, c_w_out=bf(c_w_out),
        b_proj=[b_mu, bf(b_w_rkv), row(b_w0), bf(b_w1), bf(b_w2), row(b_a0), bf(b_a1), bf(b_a2),
                bf(b_g1), bf(b_g2), row(b_k_k), row(b_k_a), row(b_r_k)],
        b_post=[row(b_lnx_g), row(b_lnx_b), bf(b_w_o)],
        e=e, et=e.T,
    )
    bp = x_prompt.shape[0]
    n_b = b_mu.shape[0]
    n_c = c_w_in.shape[0]
    zero_wkv = jnp.zeros((n_b, bp) + state_b_wkv.shape[2:], state_b_wkv.dtype)
    zero_shift = jnp.zeros((n_b, bp, D_MODEL), state_b_shift.dtype)
    zero_conv = jnp.zeros((n_c, bp, CONV_W - 1, D_MODEL), state_c_conv.dtype)
    y_p, _, wkv_p, shift_p, conv_p = _run_trunk(x_prompt, p_prompt, zero_wkv, zero_shift, zero_conv, W, False)
    y_s, a_v_s, wkv_s, shift_s, conv_s = _run_trunk(x_sample, p_sample, state_b_wkv, state_b_shift, state_c_conv, W, True)
    return (y_p, y_s, a_v_s, wkv_p, shift_p, conv_p, wkv_s, shift_s, conv_s)
```

```python
import collections
import functools
import math

import jax
import jax.numpy as jnp
from jax import lax
from jax.experimental import pallas as pl
from jax.experimental.pallas import tpu as pltpu

F32 = jnp.float32
BF16 = jnp.bfloat16

D_MODEL = 1024
DEPTH = 4
N_MIXERS = 3
CHUNK = 128
A_INNER = 2 * D_MODEL
A_HEADS = 8
A_GROUP = A_INNER // A_HEADS
HEAD_SIZE = 64
B_HEADS = D_MODEL // HEAD_SIZE
CONV_W = 3
D_FF = 2816
PLE_DIM = 256
ALPHA = (2 * DEPTH) ** 0.25
LN_EPS = 1e-5
GN_EPS = 64e-5

LANES = 128
MXU_DIM = 256
ROW_TILE = 512
SCAN_CHUNK = 64
HEAD_PAIRS = B_HEADS // 2
VMEM_LIMIT = 56 * 1024 * 1024

NN = (((1,), (0,)), ((), ()))
NT = (((1,), (1,)), ((), ()))
TN = (((0,), (0,)), ((), ()))


def _dot(a, b, dims=NN):
    return lax.dot_general(a, b, dims, preferred_element_type=F32)


def _split2(x):
    hi = x.astype(BF16)
    lo = (x - hi.astype(F32)).astype(BF16)
    return hi, lo


def _split3(x):
    hi = x.astype(BF16)
    r1 = x - hi.astype(F32)
    mid = r1.astype(BF16)
    lo = (r1 - mid.astype(F32)).astype(BF16)
    return hi, mid, lo


def _layer_norm(x, g, b, eps=LN_EPS):
    mu = jnp.mean(x, axis=-1, keepdims=True)
    xc = x - mu
    var = jnp.mean(xc * xc, axis=-1, keepdims=True)
    return xc * lax.rsqrt(var + eps) * g + b


def _sigmoid(x):
    return 1.0 / (1.0 + jnp.exp(-x))


class _Sel(collections.namedtuple("_Sel", ["arr", "idx"])):
    def const_spec(self):
        k = len(self.idx)
        rest = self.arr.shape[k:]
        idx = self.idx
        return pl.BlockSpec((None,) * k + rest, lambda *_: idx + (0,) * len(rest),
                            pipeline_mode=pl.Buffered(1))

    def row_spec(self, tm):
        k = len(self.idx)
        idx = self.idx
        return pl.BlockSpec((None,) * k + (tm, self.arr.shape[-1]), lambda i: idx + (i, 0))


def _sel(arr, *idx):
    return _Sel(arr, tuple(idx))


def _row_spec(width, tm=ROW_TILE):
    return pl.BlockSpec((tm, width), lambda i: (i, 0))


def _params(sem):
    return pltpu.CompilerParams(dimension_semantics=sem, vmem_limit_bytes=VMEM_LIMIT)


FF_BLOCKS = ((0, 6 * MXU_DIM), (6 * MXU_DIM, D_FF))
assert D_FF % MXU_DIM == 0


WO_CAST_ROWS = 128


def _ffn_kernel(with_ple, with_cast, x_ref, wi_ref, wo_ref, g_ref, b_ref, *rest):
    rest = list(rest)
    if with_ple:
        p_ref, wg_ref, wp_ref = rest[:3]
        del rest[:3]
    if with_cast:
        next_wi_ref, next_wo_ref = rest[:2]
        del rest[:2]
    o_ref = rest[0]
    x = x_ref[...]
    xb = x.astype(BF16)
    acc = None
    for lo, hi in FF_BLOCKS:
        gate = _dot(xb, wi_ref[:, lo:hi])
        up = _dot(xb, wi_ref[:, D_FF + lo:D_FF + hi])
        act = (gate * _sigmoid(gate) * up).astype(BF16)
        part = _dot(act, wo_ref[lo:hi, :])
        acc = part if acc is None else acc + part
    y = _layer_norm(ALPHA * x + 0.5 * acc, g_ref[...], b_ref[...])
    if with_ple:
        gate = _sigmoid(_dot(y.astype(BF16), wg_ref[...]))
        y = y + gate * _dot(p_ref[...].astype(BF16), wp_ref[...])
    o_ref[...] = y
    if with_cast:
        cast_wi_ref, cast_wo_ref = rest[1:]
        cast_wi_ref[...] = next_wi_ref[...].astype(BF16)
        cast_wo_ref[...] = next_wo_ref[...].astype(BF16)


def _ffn(x, wi, wo, g, b, ple=None, cast_next=None):
    n = x.shape[0]
    tm = min(ROW_TILE, n)
    steps = n // tm
    consts = [wi, wo, g, b]
    args = [x] + [c.arr for c in consts]
    specs = [_row_spec(D_MODEL, tm)] + [c.const_spec() for c in consts]
    out_specs = [_row_spec(D_MODEL, tm)]
    out_shape = [jax.ShapeDtypeStruct((n, D_MODEL), F32)]
    if ple is not None:
        p, wg, wp = ple
        args += [p.arr, wg.arr, wp.arr]
        specs += [p.row_spec(tm), wg.const_spec(), wp.const_spec()]
    if cast_next is not None:
        nwi, nwo = cast_next
        wi_rows = D_MODEL // steps
        wo_steps = D_FF // WO_CAST_ROWS
        assert D_MODEL % steps == 0 and wi_rows % 16 == 0 and D_FF % WO_CAST_ROWS == 0 and wo_steps <= steps
        lead = (None,) * len(nwi.idx)
        wi_idx, wo_idx = nwi.idx, nwo.idx
        wo_step = lambda t: jnp.minimum(t, wo_steps - 1)
        args += [nwi.arr, nwo.arr]
        specs += [pl.BlockSpec(lead + (wi_rows, 2 * D_FF), lambda t: wi_idx + (t, 0)),
                  pl.BlockSpec(lead + (WO_CAST_ROWS, D_MODEL), lambda t: wo_idx + (wo_step(t), 0))]
        out_specs += [pl.BlockSpec((wi_rows, 2 * D_FF), lambda t: (t, 0)),
                      pl.BlockSpec((WO_CAST_ROWS, D_MODEL), lambda t: (wo_step(t), 0))]
        out_shape += [jax.ShapeDtypeStruct((D_MODEL, 2 * D_FF), BF16), jax.ShapeDtypeStruct((D_FF, D_MODEL), BF16)]
    res = pl.pallas_call(
        functools.partial(_ffn_kernel, ple is not None, cast_next is not None),
        grid=(steps,),
        in_specs=specs,
        out_specs=out_specs,
        out_shape=out_shape,
        compiler_params=_params(("arbitrary",)),
        name="ffn_ple" if ple is not None else "ffn",
    )(*args)
    return res if cast_next is not None else res[0]


def _gmlp_kernel(keep_v, x_ref, wi_ref, bi_ref, lg_ref, lb_ref, ws_ref, bs_ref, wo_ref,
                 g_ref, b_ref, *rest):
    if keep_v:
        o_ref, v_ref, y_ref = rest
    else:
        o_ref, y_ref = rest
    tm = x_ref.shape[0]
    hm = tm // 2
    halves = [slice(h * hm, (h + 1) * hm) for h in range(2)]
    xs = [x_ref[rows, :] for rows in halves]
    zs = [_dot(x.astype(BF16), wi_ref[...]) + bi_ref[...] for x in xs]
    us, vbs = [], []
    for rows, z in zip(halves, zs):
        z = 0.5 * z * (1.0 + lax.erf(z * (1.0 / math.sqrt(2.0))))
        v = _layer_norm(z[:, A_INNER:], lg_ref[...], lb_ref[...])
        if keep_v:
            v_ref[rows, :] = v
        us.append(z[:, :A_INNER])
        vbs.append(v.astype(BF16))
    for half, u, vb in zip(halves, us, vbs):
        for c in range(hm // CHUNK):
            rows = slice(c * CHUNK, (c + 1) * CHUNK)
            out_rows = slice(half.start + c * CHUNK, half.start + (c + 1) * CHUNK)
            for h in range(A_HEADS):
                cols = slice(h * A_GROUP, (h + 1) * A_GROUP)
                mixed = _dot(ws_ref[h], vb[rows, cols]) + bs_ref[:, h:h + 1]
                y_ref[out_rows, cols] = (u[rows, cols] * mixed).astype(BF16)
    for rows, x in zip(halves, xs):
        out = _dot(y_ref[rows, :], wo_ref[...])
        o_ref[rows, :] = _layer_norm(ALPHA * x + out, g_ref[...], b_ref[...])


def _gmlp(x, consts, keep_v):
    n = x.shape[0]
    tm = min(ROW_TILE, n)
    out_shape = [jax.ShapeDtypeStruct((n, D_MODEL), F32)]
    out_specs = [_row_spec(D_MODEL, tm)]
    if keep_v:
        out_shape.append(jax.ShapeDtypeStruct((n, A_INNER), F32))
        out_specs.append(_row_spec(A_INNER, tm))
    res = pl.pallas_call(
        functools.partial(_gmlp_kernel, keep_v),
        grid=(n // tm,),
        in_specs=[_row_spec(D_MODEL, tm)] + [c.const_spec() for c in consts],
        out_specs=out_specs,
        out_shape=out_shape,
        scratch_shapes=[pltpu.VMEM((tm, A_INNER), BF16)],
        compiler_params=_params(("parallel",)),
        name="gmlp",
    )(x, *[c.arr for c in consts])
    return (res[0], res[1]) if keep_v else (res[0], None)


def _conv_tail(x, bg, conv, wo_ref, g_ref, b_ref, o_ref):
    out = _dot((bg * conv).astype(BF16), wo_ref[...])
    o_ref[...] = _layer_norm(ALPHA * x + out, g_ref[...], b_ref[...])


def _conv_long_kernel(tiles_per_seq, x_ref, buf_ref, wi_ref, cw_ref, wo_ref, g_ref, b_ref,
                      o_ref, tail_ref, carry_ref):
    x = x_ref[...]
    tm = x.shape[0]
    h3 = _dot(x.astype(BF16), wi_ref[...])
    bg = h3[:, :D_MODEL]
    z = h3[:, D_MODEL:2 * D_MODEL] * h3[:, 2 * D_MODEL:]

    @pl.when(pl.program_id(0) % tiles_per_seq == 0)
    def _():
        carry_ref[...] = buf_ref[0]

    row = lax.broadcasted_iota(jnp.int32, (tm, D_MODEL), 0)
    prev1 = carry_ref[7:8, :]
    prev2 = carry_ref[6:7, :]
    z1 = jnp.where(row == 0, prev1, pltpu.roll(z, 1, 0))
    z2 = jnp.where(row == 0, prev2, jnp.where(row == 1, prev1, pltpu.roll(z, 2, 0)))
    conv = cw_ref[0:1, :] * z2 + cw_ref[1:2, :] * z1 + cw_ref[2:3, :] * z
    tail = z[tm - 8:, :]
    carry_ref[...] = tail
    tail_ref[0] = tail
    _conv_tail(x, bg, conv, wo_ref, g_ref, b_ref, o_ref)


def _conv_short_kernel(seq, x_ref, h1_ref, h2_ref, wi_ref, cw_ref, wo_ref, g_ref, b_ref,
                       o_ref, z_ref):
    x = x_ref[...]
    tm = x.shape[0]
    h3 = _dot(x.astype(BF16), wi_ref[...])
    bg = h3[:, :D_MODEL]
    z = h3[:, D_MODEL:2 * D_MODEL] * h3[:, 2 * D_MODEL:]
    t = lax.broadcasted_iota(jnp.int32, (tm, D_MODEL), 0) % seq
    z1 = jnp.where(t >= 1, pltpu.roll(z, 1, 0), h1_ref[...])
    z2 = jnp.where(t >= 2, pltpu.roll(z, 2, 0), h2_ref[...])
    conv = cw_ref[0:1, :] * z2 + cw_ref[1:2, :] * z1 + cw_ref[2:3, :] * z
    z_ref[...] = z
    _conv_tail(x, bg, conv, wo_ref, g_ref, b_ref, o_ref)


def _first_rows(state_rows, seq):
    bn = state_rows.shape[0]
    out = jnp.zeros((bn, seq, D_MODEL), F32).at[:, 0, :].set(state_rows)
    return out.reshape(bn * seq, D_MODEL)


def _conv_mixer(x, buf_prev, seq, consts):
    n = x.shape[0]
    bn = n // seq
    tm = min(ROW_TILE, n)
    const_specs = [c.const_spec() for c in consts]
    const_args = [c.arr for c in consts]
    if seq >= tm:
        tiles_per_seq = seq // tm
        buf8 = jnp.concatenate([jnp.zeros((bn, 6, D_MODEL), F32), buf_prev], axis=1)
        out, tails = pl.pallas_call(
            functools.partial(_conv_long_kernel, tiles_per_seq),
            grid=(n // tm,),
            in_specs=[_row_spec(D_MODEL, tm),
                      pl.BlockSpec((1, 8, D_MODEL), lambda i: (i // tiles_per_seq, 0, 0))] + const_specs,
            out_specs=[_row_spec(D_MODEL, tm), pl.BlockSpec((1, 8, D_MODEL), lambda i: (i, 0, 0))],
            out_shape=[jax.ShapeDtypeStruct((n, D_MODEL), F32),
                       jax.ShapeDtypeStruct((n // tm, 8, D_MODEL), F32)],
            scratch_shapes=[pltpu.VMEM((8, D_MODEL), F32)],
            compiler_params=_params(("arbitrary",)),
            name="conv_long",
        )(x, buf8, *const_args)
        new_buf = tails[tiles_per_seq - 1::tiles_per_seq, 6:8, :]
        return out, new_buf
    assert tm % seq == 0 and seq >= CONV_W - 1
    h1 = _first_rows(buf_prev[:, 1, :], seq)
    h2 = _first_rows(buf_prev[:, 0, :], seq) + jnp.roll(h1, 1, axis=0)
    out, z = pl.pallas_call(
        functools.partial(_conv_short_kernel, seq),
        grid=(n // tm,),
        in_specs=[_row_spec(D_MODEL, tm)] * 3 + const_specs,
        out_specs=[_row_spec(D_MODEL, tm)] * 2,
        out_shape=[jax.ShapeDtypeStruct((n, D_MODEL), F32)] * 2,
        compiler_params=_params(("parallel",)),
        name="conv_short",
    )(x, h1, h2, *const_args)
    new_buf = z.reshape(bn, seq, D_MODEL)[:, seq - (CONV_W - 1):, :]
    return out, new_buf


def _head_sum_bcast(x, e_ref, et_ref):
    s = _dot(x.astype(BF16), e_ref[...])
    s_hi, s_lo = _split2(s)
    return _dot(s_hi, et_ref[...]) + _dot(s_lo, et_ref[...])


def _rwkv_proj_body(x, xprev, mu_ref, wrkv_ref, w0_ref, w1_ref, w2_ref, a0_ref, a1_ref,
                    a2_ref, g1_ref, g2_ref, kk_ref, ka_ref, rk_ref, e_ref, et_ref,
                    r_out, lw_out, k_out, v_out, c_out, b_out, bonus_out, g_out):
    xx = xprev - x
    mix = lambda i: (x + xx * mu_ref[i:i + 1, :]).astype(BF16)
    r = _dot(mix(0), wrkv_ref[0])
    k = _dot(mix(2), wrkv_ref[1])
    v = _dot(mix(3), wrkv_ref[2])
    zw = w0_ref[...] + _dot(jnp.tanh(_dot(mix(1), w1_ref[...])).astype(BF16), w2_ref[...])
    lw_out[...] = -_sigmoid(zw) * math.exp(-0.5)
    a = _sigmoid(a0_ref[...] + _dot(_dot(mix(4), a1_ref[...]).astype(BF16), a2_ref[...]))
    g_out[...] = _dot(_sigmoid(_dot(mix(5), g1_ref[...])).astype(BF16), g2_ref[...])
    kk = k * kk_ref[...]
    norm = jnp.sqrt(_head_sum_bcast(kk * kk, e_ref, et_ref))
    c = kk / jnp.maximum(norm, 1e-12)
    kmod = k * (1.0 + (a - 1.0) * ka_ref[...])
    r_out[...] = r
    k_out[...] = kmod
    v_out[...] = v
    c_out[...] = c
    b_out[...] = c * a
    bonus_out[...] = _head_sum_bcast(r * kmod * rk_ref[...], e_ref, et_ref) * v


def _rwkv_proj_long_kernel(tiles_per_seq, x_ref, shift_ref, *rest):
    carry_ref = rest[-1]
    x = x_ref[...]
    tm = x.shape[0]

    @pl.when(pl.program_id(0) % tiles_per_seq == 0)
    def _():
        carry_ref[...] = shift_ref[...]

    row = lax.broadcasted_iota(jnp.int32, (tm, D_MODEL), 0)
    xprev = jnp.where(row == 0, carry_ref[...], pltpu.roll(x, 1, 0))
    carry_ref[...] = x[tm - 1:tm, :]
    _rwkv_proj_body(x, xprev, *rest[:-1])


def _rwkv_proj_short_kernel(seq, x_ref, h1_ref, *rest):
    x = x_ref[...]
    t = lax.broadcasted_iota(jnp.int32, x.shape, 0) % seq
    xprev = jnp.where(t >= 1, pltpu.roll(x, 1, 0), h1_ref[...])
    _rwkv_proj_body(x, xprev, *rest)


def _rwkv_proj(x, shift_prev, seq, consts):
    n = x.shape[0]
    tm = min(ROW_TILE // 2, n)
    common = dict(
        grid=(n // tm,),
        out_specs=[_row_spec(D_MODEL, tm)] * 8,
        out_shape=[jax.ShapeDtypeStruct((n, D_MODEL), F32)] * 8,
    )
    const_specs = [c.const_spec() for c in consts]
    const_args = [c.arr for c in consts]
    if seq >= tm:
        tiles_per_seq = seq // tm
        return pl.pallas_call(
            functools.partial(_rwkv_proj_long_kernel, tiles_per_seq),
            in_specs=[_row_spec(D_MODEL, tm),
                      pl.BlockSpec((None, 1, D_MODEL), lambda i: (i // tiles_per_seq, 0, 0))] + const_specs,
            scratch_shapes=[pltpu.VMEM((1, D_MODEL), F32)],
            compiler_params=_params(("arbitrary",)),
            name="rwkv_proj_long", **common,
        )(x, shift_prev[:, None, :], *const_args)
    assert tm % seq == 0
    return pl.pallas_call(
        functools.partial(_rwkv_proj_short_kernel, seq),
        in_specs=[_row_spec(D_MODEL, tm)] * 2 + const_specs,
        compiler_params=_params(("parallel",)),
        name="rwkv_proj_short", **common,
    )(x, _first_rows(shift_prev, seq), *const_args)


def _pair_rows(x, lane_lo):
    zero = jnp.zeros_like(x)
    return jnp.concatenate([jnp.where(lane_lo, x, zero), jnp.where(lane_lo, zero, x)], axis=0)


def _block_diag(a, b):
    zero = jnp.zeros_like(a)
    return jnp.concatenate([jnp.concatenate([a, zero], axis=1), jnp.concatenate([zero, b], axis=1)], axis=0)


def _scan_chunk(load, store_y, n_chains, sub, qs_fn, state_fn):
    C = SCAN_CHUNK
    pairs = range(n_chains)
    shift = int(math.log2(sub))
    row = lax.broadcasted_iota(jnp.int32, (C, 2 * C), 0)
    col = lax.broadcasted_iota(jnp.int32, (C, 2 * C), 1) & (C - 1)
    same = (row >> shift) == (col >> shift)
    strict = same & (row > col)
    incl = same & (row >= col)
    lane_lo = lax.broadcasted_iota(jnp.int32, (C, LANES), 1) < HEAD_SIZE
    r2 = lax.broadcasted_iota(jnp.int32, (C, C), 0)
    c2 = lax.broadcasted_iota(jnp.int32, (C, C), 1)
    tri = jnp.where(((r2 >> shift) == (c2 >> shift)) & (r2 >= c2), 1.0, 0.0).astype(BF16)
    r, lw, k, v, c, b = zip(*[load(p) for p in pairs])

    cum = []
    for p in pairs:
        l_hi, l_mid, l_lo = _split3(lw[p])
        cum.append(_dot(tri, l_hi) + (_dot(tri, l_mid) + _dot(tri, l_lo)))
    p_inc = [jnp.exp(cum[p]) for p in pairs]
    p_inv = [jnp.exp(-cum[p]) for p in pairs]
    p_exc = [jnp.exp(cum[p] - lw[p]) for p in pairs]
    q = [jnp.concatenate([c[p] * p_exc[p], r[p] * p_inc[p]], axis=0).astype(BF16) for p in pairs]
    bt = [(b[p] * p_inv[p]).astype(BF16) for p in pairs]
    kt = [(k[p] * p_inv[p]).astype(BF16) for p in pairs]
    vb = [v[p].astype(BF16) for p in pairs]
    kb = [jnp.concatenate([_pair_rows(bt[p], lane_lo), _pair_rows(kt[p], lane_lo)], axis=0) for p in pairs]
    gram = [_dot(q[p], kb[p], NT) for p in pairs]
    l_cb = [jnp.where(strict, gram[p][:C, :2 * C], 0.0) for p in pairs]
    l_ck = [jnp.where(strict, gram[p][:C, 2 * C:], 0.0).astype(BF16) for p in pairs]
    a_rb = [jnp.where(incl, gram[p][C:, :2 * C], 0.0) for p in pairs]
    a_rk = [jnp.where(incl, gram[p][C:, 2 * C:], 0.0) for p in pairs]
    qs = qs_fn(q)
    v_rows = [_pair_rows(vb[p], lane_lo) for p in pairs]
    u = [-(qs[p][:C] + _dot(l_ck[p], v_rows[p])) for p in pairs]
    m = [l_cb[p].astype(BF16) for p in pairs]
    u = [u[p] - _dot(m[p], _pair_rows(u[p].astype(BF16), lane_lo)) for p in pairs]
    for _ in range(shift - 1):
        m = [_dot(m[p], _pair_rows(m[p], lane_lo)).astype(BF16) for p in pairs]
        u = [u[p] + _dot(m[p], _pair_rows(u[p].astype(BF16), lane_lo)) for p in pairs]
    ub = [u[p].astype(BF16) for p in pairs]
    for p in pairs:
        a = jnp.concatenate([a_rb[p], a_rk[p]], axis=1).astype(BF16)
        uv_rows = jnp.concatenate([_pair_rows(ub[p], lane_lo), v_rows[p]], axis=0)
        store_y(p, qs[p][C:] + _dot(a, uv_rows))
    state_fn(u, v, ub, vb, bt, kt, p_inc)


def _pair_lanes(p):
    return slice(p * LANES, (p + 1) * LANES)


def _diag_blocks_mask():
    sq_row = lax.broadcasted_iota(jnp.int32, (LANES, LANES), 0) < HEAD_SIZE
    sq_col = lax.broadcasted_iota(jnp.int32, (LANES, LANES), 1) < HEAD_SIZE
    return sq_row == sq_col


def _store_pair_state(out_ref, i, p, s_pair):
    out_ref[i, 2 * p] = s_pair[:HEAD_SIZE, :HEAD_SIZE]
    out_ref[i, 2 * p + 1] = s_pair[HEAD_SIZE:, HEAD_SIZE:]


SCAN_SEQS = 4


def _rwkv_scan_kernel(r_ref, lw_ref, k_ref, v_ref, c_ref, b_ref, s0_ref, y_ref, sfin_ref, s_ref):
    C = SCAN_CHUNK
    j = pl.program_id(1)
    chains = [(s, p) for s in range(SCAN_SEQS) for p in range(HEAD_PAIRS)]

    @pl.when(j == 0)
    def _():
        for s, p in chains:
            s_ref[s, p] = _block_diag(s0_ref[s, 2 * p], s0_ref[s, 2 * p + 1])

    diag = _diag_blocks_mask()
    refs = (r_ref, lw_ref, k_ref, v_ref, c_ref, b_ref)

    def load(ch):
        s, p = chains[ch]
        return tuple(ref[s, :, _pair_lanes(p)] for ref in refs)

    def store_y(ch, y):
        s, p = chains[ch]
        y_ref[s, :, _pair_lanes(p)] = y

    def qs_fn(q):
        return [_dot(q[ch], s_ref[s, p].astype(BF16), NT) for ch, (s, p) in enumerate(chains)]

    def state_fn(u, v, ub, vb, bt, kt, p_inc):
        for ch, (s, p) in enumerate(chains):
            upd = _dot(jnp.concatenate([ub[ch], vb[ch]], axis=0), jnp.concatenate([bt[ch], kt[ch]], axis=0), TN)
            s_ref[s, p] = (s_ref[s, p] + jnp.where(diag, upd, 0.0)) * p_inc[ch][C - 1:C, :]

    _scan_chunk(load, store_y, len(chains), C, qs_fn, state_fn)

    @pl.when(j == pl.num_programs(1) - 1)
    def _():
        for s, p in chains:
            _store_pair_state(sfin_ref, s, p, s_ref[s, p])


def _rwkv_scan(r, lw, k, v, c, b, s0, bn, seq):
    assert bn % SCAN_SEQS == 0
    row_spec = pl.BlockSpec((SCAN_SEQS, SCAN_CHUNK, D_MODEL), lambda i, j: (i, j, 0))
    st_spec = pl.BlockSpec((SCAN_SEQS, B_HEADS, HEAD_SIZE, HEAD_SIZE), lambda i, j: (i, 0, 0, 0))
    as_seqs = lambda z: z.reshape(bn, seq, D_MODEL)
    ys, s_new = pl.pallas_call(
        _rwkv_scan_kernel,
        grid=(bn // SCAN_SEQS, seq // SCAN_CHUNK),
        in_specs=[row_spec] * 6 + [st_spec],
        out_specs=[row_spec, st_spec],
        out_shape=[jax.ShapeDtypeStruct((bn, seq, D_MODEL), F32),
                   jax.ShapeDtypeStruct((bn, B_HEADS, HEAD_SIZE, HEAD_SIZE), F32)],
        scratch_shapes=[pltpu.VMEM((SCAN_SEQS, HEAD_PAIRS, LANES, LANES), F32)],
        compiler_params=_params(("parallel", "arbitrary")),
        name="rwkv_scan",
    )(*(as_seqs(z) for z in (r, lw, k, v, c, b)), s0)
    return ys.reshape(bn * seq, D_MODEL), s_new


def _rwkv_scan_short_kernel(sub, r_ref, lw_ref, k_ref, v_ref, c_ref, b_ref, s0_ref, y_ref, sfin_ref):
    C = SCAN_CHUNK
    nb = C // sub
    shift = int(math.log2(sub))
    diag = _diag_blocks_mask()
    row_seq = (lax.broadcasted_iota(jnp.int32, (2 * C, LANES), 0) & (C - 1)) >> shift
    lane_seq = (lax.broadcasted_iota(jnp.int32, (LANES, 2 * C), 1) & (C - 1)) >> shift
    s_prev = {}

    def qs_fn(q):
        out = []
        for p in range(HEAD_PAIRS):
            for i in range(nb):
                s_prev[i, p] = _block_diag(s0_ref[i, 2 * p], s0_ref[i, 2 * p + 1])
            s_stack = jnp.concatenate([s_prev[i, p].astype(BF16) for i in range(nb)], axis=1)
            zero = jnp.zeros_like(q[p])
            q_wide = jnp.concatenate([jnp.where(row_seq == i, q[p], zero) for i in range(nb)], axis=1)
            out.append(_dot(q_wide, s_stack, NT))
        return out

    def state_fn(u, v, ub, vb, bt, kt, p_inc):
        for p in range(HEAD_PAIRS):
            uvt = jnp.concatenate([u[p], v[p]], axis=0).T
            stack = jnp.concatenate([jnp.where(lane_seq == i, uvt, 0.0) for i in range(nb)], axis=0)
            upd = _dot(stack.astype(BF16), jnp.concatenate([bt[p], kt[p]], axis=0))
            for i in range(nb):
                last = i * sub + sub - 1
                s_new = ((s_prev[i, p] + jnp.where(diag, upd[i * LANES:(i + 1) * LANES], 0.0))
                         * p_inc[p][last:last + 1, :])
                _store_pair_state(sfin_ref, i, p, s_new)

    refs = (r_ref, lw_ref, k_ref, v_ref, c_ref, b_ref)

    def load(p):
        return tuple(ref[:, _pair_lanes(p)] for ref in refs)

    def store_y(p, y):
        y_ref[:, _pair_lanes(p)] = y

    _scan_chunk(load, store_y, HEAD_PAIRS, sub, qs_fn, state_fn)


def _rwkv_scan_short(r, lw, k, v, c, b, s0, bn, seq):
    nb = SCAN_CHUNK // seq
    row_spec = pl.BlockSpec((SCAN_CHUNK, D_MODEL), lambda i: (i, 0))
    st_spec = pl.BlockSpec((nb, B_HEADS, HEAD_SIZE, HEAD_SIZE), lambda i: (i, 0, 0, 0))
    return pl.pallas_call(
        functools.partial(_rwkv_scan_short_kernel, seq),
        grid=(bn // nb,),
        in_specs=[row_spec] * 6 + [st_spec],
        out_specs=[row_spec, st_spec],
        out_shape=[jax.ShapeDtypeStruct((bn * seq, D_MODEL), F32),
                   jax.ShapeDtypeStruct((bn, B_HEADS, HEAD_SIZE, HEAD_SIZE), F32)],
        compiler_params=_params(("parallel",)),
        name="rwkv_scan_short",
    )(r, lw, k, v, c, b, s0)


def _rwkv_post_kernel(x_ref, y_ref, bonus_ref, gate_ref, lg_ref, lb_ref, wo_ref, e_ref, et_ref,
                      g_ref, b_ref, o_ref):
    ys = y_ref[...]
    inv = 1.0 / HEAD_SIZE
    m = _head_sum_bcast(ys, e_ref, et_ref) * inv
    yc = ys - m
    var = _head_sum_bcast(yc * yc, e_ref, et_ref) * inv
    yn = yc * lax.rsqrt(var + GN_EPS) * lg_ref[...] + lb_ref[...]
    out = _dot(((yn + bonus_ref[...]) * gate_ref[...]).astype(BF16), wo_ref[...])
    o_ref[...] = _layer_norm(ALPHA * x_ref[...] + out, g_ref[...], b_ref[...])


def _rwkv_post(x, ys, bonus, gate, consts):
    n = x.shape[0]
    tm = min(ROW_TILE, n)
    return pl.pallas_call(
        _rwkv_post_kernel,
        grid=(n // tm,),
        in_specs=[_row_spec(D_MODEL, tm)] * 4 + [c.const_spec() for c in consts],
        out_specs=_row_spec(D_MODEL, tm),
        out_shape=jax.ShapeDtypeStruct((n, D_MODEL), F32),
        compiler_params=_params(("parallel",)),
        name="rwkv_post",
    )(x, ys, bonus, gate, *[c.arr for c in consts])


def _rwkv_mixer(x, shift_prev, s0, seq, w, g, b):
    n = x.shape[0]
    bn = n // seq
    r, lw, k, v, c, bb, bonus, gate = _rwkv_proj(x, shift_prev, seq, w["proj"])
    if seq % SCAN_CHUNK == 0:
        scan = _rwkv_scan
    else:
        assert SCAN_CHUNK % seq == 0 and seq & (seq - 1) == 0 and n % SCAN_CHUNK == 0
        scan = _rwkv_scan_short
    ys, s_new = scan(r, lw, k, v, c, bb, s0, bn, seq)
    out = _rwkv_post(x, ys, bonus, gate, w["post"] + [g, b])
    return out, x.reshape(bn, seq, D_MODEL)[:, -1, :], s_new


def _run_trunks(x_long, x_short, p_long, p_short, states_long, states_short, W):
    shapes = [x_long.shape, x_short.shape]
    seqs = [s[1] for s in shapes]
    assert seqs[0] >= CHUNK > seqs[1]
    xs = [x_long.reshape(-1, D_MODEL), x_short.reshape(-1, D_MODEL)]
    ps = [p_long.reshape(DEPTH, -1, PLE_DIM), p_short.reshape(DEPTH, -1, PLE_DIM)]
    states = [states_long, states_short]
    new_v, new_wkv, new_shift, new_conv = [], ([], []), ([], []), ([], [])
    ffn_w = (_sel(W["ffn_w_in"][0, 0].astype(BF16)), _sel(W["ffn_w_out"][0, 0].astype(BF16)))

    def ffn_pair(xs, ffn_w, i, s, ple=None):
        nxt = (i, s + 1) if s == 0 else (i + 1, 0)
        cast_next = None
        if nxt[0] < DEPTH:
            cast_next = (_sel(W["ffn_w_in"], *nxt), _sel(W["ffn_w_out"], *nxt))
        norm = (_sel(W["ln_g"], i, 2 * s), _sel(W["ln_b"], i, 2 * s))
        ple_of = lambda t: None if ple is None else (_sel(ps[t], i),) + ple
        res = _ffn(xs[0], *ffn_w, *norm, ple=ple_of(0), cast_next=cast_next)
        x_short = _ffn(xs[1], *ffn_w, *norm, ple=ple_of(1))
        if cast_next is None:
            return [res, x_short], None
        return [res[0], x_short], (_sel(res[1]), _sel(res[2]))

    for i in range(DEPTH):
        j, kind = divmod(i, N_MIXERS)
        ln_g = lambda s: _sel(W["ln_g"], i, s)
        ln_b = lambda s: _sel(W["ln_b"], i, s)
        xs, ffn_w = ffn_pair(xs, ffn_w, i, 0)
        if kind == 0:
            for t in range(2):
                mix = W["a_mix_long"] if seqs[t] >= CHUNK else W["a_mix_short"]
                consts = [_sel(W[name], j) for name in ("a_w_in", "a_b_in", "a_ln_g", "a_ln_b")]
                consts += [_sel(mix[0], j), _sel(mix[1], j), _sel(W["a_w_out"], j), ln_g(1), ln_b(1)]
                xs[t], v = _gmlp(xs[t], consts, keep_v=(t == 1))
            new_v.append(v.reshape(shapes[1][0], seqs[1], A_INNER))
        elif kind == 1:
            w = dict(proj=[_sel(a, j) for a in W["b_proj"]] + [_sel(W["e"]), _sel(W["et"])],
                     post=[_sel(a, j) for a in W["b_post"]] + [_sel(W["e"]), _sel(W["et"])])
            for t in range(2):
                wkv, shift, _ = states[t]
                xs[t], sh, s = _rwkv_mixer(xs[t], shift[j], wkv[j], seqs[t], w, ln_g(1), ln_b(1))
                new_shift[t].append(sh)
                new_wkv[t].append(s)
        else:
            consts = [_sel(W[name], j) for name in ("c_w_in", "c_conv_w", "c_w_out")] + [ln_g(1), ln_b(1)]
            for t in range(2):
                xs[t], buf = _conv_mixer(xs[t], states[t][2][j], seqs[t], consts)
                new_conv[t].append(buf)
        xs, ffn_w = ffn_pair(xs, ffn_w, i, 1, ple=(_sel(W["ple_w_gate"], i), _sel(W["ple_w_proj"], i)))
    outs = [(xs[t].reshape(shapes[t]), jnp.stack(new_wkv[t]), jnp.stack(new_shift[t]), jnp.stack(new_conv[t]))
            for t in range(2)]
    return outs[0], outs[1], jnp.stack(new_v)


def _gmlp_mix_mats(w_s, b_s, seq):
    l = min(seq, CHUNK)
    ws = jnp.where(jnp.tril(jnp.ones((l, l), dtype=bool)), w_s[..., :l, :l], 0.0)
    reps = CHUNK // l
    if reps > 1:
        eye = jnp.eye(reps, dtype=F32)
        ws = jnp.einsum("ab,jhts->jhatbs", eye, ws).reshape(-1, A_HEADS, CHUNK, CHUNK)
    bias = jnp.tile(jnp.swapaxes(b_s[..., :l], -1, -2), (1, reps, 1))
    return ws.astype(BF16), bias


def kernel(x_prompt, x_sample, state_b_wkv, state_b_shift, state_c_conv, p_prompt, p_sample, ln_g, ln_b, ffn_w_in, ffn_w_out, ple_w_gate, ple_w_proj, a_w_in, a_b_in, a_ln_g, a_ln_b, a_w_s, a_b_s, a_w_out, b_mu, b_w_rkv, b_w0, b_w1, b_w2, b_a0, b_a1, b_a2, b_g1, b_g2, b_k_k, b_k_a, b_r_k, b_lnx_g, b_lnx_b, b_w_o, c_w_in, c_conv_w, c_w_out):
    bf = lambda w: w.astype(BF16)
    row = lambda w: w.reshape(w.shape[0], 1, -1)
    head_of_lane = jnp.arange(D_MODEL) // HEAD_SIZE
    e = (head_of_lane[:, None] == jnp.arange(LANES)[None, :]).astype(BF16)
    W = dict(
        ln_g=ln_g[:, :, None, :], ln_b=ln_b[:, :, None, :],
        ffn_w_in=ffn_w_in, ffn_w_out=ffn_w_out,
        ple_w_gate=bf(ple_w_gate), ple_w_proj=bf(ple_w_proj),
        a_w_in=bf(a_w_in), a_b_in=row(a_b_in), a_ln_g=row(a_ln_g), a_ln_b=row(a_ln_b), a_w_out=bf(a_w_out),
        a_mix_long=_gmlp_mix_mats(a_w_s, a_b_s, x_prompt.shape[1]),
        a_mix_short=_gmlp_mix_mats(a_w_s, a_b_s, x_sample.shape[1]),
        c_w_in=bf(c_w_in), c_conv_w=c_conv_w, c_w_out=bf(c_w_out),
        b_proj=[b_mu, bf(b_w_rkv), row(b_w0), bf(b_w1), bf(b_w2), row(b_a0), bf(b_a1), bf(b_a2),
                bf(b_g1), bf(b_g2), row(b_k_k), row(b_k_a), row(b_r_k)],
        b_post=[row(b_lnx_g), row(b_lnx_b), bf(b_w_o)],
        e=e, et=e.T,
    )
    bp = x_prompt.shape[0]
    n_b = b_mu.shape[0]
    n_c = c_w_in.shape[0]
    zero_wkv = jnp.zeros((n_b, bp) + state_b_wkv.shape[2:], state_b_wkv.dtype)
    zero_shift = jnp.zeros((n_b, bp, D_MODEL), state_b_shift.dtype)
    zero_conv = jnp.zeros((n_c, bp, CONV_W - 1, D_MODEL), state_c_conv.dtype)
    (y_p, wkv_p, shift_p, conv_p), (y_s, wkv_s, shift_s, conv_s), a_v_s = _run_trunks(
        x_prompt, x_sample, p_prompt, p_sample, (zero_wkv, zero_shift, zero_conv),
        (state_b_wkv, state_b_shift, state_c_conv), W)
    return (y_p, y_s, a_v_s, wkv_p, shift_p, conv_p, wkv_s, shift_s, conv_s)
```

```python
import collections
import functools
import math

import jax
import jax.numpy as jnp
from jax import lax
from jax.experimental import pallas as pl
from jax.experimental.pallas import tpu as pltpu

F32 = jnp.float32
BF16 = jnp.bfloat16

D_MODEL = 1024
DEPTH = 4
N_MIXERS = 3
CHUNK = 128
A_INNER = 2 * D_MODEL
A_HEADS = 8
A_GROUP = A_INNER // A_HEADS
HEAD_SIZE = 64
B_HEADS = D_MODEL // HEAD_SIZE
CONV_W = 3
D_FF = 2816
PLE_DIM = 256
ALPHA = (2 * DEPTH) ** 0.25
LN_EPS = 1e-5
GN_EPS = 64e-5

LANES = 128
MXU_DIM = 256
ROW_TILE = 512
SCAN_CHUNK = 64
HEAD_PAIRS = B_HEADS // 2
VMEM_LIMIT = 56 * 1024 * 1024

NN = (((1,), (0,)), ((), ()))
NT = (((1,), (1,)), ((), ()))
TN = (((0,), (0,)), ((), ()))


def _dot(a, b, dims=NN):
    return lax.dot_general(a, b, dims, preferred_element_type=F32)


def _split2(x):
    hi = x.astype(BF16)
    lo = (x - hi.astype(F32)).astype(BF16)
    return hi, lo


def _split3(x):
    hi = x.astype(BF16)
    r1 = x - hi.astype(F32)
    mid = r1.astype(BF16)
    lo = (r1 - mid.astype(F32)).astype(BF16)
    return hi, mid, lo


def _layer_norm(x, g, b, eps=LN_EPS):
    mu = jnp.mean(x, axis=-1, keepdims=True)
    xc = x - mu
    var = jnp.mean(xc * xc, axis=-1, keepdims=True)
    return xc * lax.rsqrt(var + eps) * g + b


def _sigmoid(x):
    return 1.0 / (1.0 + jnp.exp(-x))


class _Sel(collections.namedtuple("_Sel", ["arr", "idx"])):
    def const_spec(self):
        k = len(self.idx)
        rest = self.arr.shape[k:]
        idx = self.idx
        return pl.BlockSpec((None,) * k + rest, lambda *_: idx + (0,) * len(rest),
                            pipeline_mode=pl.Buffered(1))

    def row_spec(self, tm):
        k = len(self.idx)
        idx = self.idx
        return pl.BlockSpec((None,) * k + (tm, self.arr.shape[-1]), lambda i: idx + (i, 0))


def _sel(arr, *idx):
    return _Sel(arr, tuple(idx))


def _row_spec(width, tm=ROW_TILE):
    return pl.BlockSpec((tm, width), lambda i: (i, 0))


def _params(sem):
    return pltpu.CompilerParams(dimension_semantics=sem, vmem_limit_bytes=VMEM_LIMIT)


FF_BLOCKS = ((0, 6 * MXU_DIM), (6 * MXU_DIM, D_FF))
assert D_FF % MXU_DIM == 0


WO_CAST_ROWS = 128


def _ffn_kernel(with_ple, with_cast, x_ref, wi_ref, wo_ref, g_ref, b_ref, *rest):
    rest = list(rest)
    if with_ple:
        p_ref, wg_ref, wp_ref = rest[:3]
        del rest[:3]
    if with_cast:
        next_wi_ref, next_wo_ref = rest[:2]
        del rest[:2]
    o_ref = rest[0]
    x = x_ref[...]
    xb = x.astype(BF16)
    acc = None
    for lo, hi in FF_BLOCKS:
        gate = _dot(xb, wi_ref[:, lo:hi])
        up = _dot(xb, wi_ref[:, D_FF + lo:D_FF + hi])
        act = (gate * _sigmoid(gate) * up).astype(BF16)
        part = _dot(act, wo_ref[lo:hi, :])
        acc = part if acc is None else acc + part
    y = _layer_norm(ALPHA * x + 0.5 * acc, g_ref[...], b_ref[...])
    if with_ple:
        gate = _sigmoid(_dot(y.astype(BF16), wg_ref[...]))
        y = y + gate * _dot(p_ref[...].astype(BF16), wp_ref[...])
    o_ref[...] = y
    if with_cast:
        cast_wi_ref, cast_wo_ref = rest[1:]
        cast_wi_ref[...] = next_wi_ref[...].astype(BF16)
        cast_wo_ref[...] = next_wo_ref[...].astype(BF16)


FF_STREAM_STEPS = 2


def _ffn_stream_kernel(with_ple, x_ref, wg_ref, wu_ref, wo_ref, g_ref, b_ref, *rest):
    if with_ple:
        p_ref, wgate_ref, wproj_ref, o_ref, acc_ref = rest
    else:
        o_ref, acc_ref = rest
    k = pl.program_id(0)
    xb = x_ref[...].astype(BF16)
    gate = _dot(xb, wg_ref[...])
    up = _dot(xb, wu_ref[...])
    part = _dot((gate * _sigmoid(gate) * up).astype(BF16), wo_ref[...])

    @pl.when(k == 0)
    def _():
        acc_ref[...] = part

    @pl.when(k > 0)
    def _():
        acc_ref[...] += part

    @pl.when(k == pl.num_programs(0) - 1)
    def _():
        y = _layer_norm(ALPHA * x_ref[...] + 0.5 * acc_ref[...], g_ref[...], b_ref[...])
        if with_ple:
            gate_p = _sigmoid(_dot(y.astype(BF16), wgate_ref[...]))
            y = y + gate_p * _dot(p_ref[...].astype(BF16), wproj_ref[...])
        o_ref[...] = y


def _ffn_stream(x, wi, wo, g, b, ple=None):
    n = x.shape[0]
    slab = D_FF // FF_STREAM_STEPS
    assert D_FF % FF_STREAM_STEPS == 0 and slab % LANES == 0
    whole = lambda width: pl.BlockSpec((n, width), lambda k: (0, 0))
    lead_i, lead_o = (None,) * len(wi.idx), (None,) * len(wo.idx)
    wi_idx, wo_idx = wi.idx, wo.idx
    args = [x, wi.arr, wi.arr, wo.arr, g.arr, b.arr]
    specs = [whole(D_MODEL),
             pl.BlockSpec(lead_i + (D_MODEL, slab), lambda k: wi_idx + (0, k)),
             pl.BlockSpec(lead_i + (D_MODEL, slab), lambda k: wi_idx + (0, k + FF_STREAM_STEPS)),
             pl.BlockSpec(lead_o + (slab, D_MODEL), lambda k: wo_idx + (k, 0)),
             g.const_spec(), b.const_spec()]
    if ple is not None:
        p, wg, wp = ple
        lead_p, p_idx = (None,) * len(p.idx), p.idx
        args += [p.arr, wg.arr, wp.arr]
        specs += [pl.BlockSpec(lead_p + (n, PLE_DIM), lambda k: p_idx + (0, 0)), wg.const_spec(), wp.const_spec()]
    return pl.pallas_call(
        functools.partial(_ffn_stream_kernel, ple is not None),
        grid=(FF_STREAM_STEPS,),
        in_specs=specs,
        out_specs=whole(D_MODEL),
        out_shape=jax.ShapeDtypeStruct((n, D_MODEL), F32),
        scratch_shapes=[pltpu.VMEM((n, D_MODEL), F32)],
        compiler_params=_params(("arbitrary",)),
        name="ffn_ple_stream" if ple is not None else "ffn_stream",
    )(*args)


def _ffn(x, wi, wo, g, b, ple=None, cast_next=None):
    n = x.shape[0]
    tm = min(ROW_TILE, n)
    steps = n // tm
    consts = [wi, wo, g, b]
    args = [x] + [c.arr for c in consts]
    specs = [_row_spec(D_MODEL, tm)] + [c.const_spec() for c in consts]
    out_specs = [_row_spec(D_MODEL, tm)]
    out_shape = [jax.ShapeDtypeStruct((n, D_MODEL), F32)]
    if ple is not None:
        p, wg, wp = ple
        args += [p.arr, wg.arr, wp.arr]
        specs += [p.row_spec(tm), wg.const_spec(), wp.const_spec()]
    if cast_next is not None:
        nwi, nwo = cast_next
        wi_rows = D_MODEL // steps
        wo_steps = D_FF // WO_CAST_ROWS
        assert D_MODEL % steps == 0 and wi_rows % 16 == 0 and D_FF % WO_CAST_ROWS == 0 and wo_steps <= steps
        lead = (None,) * len(nwi.idx)
        wi_idx, wo_idx = nwi.idx, nwo.idx
        wo_step = lambda t: jnp.minimum(t, wo_steps - 1)
        args += [nwi.arr, nwo.arr]
        specs += [pl.BlockSpec(lead + (wi_rows, 2 * D_FF), lambda t: wi_idx + (t, 0)),
                  pl.BlockSpec(lead + (WO_CAST_ROWS, D_MODEL), lambda t: wo_idx + (wo_step(t), 0))]
        out_specs += [pl.BlockSpec((wi_rows, 2 * D_FF), lambda t: (t, 0)),
                      pl.BlockSpec((WO_CAST_ROWS, D_MODEL), lambda t: (wo_step(t), 0))]
        out_shape += [jax.ShapeDtypeStruct((D_MODEL, 2 * D_FF), BF16), jax.ShapeDtypeStruct((D_FF, D_MODEL), BF16)]
    res = pl.pallas_call(
        functools.partial(_ffn_kernel, ple is not None, cast_next is not None),
        grid=(steps,),
        in_specs=specs,
        out_specs=out_specs,
        out_shape=out_shape,
        compiler_params=_params(("arbitrary",)),
        name="ffn_ple" if ple is not None else "ffn",
    )(*args)
    return res if cast_next is not None else res[0]


def _gmlp_kernel(keep_v, x_ref, wi_ref, bi_ref, lg_ref, lb_ref, ws_ref, bs_ref, wo_ref,
                 g_ref, b_ref, *rest):
    if keep_v:
        o_ref, v_ref, y_ref = rest
    else:
        o_ref, y_ref = rest
    tm = x_ref.shape[0]
    hm = tm // 2
    halves = [slice(h * hm, (h + 1) * hm) for h in range(2)]
    xs = [x_ref[rows, :] for rows in halves]
    zs = [_dot(x.astype(BF16), wi_ref[...]) + bi_ref[...] for x in xs]
    us, vbs = [], []
    for rows, z in zip(halves, zs):
        z = 0.5 * z * (1.0 + lax.erf(z * (1.0 / math.sqrt(2.0))))
        v = _layer_norm(z[:, A_INNER:], lg_ref[...], lb_ref[...])
        if keep_v:
            v_ref[rows, :] = v
        us.append(z[:, :A_INNER])
        vbs.append(v.astype(BF16))
    for half, u, vb in zip(halves, us, vbs):
        for c in range(hm // CHUNK):
            rows = slice(c * CHUNK, (c + 1) * CHUNK)
            out_rows = slice(half.start + c * CHUNK, half.start + (c + 1) * CHUNK)
            for h in range(A_HEADS):
                cols = slice(h * A_GROUP, (h + 1) * A_GROUP)
                mixed = _dot(ws_ref[h], vb[rows, cols]) + bs_ref[:, h:h + 1]
                y_ref[out_rows, cols] = (u[rows, cols] * mixed).astype(BF16)
    for rows, x in zip(halves, xs):
        out = _dot(y_ref[rows, :], wo_ref[...])
        o_ref[rows, :] = _layer_norm(ALPHA * x + out, g_ref[...], b_ref[...])


def _gmlp(x, consts, keep_v):
    n = x.shape[0]
    tm = min(ROW_TILE, n)
    out_shape = [jax.ShapeDtypeStruct((n, D_MODEL), F32)]
    out_specs = [_row_spec(D_MODEL, tm)]
    if keep_v:
        out_shape.append(jax.ShapeDtypeStruct((n, A_INNER), F32))
        out_specs.append(_row_spec(A_INNER, tm))
    res = pl.pallas_call(
        functools.partial(_gmlp_kernel, keep_v),
        grid=(n // tm,),
        in_specs=[_row_spec(D_MODEL, tm)] + [c.const_spec() for c in consts],
        out_specs=out_specs,
        out_shape=out_shape,
        scratch_shapes=[pltpu.VMEM((tm, A_INNER), BF16)],
        compiler_params=_params(("parallel",)),
        name="gmlp",
    )(x, *[c.arr for c in consts])
    return (res[0], res[1]) if keep_v else (res[0], None)


def _conv_tail(x, bg, conv, wo_ref, g_ref, b_ref, o_ref):
    out = _dot((bg * conv).astype(BF16), wo_ref[...])
    o_ref[...] = _layer_norm(ALPHA * x + out, g_ref[...], b_ref[...])


def _conv_long_kernel(tiles_per_seq, x_ref, buf_ref, wi_ref, cw_ref, wo_ref, g_ref, b_ref,
                      o_ref, tail_ref, carry_ref):
    x = x_ref[...]
    tm = x.shape[0]
    h3 = _dot(x.astype(BF16), wi_ref[...])
    bg = h3[:, :D_MODEL]
    z = h3[:, D_MODEL:2 * D_MODEL] * h3[:, 2 * D_MODEL:]

    @pl.when(pl.program_id(0) % tiles_per_seq == 0)
    def _():
        carry_ref[...] = buf_ref[0]

    row = lax.broadcasted_iota(jnp.int32, (tm, D_MODEL), 0)
    prev1 = carry_ref[7:8, :]
    prev2 = carry_ref[6:7, :]
    z1 = jnp.where(row == 0, prev1, pltpu.roll(z, 1, 0))
    z2 = jnp.where(row == 0, prev2, jnp.where(row == 1, prev1, pltpu.roll(z, 2, 0)))
    conv = cw_ref[0:1, :] * z2 + cw_ref[1:2, :] * z1 + cw_ref[2:3, :] * z
    tail = z[tm - 8:, :]
    carry_ref[...] = tail
    tail_ref[0] = tail
    _conv_tail(x, bg, conv, wo_ref, g_ref, b_ref, o_ref)


def _conv_short_kernel(seq, x_ref, h1_ref, h2_ref, wi_ref, cw_ref, wo_ref, g_ref, b_ref,
                       o_ref, z_ref):
    x = x_ref[...]
    tm = x.shape[0]
    h3 = _dot(x.astype(BF16), wi_ref[...])
    bg = h3[:, :D_MODEL]
    z = h3[:, D_MODEL:2 * D_MODEL] * h3[:, 2 * D_MODEL:]
    t = lax.broadcasted_iota(jnp.int32, (tm, D_MODEL), 0) % seq
    z1 = jnp.where(t >= 1, pltpu.roll(z, 1, 0), h1_ref[...])
    z2 = jnp.where(t >= 2, pltpu.roll(z, 2, 0), h2_ref[...])
    conv = cw_ref[0:1, :] * z2 + cw_ref[1:2, :] * z1 + cw_ref[2:3, :] * z
    z_ref[...] = z
    _conv_tail(x, bg, conv, wo_ref, g_ref, b_ref, o_ref)


def _first_rows(state_rows, seq):
    bn = state_rows.shape[0]
    out = jnp.zeros((bn, seq, D_MODEL), F32).at[:, 0, :].set(state_rows)
    return out.reshape(bn * seq, D_MODEL)


def _conv_mixer(x, buf_prev, seq, consts):
    n = x.shape[0]
    bn = n // seq
    tm = min(ROW_TILE, n)
    const_specs = [c.const_spec() for c in consts]
    const_args = [c.arr for c in consts]
    if seq >= tm:
        tiles_per_seq = seq // tm
        buf8 = jnp.concatenate([jnp.zeros((bn, 6, D_MODEL), F32), buf_prev], axis=1)
        out, tails = pl.pallas_call(
            functools.partial(_conv_long_kernel, tiles_per_seq),
            grid=(n // tm,),
            in_specs=[_row_spec(D_MODEL, tm),
                      pl.BlockSpec((1, 8, D_MODEL), lambda i: (i // tiles_per_seq, 0, 0))] + const_specs,
            out_specs=[_row_spec(D_MODEL, tm), pl.BlockSpec((1, 8, D_MODEL), lambda i: (i, 0, 0))],
            out_shape=[jax.ShapeDtypeStruct((n, D_MODEL), F32),
                       jax.ShapeDtypeStruct((n // tm, 8, D_MODEL), F32)],
            scratch_shapes=[pltpu.VMEM((8, D_MODEL), F32)],
            compiler_params=_params(("arbitrary",)),
            name="conv_long",
        )(x, buf8, *const_args)
        new_buf = tails[tiles_per_seq - 1::tiles_per_seq, 6:8, :]
        return out, new_buf
    assert tm % seq == 0 and seq >= CONV_W - 1
    h1 = _first_rows(buf_prev[:, 1, :], seq)
    h2 = _first_rows(buf_prev[:, 0, :], seq) + jnp.roll(h1, 1, axis=0)
    out, z = pl.pallas_call(
        functools.partial(_conv_short_kernel, seq),
        grid=(n // tm,),
        in_specs=[_row_spec(D_MODEL, tm)] * 3 + const_specs,
        out_specs=[_row_spec(D_MODEL, tm)] * 2,
        out_shape=[jax.ShapeDtypeStruct((n, D_MODEL), F32)] * 2,
        compiler_params=_params(("parallel",)),
        name="conv_short",
    )(x, h1, h2, *const_args)
    new_buf = z.reshape(bn, seq, D_MODEL)[:, seq - (CONV_W - 1):, :]
    return out, new_buf


def _head_sum_bcast(x, e_ref, et_ref):
    s = _dot(x.astype(BF16), e_ref[...])
    s_hi, s_lo = _split2(s)
    return _dot(s_hi, et_ref[...]) + _dot(s_lo, et_ref[...])


def _rwkv_proj_body(x, xprev, mu_ref, wrkv_ref, w0_ref, w1_ref, w2_ref, a0_ref, a1_ref,
                    a2_ref, g1_ref, g2_ref, kk_ref, ka_ref, rk_ref, e_ref, et_ref,
                    r_out, lw_out, k_out, v_out, c_out, b_out, bonus_out, g_out):
    xx = xprev - x
    mix = lambda i: (x + xx * mu_ref[i:i + 1, :]).astype(BF16)
    r = _dot(mix(0), wrkv_ref[0])
    k = _dot(mix(2), wrkv_ref[1])
    v = _dot(mix(3), wrkv_ref[2])
    zw = w0_ref[...] + _dot(jnp.tanh(_dot(mix(1), w1_ref[...])).astype(BF16), w2_ref[...])
    lw_out[...] = -_sigmoid(zw) * math.exp(-0.5)
    a = _sigmoid(a0_ref[...] + _dot(_dot(mix(4), a1_ref[...]).astype(BF16), a2_ref[...]))
    g_out[...] = _dot(_sigmoid(_dot(mix(5), g1_ref[...])).astype(BF16), g2_ref[...])
    kk = k * kk_ref[...]
    norm = jnp.sqrt(_head_sum_bcast(kk * kk, e_ref, et_ref))
    c = kk / jnp.maximum(norm, 1e-12)
    kmod = k * (1.0 + (a - 1.0) * ka_ref[...])
    r_out[...] = r
    k_out[...] = kmod
    v_out[...] = v
    c_out[...] = c
    b_out[...] = c * a
    bonus_out[...] = _head_sum_bcast(r * kmod * rk_ref[...], e_ref, et_ref) * v


def _rwkv_proj_long_kernel(tiles_per_seq, x_ref, shift_ref, *rest):
    carry_ref = rest[-1]
    x = x_ref[...]
    tm = x.shape[0]

    @pl.when(pl.program_id(0) % tiles_per_seq == 0)
    def _():
        carry_ref[...] = shift_ref[...]

    row = lax.broadcasted_iota(jnp.int32, (tm, D_MODEL), 0)
    xprev = jnp.where(row == 0, carry_ref[...], pltpu.roll(x, 1, 0))
    carry_ref[...] = x[tm - 1:tm, :]
    _rwkv_proj_body(x, xprev, *rest[:-1])


def _rwkv_proj_short_kernel(seq, x_ref, h1_ref, *rest):
    x = x_ref[...]
    t = lax.broadcasted_iota(jnp.int32, x.shape, 0) % seq
    xprev = jnp.where(t >= 1, pltpu.roll(x, 1, 0), h1_ref[...])
    _rwkv_proj_body(x, xprev, *rest)


def _rwkv_proj(x, shift_prev, seq, consts):
    n = x.shape[0]
    tm = min(ROW_TILE // 2, n)
    common = dict(
        grid=(n // tm,),
        out_specs=[_row_spec(D_MODEL, tm)] * 8,
        out_shape=[jax.ShapeDtypeStruct((n, D_MODEL), F32)] * 8,
    )
    const_specs = [c.const_spec() for c in consts]
    const_args = [c.arr for c in consts]
    if seq >= tm:
        tiles_per_seq = seq // tm
        return pl.pallas_call(
            functools.partial(_rwkv_proj_long_kernel, tiles_per_seq),
            in_specs=[_row_spec(D_MODEL, tm),
                      pl.BlockSpec((None, 1, D_MODEL), lambda i: (i // tiles_per_seq, 0, 0))] + const_specs,
            scratch_shapes=[pltpu.VMEM((1, D_MODEL), F32)],
            compiler_params=_params(("arbitrary",)),
            name="rwkv_proj_long", **common,
        )(x, shift_prev[:, None, :], *const_args)
    assert tm % seq == 0
    return pl.pallas_call(
        functools.partial(_rwkv_proj_short_kernel, seq),
        in_specs=[_row_spec(D_MODEL, tm)] * 2 + const_specs,
        compiler_params=_params(("parallel",)),
        name="rwkv_proj_short", **common,
    )(x, _first_rows(shift_prev, seq), *const_args)


def _pair_rows(x, lane_lo):
    zero = jnp.zeros_like(x)
    return jnp.concatenate([jnp.where(lane_lo, x, zero), jnp.where(lane_lo, zero, x)], axis=0)


def _block_diag(a, b):
    zero = jnp.zeros_like(a)
    return jnp.concatenate([jnp.concatenate([a, zero], axis=1), jnp.concatenate([zero, b], axis=1)], axis=0)


def _scan_chunk(load, store_y, n_chains, sub, qs_fn, state_fn):
    C = SCAN_CHUNK
    pairs = range(n_chains)
    shift = int(math.log2(sub))
    row = lax.broadcasted_iota(jnp.int32, (C, 2 * C), 0)
    col = lax.broadcasted_iota(jnp.int32, (C, 2 * C), 1) & (C - 1)
    same = (row >> shift) == (col >> shift)
    strict = same & (row > col)
    incl = same & (row >= col)
    lane_lo = lax.broadcasted_iota(jnp.int32, (C, LANES), 1) < HEAD_SIZE
    r2 = lax.broadcasted_iota(jnp.int32, (C, C), 0)
    c2 = lax.broadcasted_iota(jnp.int32, (C, C), 1)
    tri = jnp.where(((r2 >> shift) == (c2 >> shift)) & (r2 >= c2), 1.0, 0.0).astype(BF16)
    r, lw, k, v, c, b = zip(*[load(p) for p in pairs])

    cum = []
    for p in pairs:
        l_hi, l_mid, l_lo = _split3(lw[p])
        cum.append(_dot(tri, l_hi) + (_dot(tri, l_mid) + _dot(tri, l_lo)))
    p_inc = [jnp.exp(cum[p]) for p in pairs]
    p_inv = [jnp.exp(-cum[p]) for p in pairs]
    p_exc = [jnp.exp(cum[p] - lw[p]) for p in pairs]
    q = [jnp.concatenate([c[p] * p_exc[p], r[p] * p_inc[p]], axis=0).astype(BF16) for p in pairs]
    bt = [(b[p] * p_inv[p]).astype(BF16) for p in pairs]
    kt = [(k[p] * p_inv[p]).astype(BF16) for p in pairs]
    vb = [v[p].astype(BF16) for p in pairs]
    kb = [jnp.concatenate([_pair_rows(bt[p], lane_lo), _pair_rows(kt[p], lane_lo)], axis=0) for p in pairs]
    gram = [_dot(q[p], kb[p], NT) for p in pairs]
    l_cb = [jnp.where(strict, gram[p][:C, :2 * C], 0.0) for p in pairs]
    l_ck = [jnp.where(strict, gram[p][:C, 2 * C:], 0.0).astype(BF16) for p in pairs]
    a_rb = [jnp.where(incl, gram[p][C:, :2 * C], 0.0) for p in pairs]
    a_rk = [jnp.where(incl, gram[p][C:, 2 * C:], 0.0) for p in pairs]
    qs = qs_fn(q)
    v_rows = [_pair_rows(vb[p], lane_lo) for p in pairs]
    u = [-(qs[p][:C] + _dot(l_ck[p], v_rows[p])) for p in pairs]
    m = [l_cb[p].astype(BF16) for p in pairs]
    u = [u[p] - _dot(m[p], _pair_rows(u[p].astype(BF16), lane_lo)) for p in pairs]
    for _ in range(shift - 1):
        m = [_dot(m[p], _pair_rows(m[p], lane_lo)).astype(BF16) for p in pairs]
        u = [u[p] + _dot(m[p], _pair_rows(u[p].astype(BF16), lane_lo)) for p in pairs]
    ub = [u[p].astype(BF16) for p in pairs]
    for p in pairs:
        a = jnp.concatenate([a_rb[p], a_rk[p]], axis=1).astype(BF16)
        uv_rows = jnp.concatenate([_pair_rows(ub[p], lane_lo), v_rows[p]], axis=0)
        store_y(p, qs[p][C:] + _dot(a, uv_rows))
    state_fn(u, v, ub, vb, bt, kt, p_inc)


def _pair_lanes(p):
    return slice(p * LANES, (p + 1) * LANES)


def _diag_blocks_mask():
    sq_row = lax.broadcasted_iota(jnp.int32, (LANES, LANES), 0) < HEAD_SIZE
    sq_col = lax.broadcasted_iota(jnp.int32, (LANES, LANES), 1) < HEAD_SIZE
    return sq_row == sq_col


def _store_pair_state(out_ref, i, p, s_pair):
    out_ref[i, 2 * p] = s_pair[:HEAD_SIZE, :HEAD_SIZE]
    out_ref[i, 2 * p + 1] = s_pair[HEAD_SIZE:, HEAD_SIZE:]


SCAN_SEQS = 4


def _rwkv_scan_kernel(r_ref, lw_ref, k_ref, v_ref, c_ref, b_ref, s0_ref, y_ref, sfin_ref, s_ref):
    C = SCAN_CHUNK
    j = pl.program_id(1)
    chains = [(s, p) for s in range(SCAN_SEQS) for p in range(HEAD_PAIRS)]

    @pl.when(j == 0)
    def _():
        for s, p in chains:
            s_ref[s, p] = _block_diag(s0_ref[s, 2 * p], s0_ref[s, 2 * p + 1])

    diag = _diag_blocks_mask()
    refs = (r_ref, lw_ref, k_ref, v_ref, c_ref, b_ref)

    def load(ch):
        s, p = chains[ch]
        return tuple(ref[s, :, _pair_lanes(p)] for ref in refs)

    def store_y(ch, y):
        s, p = chains[ch]
        y_ref[s, :, _pair_lanes(p)] = y

    def qs_fn(q):
        return [_dot(q[ch], s_ref[s, p].astype(BF16), NT) for ch, (s, p) in enumerate(chains)]

    def state_fn(u, v, ub, vb, bt, kt, p_inc):
        for ch, (s, p) in enumerate(chains):
            upd = _dot(jnp.concatenate([ub[ch], vb[ch]], axis=0), jnp.concatenate([bt[ch], kt[ch]], axis=0), TN)
            s_ref[s, p] = (s_ref[s, p] + jnp.where(diag, upd, 0.0)) * p_inc[ch][C - 1:C, :]

    _scan_chunk(load, store_y, len(chains), C, qs_fn, state_fn)

    @pl.when(j == pl.num_programs(1) - 1)
    def _():
        for s, p in chains:
            _store_pair_state(sfin_ref, s, p, s_ref[s, p])


def _rwkv_scan(r, lw, k, v, c, b, s0, bn, seq):
    assert bn % SCAN_SEQS == 0
    row_spec = pl.BlockSpec((SCAN_SEQS, SCAN_CHUNK, D_MODEL), lambda i, j: (i, j, 0))
    st_spec = pl.BlockSpec((SCAN_SEQS, B_HEADS, HEAD_SIZE, HEAD_SIZE), lambda i, j: (i, 0, 0, 0))
    as_seqs = lambda z: z.reshape(bn, seq, D_MODEL)
    ys, s_new = pl.pallas_call(
        _rwkv_scan_kernel,
        grid=(bn // SCAN_SEQS, seq // SCAN_CHUNK),
        in_specs=[row_spec] * 6 + [st_spec],
        out_specs=[row_spec, st_spec],
        out_shape=[jax.ShapeDtypeStruct((bn, seq, D_MODEL), F32),
                   jax.ShapeDtypeStruct((bn, B_HEADS, HEAD_SIZE, HEAD_SIZE), F32)],
        scratch_shapes=[pltpu.VMEM((SCAN_SEQS, HEAD_PAIRS, LANES, LANES), F32)],
        compiler_params=_params(("parallel", "arbitrary")),
        name="rwkv_scan",
    )(*(as_seqs(z) for z in (r, lw, k, v, c, b)), s0)
    return ys.reshape(bn * seq, D_MODEL), s_new


def _rwkv_scan_short_kernel(sub, r_ref, lw_ref, k_ref, v_ref, c_ref, b_ref, s0_ref, y_ref, sfin_ref):
    C = SCAN_CHUNK
    nb = C // sub
    shift = int(math.log2(sub))
    diag = _diag_blocks_mask()
    row_seq = (lax.broadcasted_iota(jnp.int32, (2 * C, LANES), 0) & (C - 1)) >> shift
    lane_seq = (lax.broadcasted_iota(jnp.int32, (LANES, 2 * C), 1) & (C - 1)) >> shift
    s_prev = {}

    def qs_fn(q):
        out = []
        for p in range(HEAD_PAIRS):
            for i in range(nb):
                s_prev[i, p] = _block_diag(s0_ref[i, 2 * p], s0_ref[i, 2 * p + 1])
            s_stack = jnp.concatenate([s_prev[i, p].astype(BF16) for i in range(nb)], axis=1)
            zero = jnp.zeros_like(q[p])
            q_wide = jnp.concatenate([jnp.where(row_seq == i, q[p], zero) for i in range(nb)], axis=1)
            out.append(_dot(q_wide, s_stack, NT))
        return out

    def state_fn(u, v, ub, vb, bt, kt, p_inc):
        for p in range(HEAD_PAIRS):
            uvt = jnp.concatenate([u[p], v[p]], axis=0).T
            stack = jnp.concatenate([jnp.where(lane_seq == i, uvt, 0.0) for i in range(nb)], axis=0)
            upd = _dot(stack.astype(BF16), jnp.concatenate([bt[p], kt[p]], axis=0))
            for i in range(nb):
                last = i * sub + sub - 1
                s_new = ((s_prev[i, p] + jnp.where(diag, upd[i * LANES:(i + 1) * LANES], 0.0))
                         * p_inc[p][last:last + 1, :])
                _store_pair_state(sfin_ref, i, p, s_new)

    refs = (r_ref, lw_ref, k_ref, v_ref, c_ref, b_ref)

    def load(p):
        return tuple(ref[:, _pair_lanes(p)] for ref in refs)

    def store_y(p, y):
        y_ref[:, _pair_lanes(p)] = y

    _scan_chunk(load, store_y, HEAD_PAIRS, sub, qs_fn, state_fn)


def _rwkv_scan_short(r, lw, k, v, c, b, s0, bn, seq):
    nb = SCAN_CHUNK // seq
    row_spec = pl.BlockSpec((SCAN_CHUNK, D_MODEL), lambda i: (i, 0))
    st_spec = pl.BlockSpec((nb, B_HEADS, HEAD_SIZE, HEAD_SIZE), lambda i: (i, 0, 0, 0))
    return pl.pallas_call(
        functools.partial(_rwkv_scan_short_kernel, seq),
        grid=(bn // nb,),
        in_specs=[row_spec] * 6 + [st_spec],
        out_specs=[row_spec, st_spec],
        out_shape=[jax.ShapeDtypeStruct((bn * seq, D_MODEL), F32),
                   jax.ShapeDtypeStruct((bn, B_HEADS, HEAD_SIZE, HEAD_SIZE), F32)],
        compiler_params=_params(("parallel",)),
        name="rwkv_scan_short",
    )(r, lw, k, v, c, b, s0)


def _rwkv_post_kernel(x_ref, y_ref, bonus_ref, gate_ref, lg_ref, lb_ref, wo_ref, e_ref, et_ref,
                      g_ref, b_ref, o_ref):
    ys = y_ref[...]
    inv = 1.0 / HEAD_SIZE
    m = _head_sum_bcast(ys, e_ref, et_ref) * inv
    yc = ys - m
    var = _head_sum_bcast(yc * yc, e_ref, et_ref) * inv
    yn = yc * lax.rsqrt(var + GN_EPS) * lg_ref[...] + lb_ref[...]
    out = _dot(((yn + bonus_ref[...]) * gate_ref[...]).astype(BF16), wo_ref[...])
    o_ref[...] = _layer_norm(ALPHA * x_ref[...] + out, g_ref[...], b_ref[...])


def _rwkv_post(x, ys, bonus, gate, consts):
    n = x.shape[0]
    tm = min(ROW_TILE, n)
    return pl.pallas_call(
        _rwkv_post_kernel,
        grid=(n // tm,),
        in_specs=[_row_spec(D_MODEL, tm)] * 4 + [c.const_spec() for c in consts],
        out_specs=_row_spec(D_MODEL, tm),
        out_shape=jax.ShapeDtypeStruct((n, D_MODEL), F32),
        compiler_params=_params(("parallel",)),
        name="rwkv_post",
    )(x, ys, bonus, gate, *[c.arr for c in consts])


def _rwkv_mixer(x, shift_prev, s0, seq, w, g, b):
    n = x.shape[0]
    bn = n // seq
    r, lw, k, v, c, bb, bonus, gate = _rwkv_proj(x, shift_prev, seq, w["proj"])
    if seq % SCAN_CHUNK == 0:
        scan = _rwkv_scan
    else:
        assert SCAN_CHUNK % seq == 0 and seq & (seq - 1) == 0 and n % SCAN_CHUNK == 0
        scan = _rwkv_scan_short
    ys, s_new = scan(r, lw, k, v, c, bb, s0, bn, seq)
    out = _rwkv_post(x, ys, bonus, gate, w["post"] + [g, b])
    return out, x.reshape(bn, seq, D_MODEL)[:, -1, :], s_new


def _run_trunks(x_long, x_short, p_long, p_short, states_long, states_short, W):
    shapes = [x_long.shape, x_short.shape]
    seqs = [s[1] for s in shapes]
    assert seqs[0] >= CHUNK > seqs[1]
    xs = [x_long.reshape(-1, D_MODEL), x_short.reshape(-1, D_MODEL)]
    ps = [p_long.reshape(DEPTH, -1, PLE_DIM), p_short.reshape(DEPTH, -1, PLE_DIM)]
    states = [states_long, states_short]
    new_v, new_wkv, new_shift, new_conv = [], ([], []), ([], []), ([], [])
    ffn_w = (_sel(W["ffn_w_in"][0, 0].astype(BF16)), _sel(W["ffn_w_out"][0, 0].astype(BF16)))

    def ffn_pair(xs, ffn_w, i, s, ple=None):
        nxt = (i, s + 1) if s == 0 else (i + 1, 0)
        cast_next = None
        if nxt[0] < DEPTH:
            cast_next = (_sel(W["ffn_w_in"], *nxt), _sel(W["ffn_w_out"], *nxt))
        norm = (_sel(W["ln_g"], i, 2 * s), _sel(W["ln_b"], i, 2 * s))
        ple_of = lambda t: None if ple is None else (_sel(ps[t], i),) + ple
        res = _ffn(xs[0], *ffn_w, *norm, ple=ple_of(0), cast_next=cast_next)
        ffn_short = _ffn_stream if xs[1].shape[0] <= ROW_TILE else _ffn
        x_short = ffn_short(xs[1], *ffn_w, *norm, ple=ple_of(1))
        if cast_next is None:
            return [res, x_short], None
        return [res[0], x_short], (_sel(res[1]), _sel(res[2]))

    for i in range(DEPTH):
        j, kind = divmod(i, N_MIXERS)
        ln_g = lambda s: _sel(W["ln_g"], i, s)
        ln_b = lambda s: _sel(W["ln_b"], i, s)
        xs, ffn_w = ffn_pair(xs, ffn_w, i, 0)
        if kind == 0:
            for t in range(2):
                mix = W["a_mix_long"] if seqs[t] >= CHUNK else W["a_mix_short"]
                consts = [_sel(W[name], j) for name in ("a_w_in", "a_b_in", "a_ln_g", "a_ln_b")]
                consts += [_sel(mix[0], j), _sel(mix[1], j), _sel(W["a_w_out"], j), ln_g(1), ln_b(1)]
                xs[t], v = _gmlp(xs[t], consts, keep_v=(t == 1))
            new_v.append(v.reshape(shapes[1][0], seqs[1], A_INNER))
        elif kind == 1:
            w = dict(proj=[_sel(a, j) for a in W["b_proj"]] + [_sel(W["e"]), _sel(W["et"])],
                     post=[_sel(a, j) for a in W["b_post"]] + [_sel(W["e"]), _sel(W["et"])])
            for t in range(2):
                wkv, shift, _ = states[t]
                xs[t], sh, s = _rwkv_mixer(xs[t], shift[j], wkv[j], seqs[t], w, ln_g(1), ln_b(1))
                new_shift[t].append(sh)
                new_wkv[t].append(s)
        else:
            consts = [_sel(W[name], j) for name in ("c_w_in", "c_conv_w", "c_w_out")] + [ln_g(1), ln_b(1)]
            for t in range(2):
                xs[t], buf = _conv_mixer(xs[t], states[t][2][j], seqs[t], consts)
                new_conv[t].append(buf)
        xs, ffn_w = ffn_pair(xs, ffn_w, i, 1, ple=(_sel(W["ple_w_gate"], i), _sel(W["ple_w_proj"], i)))
    outs = [(xs[t].reshape(shapes[t]), jnp.stack(new_wkv[t]), jnp.stack(new_shift[t]), jnp.stack(new_conv[t]))
            for t in range(2)]
    return outs[0], outs[1], jnp.stack(new_v)


def _gmlp_mix_mats(w_s, b_s, seq):
    l = min(seq, CHUNK)
    ws = jnp.where(jnp.tril(jnp.ones((l, l), dtype=bool)), w_s[..., :l, :l], 0.0)
    reps = CHUNK // l
    if reps > 1:
        pos = jnp.arange(CHUNK) // l
        ws = jnp.where(pos[:, None] == pos[None, :], jnp.tile(ws, (1, 1, reps, reps)), 0.0)
    bias = jnp.tile(jnp.swapaxes(b_s[..., :l], -1, -2), (1, reps, 1))
    return ws.astype(BF16), bias


def kernel(x_prompt, x_sample, state_b_wkv, state_b_shift, state_c_conv, p_prompt, p_sample, ln_g, ln_b, ffn_w_in, ffn_w_out, ple_w_gate, ple_w_proj, a_w_in, a_b_in, a_ln_g, a_ln_b, a_w_s, a_b_s, a_w_out, b_mu, b_w_rkv, b_w0, b_w1, b_w2, b_a0, b_a1, b_a2, b_g1, b_g2, b_k_k, b_k_a, b_r_k, b_lnx_g, b_lnx_b, b_w_o, c_w_in, c_conv_w, c_w_out):
    bf = lambda w: w.astype(BF16)
    row = lambda w: w.reshape(w.shape[0], 1, -1)
    head_of_lane = jnp.arange(D_MODEL) // HEAD_SIZE
    e = (head_of_lane[:, None] == jnp.arange(LANES)[None, :]).astype(BF16)
    W = dict(
        ln_g=ln_g[:, :, None, :], ln_b=ln_b[:, :, None, :],
        ffn_w_in=ffn_w_in, ffn_w_out=ffn_w_out,
        ple_w_gate=bf(ple_w_gate), ple_w_proj=bf(ple_w_proj),
        a_w_in=bf(a_w_in), a_b_in=row(a_b_in), a_ln_g=row(a_ln_g), a_ln_b=row(a_ln_b), a_w_out=bf(a_w_out),
        a_mix_long=_gmlp_mix_mats(a_w_s, a_b_s, x_prompt.shape[1]),
        a_mix_short=_gmlp_mix_mats(a_w_s, a_b_s, x_sample.shape[1]),
        c_w_in=bf(c_w_in), c_conv_w=c_conv_w, c_w_out=bf(c_w_out),
        b_proj=[b_mu, bf(b_w_rkv), row(b_w0), bf(b_w1), bf(b_w2), row(b_a0), bf(b_a1), bf(b_a2),
                bf(b_g1), bf(b_g2), row(b_k_k), row(b_k_a), row(b_r_k)],
        b_post=[row(b_lnx_g), row(b_lnx_b), bf(b_w_o)],
        e=e, et=e.T,
    )
    bp = x_prompt.shape[0]
    n_b = b_mu.shape[0]
    n_c = c_w_in.shape[0]
    zero_wkv = jnp.zeros((n_b, bp) + state_b_wkv.shape[2:], state_b_wkv.dtype)
    zero_shift = jnp.zeros((n_b, bp, D_MODEL), state_b_shift.dtype)
    zero_conv = jnp.zeros((n_c, bp, CONV_W - 1, D_MODEL), state_c_conv.dtype)
    (y_p, wkv_p, shift_p, conv_p), (y_s, wkv_s, shift_s, conv_s), a_v_s = _run_trunks(
        x_prompt, x_sample, p_prompt, p_sample, (zero_wkv, zero_shift, zero_conv),
        (state_b_wkv, state_b_shift, state_c_conv), W)
    return (y_p, y_s, a_v_s, wkv_p, shift_p, conv_p, wkv_s, shift_s, conv_s)
```

```python
import collections
import functools
import math

import jax
import jax.numpy as jnp
from jax import lax
from jax.experimental import pallas as pl
from jax.experimental.pallas import tpu as pltpu

F32 = jnp.float32
BF16 = jnp.bfloat16

D_MODEL = 1024
DEPTH = 4
N_MIXERS = 3
CHUNK = 128
A_INNER = 2 * D_MODEL
A_HEADS = 8
A_GROUP = A_INNER // A_HEADS
HEAD_SIZE = 64
B_HEADS = D_MODEL // HEAD_SIZE
CONV_W = 3
D_FF = 2816
PLE_DIM = 256
ALPHA = (2 * DEPTH) ** 0.25
LN_EPS = 1e-5
GN_EPS = 64e-5

LANES = 128
MXU_DIM = 256
ROW_TILE = 512
SCAN_CHUNK = 64
HEAD_PAIRS = B_HEADS // 2
VMEM_LIMIT = 56 * 1024 * 1024

NN = (((1,), (0,)), ((), ()))
NT = (((1,), (1,)), ((), ()))
TN = (((0,), (0,)), ((), ()))


def _dot(a, b, dims=NN):
    return lax.dot_general(a, b, dims, preferred_element_type=F32)


def _split2(x):
    hi = x.astype(BF16)
    lo = (x - hi.astype(F32)).astype(BF16)
    return hi, lo


def _split3(x):
    hi = x.astype(BF16)
    r1 = x - hi.astype(F32)
    mid = r1.astype(BF16)
    lo = (r1 - mid.astype(F32)).astype(BF16)
    return hi, mid, lo


def _layer_norm(x, g, b, eps=LN_EPS):
    mu = jnp.mean(x, axis=-1, keepdims=True)
    xc = x - mu
    var = jnp.mean(xc * xc, axis=-1, keepdims=True)
    return xc * lax.rsqrt(var + eps) * g + b


def _sigmoid(x):
    return 1.0 / (1.0 + jnp.exp(-x))


class _Sel(collections.namedtuple("_Sel", ["arr", "idx"])):
    def const_spec(self):
        k = len(self.idx)
        rest = self.arr.shape[k:]
        idx = self.idx
        return pl.BlockSpec((None,) * k + rest, lambda *_: idx + (0,) * len(rest),
                            pipeline_mode=pl.Buffered(1))

    def row_spec(self, tm):
        k = len(self.idx)
        idx = self.idx
        return pl.BlockSpec((None,) * k + (tm, self.arr.shape[-1]), lambda i: idx + (i, 0))


def _sel(arr, *idx):
    return _Sel(arr, tuple(idx))


def _row_spec(width, tm=ROW_TILE):
    return pl.BlockSpec((tm, width), lambda i: (i, 0))


def _params(sem):
    return pltpu.CompilerParams(dimension_semantics=sem, vmem_limit_bytes=VMEM_LIMIT)


FF_BLOCKS = ((0, 6 * MXU_DIM), (6 * MXU_DIM, D_FF))
assert D_FF % MXU_DIM == 0


WO_CAST_ROWS = 128


def _ffn_kernel(with_ple, with_cast, x_ref, wi_ref, wo_ref, g_ref, b_ref, *rest):
    rest = list(rest)
    if with_ple:
        p_ref, wg_ref, wp_ref = rest[:3]
        del rest[:3]
    if with_cast:
        next_wi_ref, next_wo_ref = rest[:2]
        del rest[:2]
    o_ref = rest[0]
    x = x_ref[...]
    xb = x.astype(BF16)
    acc = None
    for lo, hi in FF_BLOCKS:
        gate = _dot(xb, wi_ref[:, lo:hi])
        up = _dot(xb, wi_ref[:, D_FF + lo:D_FF + hi])
        act = (gate * _sigmoid(gate) * up).astype(BF16)
        part = _dot(act, wo_ref[lo:hi, :])
        acc = part if acc is None else acc + part
    y = _layer_norm(ALPHA * x + 0.5 * acc, g_ref[...], b_ref[...])
    if with_ple:
        gate = _sigmoid(_dot(y.astype(BF16), wg_ref[...]))
        y = y + gate * _dot(p_ref[...].astype(BF16), wp_ref[...])
    o_ref[...] = y
    if with_cast:
        cast_wi_ref, cast_wo_ref = rest[1:]
        cast_wi_ref[...] = next_wi_ref[...].astype(BF16)
        cast_wo_ref[...] = next_wo_ref[...].astype(BF16)


def _ffn(x, wi, wo, g, b, ple=None, cast_next=None):
    n = x.shape[0]
    tm = min(ROW_TILE, n)
    steps = n // tm
    consts = [wi, wo, g, b]
    args = [x] + [c.arr for c in consts]
    specs = [_row_spec(D_MODEL, tm)] + [c.const_spec() for c in consts]
    out_specs = [_row_spec(D_MODEL, tm)]
    out_shape = [jax.ShapeDtypeStruct((n, D_MODEL), F32)]
    if ple is not None:
        p, wg, wp = ple
        args += [p.arr, wg.arr, wp.arr]
        specs += [p.row_spec(tm), wg.const_spec(), wp.const_spec()]
    if cast_next is not None:
        nwi, nwo = cast_next
        wi_rows = D_MODEL // steps
        wo_steps = D_FF // WO_CAST_ROWS
        assert D_MODEL % steps == 0 and wi_rows % 16 == 0 and D_FF % WO_CAST_ROWS == 0 and wo_steps <= steps
        lead = (None,) * len(nwi.idx)
        wi_idx, wo_idx = nwi.idx, nwo.idx
        wo_step = lambda t: jnp.minimum(t, wo_steps - 1)
        args += [nwi.arr, nwo.arr]
        specs += [pl.BlockSpec(lead + (wi_rows, 2 * D_FF), lambda t: wi_idx + (t, 0)),
                  pl.BlockSpec(lead + (WO_CAST_ROWS, D_MODEL), lambda t: wo_idx + (wo_step(t), 0))]
        out_specs += [pl.BlockSpec((wi_rows, 2 * D_FF), lambda t: (t, 0)),
                      pl.BlockSpec((WO_CAST_ROWS, D_MODEL), lambda t: (wo_step(t), 0))]
        out_shape += [jax.ShapeDtypeStruct((D_MODEL, 2 * D_FF), BF16), jax.ShapeDtypeStruct((D_FF, D_MODEL), BF16)]
    res = pl.pallas_call(
        functools.partial(_ffn_kernel, ple is not None, cast_next is not None),
        grid=(steps,),
        in_specs=specs,
        out_specs=out_specs,
        out_shape=out_shape,
        compiler_params=_params(("arbitrary",)),
        name="ffn_ple" if ple is not None else "ffn",
    )(*args)
    return res if cast_next is not None else res[0]


def _gmlp_kernel(keep_v, x_ref, wi_ref, bi_ref, lg_ref, lb_ref, ws_ref, bs_ref, wo_ref,
                 g_ref, b_ref, *rest):
    if keep_v:
        o_ref, v_ref, y_ref = rest
    else:
        o_ref, y_ref = rest
    tm = x_ref.shape[0]
    hm = tm // 2
    halves = [slice(h * hm, (h + 1) * hm) for h in range(2)]
    xs = [x_ref[rows, :] for rows in halves]
    zs = [_dot(x.astype(BF16), wi_ref[...]) + bi_ref[...] for x in xs]
    us, vbs = [], []
    for rows, z in zip(halves, zs):
        z = 0.5 * z * (1.0 + lax.erf(z * (1.0 / math.sqrt(2.0))))
        v = _layer_norm(z[:, A_INNER:], lg_ref[...], lb_ref[...])
        if keep_v:
            v_ref[rows, :] = v
        us.append(z[:, :A_INNER])
        vbs.append(v.astype(BF16))
    for half, u, vb in zip(halves, us, vbs):
        for c in range(hm // CHUNK):
            rows = slice(c * CHUNK, (c + 1) * CHUNK)
            out_rows = slice(half.start + c * CHUNK, half.start + (c + 1) * CHUNK)
            for h in range(A_HEADS):
                cols = slice(h * A_GROUP, (h + 1) * A_GROUP)
                mixed = _dot(ws_ref[h], vb[rows, cols]) + bs_ref[:, h:h + 1]
                y_ref[out_rows, cols] = (u[rows, cols] * mixed).astype(BF16)
    for rows, x in zip(halves, xs):
        out = _dot(y_ref[rows, :], wo_ref[...])
        o_ref[rows, :] = _layer_norm(ALPHA * x + out, g_ref[...], b_ref[...])


def _gmlp(x, consts, keep_v):
    n = x.shape[0]
    tm = min(ROW_TILE, n)
    out_shape = [jax.ShapeDtypeStruct((n, D_MODEL), F32)]
    out_specs = [_row_spec(D_MODEL, tm)]
    if keep_v:
        out_shape.append(jax.ShapeDtypeStruct((n, A_INNER), F32))
        out_specs.append(_row_spec(A_INNER, tm))
    res = pl.pallas_call(
        functools.partial(_gmlp_kernel, keep_v),
        grid=(n // tm,),
        in_specs=[_row_spec(D_MODEL, tm)] + [c.const_spec() for c in consts],
        out_specs=out_specs,
        out_shape=out_shape,
        scratch_shapes=[pltpu.VMEM((tm, A_INNER), BF16)],
        compiler_params=_params(("parallel",)),
        name="gmlp",
    )(x, *[c.arr for c in consts])
    return (res[0], res[1]) if keep_v else (res[0], None)


def _conv_tail(x, bg, conv, wo_ref, g_ref, b_ref, o_ref):
    out = _dot((bg * conv).astype(BF16), wo_ref[...])
    o_ref[...] = _layer_norm(ALPHA * x + out, g_ref[...], b_ref[...])


def _conv_long_kernel(tiles_per_seq, x_ref, buf_ref, wi_ref, cw_ref, wo_ref, g_ref, b_ref,
                      o_ref, tail_ref, carry_ref):
    x = x_ref[...]
    tm = x.shape[0]
    h3 = _dot(x.astype(BF16), wi_ref[...])
    bg = h3[:, :D_MODEL]
    z = h3[:, D_MODEL:2 * D_MODEL] * h3[:, 2 * D_MODEL:]

    @pl.when(pl.program_id(0) % tiles_per_seq == 0)
    def _():
        carry_ref[...] = buf_ref[0]

    row = lax.broadcasted_iota(jnp.int32, (tm, D_MODEL), 0)
    prev1 = carry_ref[7:8, :]
    prev2 = carry_ref[6:7, :]
    z1 = jnp.where(row == 0, prev1, pltpu.roll(z, 1, 0))
    z2 = jnp.where(row == 0, prev2, jnp.where(row == 1, prev1, pltpu.roll(z, 2, 0)))
    conv = cw_ref[0:1, :] * z2 + cw_ref[1:2, :] * z1 + cw_ref[2:3, :] * z
    tail = z[tm - 8:, :]
    carry_ref[...] = tail
    tail_ref[0] = tail
    _conv_tail(x, bg, conv, wo_ref, g_ref, b_ref, o_ref)


def _conv_short_kernel(seq, x_ref, h1_ref, h2_ref, wi_ref, cw_ref, wo_ref, g_ref, b_ref,
                       o_ref, z_ref):
    x = x_ref[...]
    tm = x.shape[0]
    h3 = _dot(x.astype(BF16), wi_ref[...])
    bg = h3[:, :D_MODEL]
    z = h3[:, D_MODEL:2 * D_MODEL] * h3[:, 2 * D_MODEL:]
    t = lax.broadcasted_iota(jnp.int32, (tm, D_MODEL), 0) % seq
    z1 = jnp.where(t >= 1, pltpu.roll(z, 1, 0), h1_ref[...])
    z2 = jnp.where(t >= 2, pltpu.roll(z, 2, 0), h2_ref[...])
    conv = cw_ref[0:1, :] * z2 + cw_ref[1:2, :] * z1 + cw_ref[2:3, :] * z
    z_ref[...] = z
    _conv_tail(x, bg, conv, wo_ref, g_ref, b_ref, o_ref)


def _first_rows(state_rows, seq):
    bn = state_rows.shape[0]
    out = jnp.zeros((bn, seq, D_MODEL), F32).at[:, 0, :].set(state_rows)
    return out.reshape(bn * seq, D_MODEL)


def _conv_mixer(x, buf_prev, seq, consts):
    n = x.shape[0]
    bn = n // seq
    tm = min(ROW_TILE, n)
    const_specs = [c.const_spec() for c in consts]
    const_args = [c.arr for c in consts]
    if seq >= tm:
        tiles_per_seq = seq // tm
        buf8 = jnp.concatenate([jnp.zeros((bn, 6, D_MODEL), F32), buf_prev], axis=1)
        out, tails = pl.pallas_call(
            functools.partial(_conv_long_kernel, tiles_per_seq),
            grid=(n // tm,),
            in_specs=[_row_spec(D_MODEL, tm),
                      pl.BlockSpec((1, 8, D_MODEL), lambda i: (i // tiles_per_seq, 0, 0))] + const_specs,
            out_specs=[_row_spec(D_MODEL, tm), pl.BlockSpec((1, 8, D_MODEL), lambda i: (i, 0, 0))],
            out_shape=[jax.ShapeDtypeStruct((n, D_MODEL), F32),
                       jax.ShapeDtypeStruct((n // tm, 8, D_MODEL), F32)],
            scratch_shapes=[pltpu.VMEM((8, D_MODEL), F32)],
            compiler_params=_params(("arbitrary",)),
            name="conv_long",
        )(x, buf8, *const_args)
        new_buf = tails[tiles_per_seq - 1::tiles_per_seq, 6:8, :]
        return out, new_buf
    assert tm % seq == 0 and seq >= CONV_W - 1
    h1 = _first_rows(buf_prev[:, 1, :], seq)
    h2 = _first_rows(buf_prev[:, 0, :], seq) + jnp.roll(h1, 1, axis=0)
    out, z = pl.pallas_call(
        functools.partial(_conv_short_kernel, seq),
        grid=(n // tm,),
        in_specs=[_row_spec(D_MODEL, tm)] * 3 + const_specs,
        out_specs=[_row_spec(D_MODEL, tm)] * 2,
        out_shape=[jax.ShapeDtypeStruct((n, D_MODEL), F32)] * 2,
        compiler_params=_params(("parallel",)),
        name="conv_short",
    )(x, h1, h2, *const_args)
    new_buf = z.reshape(bn, seq, D_MODEL)[:, seq - (CONV_W - 1):, :]
    return out, new_buf


def _head_sum_bcast(x, e_ref, et_ref):
    s = _dot(x.astype(BF16), e_ref[...])
    s_hi, s_lo = _split2(s)
    return _dot(s_hi, et_ref[...]) + _dot(s_lo, et_ref[...])


def _rwkv_proj_body(x, xprev, mu_ref, wrkv_ref, w0_ref, w1_ref, w2_ref, a0_ref, a1_ref,
                    a2_ref, g1_ref, g2_ref, kk_ref, ka_ref, rk_ref, e_ref, et_ref,
                    r_out, lw_out, k_out, v_out, c_out, b_out, bonus_out, g_out):
    xx = xprev - x
    mix = lambda i: (x + xx * mu_ref[i:i + 1, :]).astype(BF16)
    r = _dot(mix(0), wrkv_ref[0])
    k = _dot(mix(2), wrkv_ref[1])
    v = _dot(mix(3), wrkv_ref[2])
    zw = w0_ref[...] + _dot(jnp.tanh(_dot(mix(1), w1_ref[...])).astype(BF16), w2_ref[...])
    lw_out[...] = -_sigmoid(zw) * math.exp(-0.5)
    a = _sigmoid(a0_ref[...] + _dot(_dot(mix(4), a1_ref[...]).astype(BF16), a2_ref[...]))
    g_out[...] = _dot(_sigmoid(_dot(mix(5), g1_ref[...])).astype(BF16), g2_ref[...])
    kk = k * kk_ref[...]
    norm = jnp.sqrt(_head_sum_bcast(kk * kk, e_ref, et_ref))
    c = kk / jnp.maximum(norm, 1e-12)
    kmod = k * (1.0 + (a - 1.0) * ka_ref[...])
    r_out[...] = r
    k_out[...] = kmod
    v_out[...] = v
    c_out[...] = c
    b_out[...] = c * a
    bonus_out[...] = _head_sum_bcast(r * kmod * rk_ref[...], e_ref, et_ref) * v


def _rwkv_proj_long_kernel(tiles_per_seq, x_ref, shift_ref, *rest):
    carry_ref = rest[-1]
    x = x_ref[...]
    tm = x.shape[0]

    @pl.when(pl.program_id(0) % tiles_per_seq == 0)
    def _():
        carry_ref[...] = shift_ref[...]

    row = lax.broadcasted_iota(jnp.int32, (tm, D_MODEL), 0)
    xprev = jnp.where(row == 0, carry_ref[...], pltpu.roll(x, 1, 0))
    carry_ref[...] = x[tm - 1:tm, :]
    _rwkv_proj_body(x, xprev, *rest[:-1])


def _rwkv_proj_short_kernel(seq, x_ref, h1_ref, *rest):
    x = x_ref[...]
    t = lax.broadcasted_iota(jnp.int32, x.shape, 0) % seq
    xprev = jnp.where(t >= 1, pltpu.roll(x, 1, 0), h1_ref[...])
    _rwkv_proj_body(x, xprev, *rest)


def _rwkv_proj(x, shift_prev, seq, consts):
    n = x.shape[0]
    tm = min(ROW_TILE // 2, n)
    common = dict(
        grid=(n // tm,),
        out_specs=[_row_spec(D_MODEL, tm)] * 8,
        out_shape=[jax.ShapeDtypeStruct((n, D_MODEL), F32)] * 8,
    )
    const_specs = [c.const_spec() for c in consts]
    const_args = [c.arr for c in consts]
    if seq >= tm:
        tiles_per_seq = seq // tm
        return pl.pallas_call(
            functools.partial(_rwkv_proj_long_kernel, tiles_per_seq),
            in_specs=[_row_spec(D_MODEL, tm),
                      pl.BlockSpec((None, 1, D_MODEL), lambda i: (i // tiles_per_seq, 0, 0))] + const_specs,
            scratch_shapes=[pltpu.VMEM((1, D_MODEL), F32)],
            compiler_params=_params(("arbitrary",)),
            name="rwkv_proj_long", **common,
        )(x, shift_prev[:, None, :], *const_args)
    assert tm % seq == 0
    return pl.pallas_call(
        functools.partial(_rwkv_proj_short_kernel, seq),
        in_specs=[_row_spec(D_MODEL, tm)] * 2 + const_specs,
        compiler_params=_params(("parallel",)),
        name="rwkv_proj_short", **common,
    )(x, _first_rows(shift_prev, seq), *const_args)


def _pair_rows(x, lane_lo):
    zero = jnp.zeros_like(x)
    return jnp.concatenate([jnp.where(lane_lo, x, zero), jnp.where(lane_lo, zero, x)], axis=0)


def _block_diag(a, b):
    zero = jnp.zeros_like(a)
    return jnp.concatenate([jnp.concatenate([a, zero], axis=1), jnp.concatenate([zero, b], axis=1)], axis=0)


def _scan_chunk(load, store_y, n_chains, qs_fn, state_fn):
    C = SCAN_CHUNK
    pairs = range(n_chains)
    row = lax.broadcasted_iota(jnp.int32, (C, 2 * C), 0)
    col = lax.broadcasted_iota(jnp.int32, (C, 2 * C), 1) & (C - 1)
    strict = row > col
    incl = row >= col
    lane_lo = lax.broadcasted_iota(jnp.int32, (C, LANES), 1) < HEAD_SIZE
    tri = jnp.where(lax.broadcasted_iota(jnp.int32, (C, C), 0) >= lax.broadcasted_iota(jnp.int32, (C, C), 1),
                    1.0, 0.0).astype(BF16)
    r, lw, k, v, c, b = zip(*[load(p) for p in pairs])

    cum = []
    for p in pairs:
        l_hi, l_mid, l_lo = _split3(lw[p])
        cum.append(_dot(tri, l_hi) + (_dot(tri, l_mid) + _dot(tri, l_lo)))
    p_inc = [jnp.exp(cum[p]) for p in pairs]
    p_inv = [jnp.exp(-cum[p]) for p in pairs]
    p_exc = [jnp.exp(cum[p] - lw[p]) for p in pairs]
    q = [jnp.concatenate([c[p] * p_exc[p], r[p] * p_inc[p]], axis=0).astype(BF16) for p in pairs]
    bt = [(b[p] * p_inv[p]).astype(BF16) for p in pairs]
    kt = [(k[p] * p_inv[p]).astype(BF16) for p in pairs]
    vb = [v[p].astype(BF16) for p in pairs]
    kb = [jnp.concatenate([_pair_rows(bt[p], lane_lo), _pair_rows(kt[p], lane_lo)], axis=0) for p in pairs]
    gram = [_dot(q[p], kb[p], NT) for p in pairs]
    l_cb = [jnp.where(strict, gram[p][:C, :2 * C], 0.0) for p in pairs]
    l_ck = [jnp.where(strict, gram[p][:C, 2 * C:], 0.0).astype(BF16) for p in pairs]
    a_rb = [jnp.where(incl, gram[p][C:, :2 * C], 0.0) for p in pairs]
    a_rk = [jnp.where(incl, gram[p][C:, 2 * C:], 0.0) for p in pairs]
    qs = qs_fn(q)
    v_rows = [_pair_rows(vb[p], lane_lo) for p in pairs]
    u = [-(qs[p][:C] + _dot(l_ck[p], v_rows[p])) for p in pairs]
    m = [l_cb[p].astype(BF16) for p in pairs]
    u = [u[p] - _dot(m[p], _pair_rows(u[p].astype(BF16), lane_lo)) for p in pairs]
    for _ in range(int(math.log2(C)) - 1):
        m = [_dot(m[p], _pair_rows(m[p], lane_lo)).astype(BF16) for p in pairs]
        u = [u[p] + _dot(m[p], _pair_rows(u[p].astype(BF16), lane_lo)) for p in pairs]
    ub = [u[p].astype(BF16) for p in pairs]
    for p in pairs:
        a = jnp.concatenate([a_rb[p], a_rk[p]], axis=1).astype(BF16)
        uv_rows = jnp.concatenate([_pair_rows(ub[p], lane_lo), v_rows[p]], axis=0)
        store_y(p, qs[p][C:] + _dot(a, uv_rows))
    state_fn(u, v, ub, vb, bt, kt, p_inc)


def _pair_lanes(p):
    return slice(p * LANES, (p + 1) * LANES)


def _diag_blocks_mask():
    sq_row = lax.broadcasted_iota(jnp.int32, (LANES, LANES), 0) < HEAD_SIZE
    sq_col = lax.broadcasted_iota(jnp.int32, (LANES, LANES), 1) < HEAD_SIZE
    return sq_row == sq_col


def _store_pair_state(out_ref, i, p, s_pair):
    out_ref[i, 2 * p] = s_pair[:HEAD_SIZE, :HEAD_SIZE]
    out_ref[i, 2 * p + 1] = s_pair[HEAD_SIZE:, HEAD_SIZE:]


SCAN_SEQS = 4


def _rwkv_scan_kernel(r_ref, lw_ref, k_ref, v_ref, c_ref, b_ref, s0_ref, y_ref, sfin_ref, s_ref):
    C = SCAN_CHUNK
    j = pl.program_id(1)
    chains = [(s, p) for s in range(SCAN_SEQS) for p in range(HEAD_PAIRS)]

    @pl.when(j == 0)
    def _():
        for s, p in chains:
            s_ref[s, p] = _block_diag(s0_ref[s, 2 * p], s0_ref[s, 2 * p + 1])

    diag = _diag_blocks_mask()
    refs = (r_ref, lw_ref, k_ref, v_ref, c_ref, b_ref)

    def load(ch):
        s, p = chains[ch]
        return tuple(ref[s, :, _pair_lanes(p)] for ref in refs)

    def store_y(ch, y):
        s, p = chains[ch]
        y_ref[s, :, _pair_lanes(p)] = y

    def qs_fn(q):
        return [_dot(q[ch], s_ref[s, p].astype(BF16), NT) for ch, (s, p) in enumerate(chains)]

    def state_fn(u, v, ub, vb, bt, kt, p_inc):
        for ch, (s, p) in enumerate(chains):
            upd = _dot(jnp.concatenate([ub[ch], vb[ch]], axis=0), jnp.concatenate([bt[ch], kt[ch]], axis=0), TN)
            s_ref[s, p] = (s_ref[s, p] + jnp.where(diag, upd, 0.0)) * p_inc[ch][C - 1:C, :]

    _scan_chunk(load, store_y, len(chains), qs_fn, state_fn)

    @pl.when(j == pl.num_programs(1) - 1)
    def _():
        for s, p in chains:
            _store_pair_state(sfin_ref, s, p, s_ref[s, p])


def _rwkv_scan(r, lw, k, v, c, b, s0, bn, seq):
    assert bn % SCAN_SEQS == 0
    row_spec = pl.BlockSpec((SCAN_SEQS, SCAN_CHUNK, D_MODEL), lambda i, j: (i, j, 0))
    st_spec = pl.BlockSpec((SCAN_SEQS, B_HEADS, HEAD_SIZE, HEAD_SIZE), lambda i, j: (i, 0, 0, 0))
    as_seqs = lambda z: z.reshape(bn, seq, D_MODEL)
    ys, s_new = pl.pallas_call(
        _rwkv_scan_kernel,
        grid=(bn // SCAN_SEQS, seq // SCAN_CHUNK),
        in_specs=[row_spec] * 6 + [st_spec],
        out_specs=[row_spec, st_spec],
        out_shape=[jax.ShapeDtypeStruct((bn, seq, D_MODEL), F32),
                   jax.ShapeDtypeStruct((bn, B_HEADS, HEAD_SIZE, HEAD_SIZE), F32)],
        scratch_shapes=[pltpu.VMEM((SCAN_SEQS, HEAD_PAIRS, LANES, LANES), F32)],
        compiler_params=_params(("parallel", "arbitrary")),
        name="rwkv_scan",
    )(*(as_seqs(z) for z in (r, lw, k, v, c, b)), s0)
    return ys.reshape(bn * seq, D_MODEL), s_new


def _rwkv_scan_lanes_kernel(seq, r_ref, lw_ref, k_ref, v_ref, c_ref, b_ref, s0_ref, y_ref, sfin_ref,
                            cols_ref, yt_ref):
    nb = s0_ref.shape[-1]
    for a, ref in enumerate((r_ref, lw_ref, k_ref, v_ref, c_ref, b_ref)):
        for t in range(seq):
            col = ref[pl.ds(t, nb, stride=seq), :].T
            cols_ref[a, t] = jnp.exp(col) if ref is lw_ref else col
    for hh in range(2):
        feat = slice(hh * HEAD_SIZE, (hh + 1) * HEAD_SIZE)
        for vi in range(HEAD_SIZE):
            row = hh * HEAD_SIZE + vi
            s = s0_ref[hh, vi]
            for t in range(seq):
                r, w, k, c, b = (cols_ref[a, t, feat, :] for a in (0, 1, 2, 4, 5))
                sa = jnp.sum(s * c, axis=0, keepdims=True)
                s = s * w - sa * b + cols_ref[3, t, row:row + 1, :] * k
                yt_ref[t, row:row + 1, :] = jnp.sum(s * r, axis=0, keepdims=True)
            sfin_ref[hh, vi] = s
    for t in range(seq):
        y_ref[pl.ds(t, nb, stride=seq), :] = yt_ref[t].T


def _rwkv_scan_lanes(r, lw, k, v, c, b, s0, bn, seq):
    assert bn == LANES
    n = bn * seq
    row_spec = pl.BlockSpec((n, LANES), lambda p: (0, p))
    st_spec = pl.BlockSpec((2, HEAD_SIZE, HEAD_SIZE, bn), lambda p: (p, 0, 0, 0))
    ys, s_new = pl.pallas_call(
        functools.partial(_rwkv_scan_lanes_kernel, seq),
        grid=(HEAD_PAIRS,),
        in_specs=[row_spec] * 6 + [st_spec],
        out_specs=[row_spec, st_spec],
        out_shape=[jax.ShapeDtypeStruct((n, D_MODEL), F32),
                   jax.ShapeDtypeStruct((B_HEADS, HEAD_SIZE, HEAD_SIZE, bn), F32)],
        scratch_shapes=[pltpu.VMEM((6, seq, LANES, bn), F32), pltpu.VMEM((seq, LANES, bn), F32)],
        compiler_params=_params(("parallel",)),
        name="rwkv_scan_lanes",
    )(r, lw, k, v, c, b, jnp.transpose(s0, (1, 2, 3, 0)))
    return ys, jnp.transpose(s_new, (3, 0, 1, 2))


def _rwkv_post_kernel(x_ref, y_ref, bonus_ref, gate_ref, lg_ref, lb_ref, wo_ref, e_ref, et_ref,
                      g_ref, b_ref, o_ref):
    ys = y_ref[...]
    inv = 1.0 / HEAD_SIZE
    m = _head_sum_bcast(ys, e_ref, et_ref) * inv
    yc = ys - m
    var = _head_sum_bcast(yc * yc, e_ref, et_ref) * inv
    yn = yc * lax.rsqrt(var + GN_EPS) * lg_ref[...] + lb_ref[...]
    out = _dot(((yn + bonus_ref[...]) * gate_ref[...]).astype(BF16), wo_ref[...])
    o_ref[...] = _layer_norm(ALPHA * x_ref[...] + out, g_ref[...], b_ref[...])


def _rwkv_post(x, ys, bonus, gate, consts):
    n = x.shape[0]
    tm = min(ROW_TILE, n)
    return pl.pallas_call(
        _rwkv_post_kernel,
        grid=(n // tm,),
        in_specs=[_row_spec(D_MODEL, tm)] * 4 + [c.const_spec() for c in consts],
        out_specs=_row_spec(D_MODEL, tm),
        out_shape=jax.ShapeDtypeStruct((n, D_MODEL), F32),
        compiler_params=_params(("parallel",)),
        name="rwkv_post",
    )(x, ys, bonus, gate, *[c.arr for c in consts])


def _rwkv_mixer(x, shift_prev, s0, seq, w, g, b):
    n = x.shape[0]
    bn = n // seq
    r, lw, k, v, c, bb, bonus, gate = _rwkv_proj(x, shift_prev, seq, w["proj"])
    scan = _rwkv_scan if seq % SCAN_CHUNK == 0 else _rwkv_scan_lanes
    ys, s_new = scan(r, lw, k, v, c, bb, s0, bn, seq)
    out = _rwkv_post(x, ys, bonus, gate, w["post"] + [g, b])
    return out, x.reshape(bn, seq, D_MODEL)[:, -1, :], s_new


def _run_trunks(x_long, x_short, p_long, p_short, states_long, states_short, W):
    shapes = [x_long.shape, x_short.shape]
    seqs = [s[1] for s in shapes]
    assert seqs[0] >= CHUNK > seqs[1]
    xs = [x_long.reshape(-1, D_MODEL), x_short.reshape(-1, D_MODEL)]
    ps = [p_long.reshape(DEPTH, -1, PLE_DIM), p_short.reshape(DEPTH, -1, PLE_DIM)]
    states = [states_long, states_short]
    new_v, new_wkv, new_shift, new_conv = [], ([], []), ([], []), ([], [])
    ffn_w = (_sel(W["ffn_w_in"][0, 0].astype(BF16)), _sel(W["ffn_w_out"][0, 0].astype(BF16)))

    def ffn_pair(xs, ffn_w, i, s, ple=None):
        nxt = (i, s + 1) if s == 0 else (i + 1, 0)
        cast_next = None
        if nxt[0] < DEPTH:
            cast_next = (_sel(W["ffn_w_in"], *nxt), _sel(W["ffn_w_out"], *nxt))
        norm = (_sel(W["ln_g"], i, 2 * s), _sel(W["ln_b"], i, 2 * s))
        ple_of = lambda t: None if ple is None else (_sel(ps[t], i),) + ple
        res = _ffn(xs[0], *ffn_w, *norm, ple=ple_of(0), cast_next=cast_next)
        x_short = _ffn(xs[1], *ffn_w, *norm, ple=ple_of(1))
        if cast_next is None:
            return [res, x_short], None
        return [res[0], x_short], (_sel(res[1]), _sel(res[2]))

    for i in range(DEPTH):
        j, kind = divmod(i, N_MIXERS)
        ln_g = lambda s: _sel(W["ln_g"], i, s)
        ln_b = lambda s: _sel(W["ln_b"], i, s)
        xs, ffn_w = ffn_pair(xs, ffn_w, i, 0)
        if kind == 0:
            for t in range(2):
                mix = W["a_mix_long"] if seqs[t] >= CHUNK else W["a_mix_short"]
                consts = [_sel(W[name], j) for name in ("a_w_in", "a_b_in", "a_ln_g", "a_ln_b")]
                consts += [_sel(mix[0], j), _sel(mix[1], j), _sel(W["a_w_out"], j), ln_g(1), ln_b(1)]
                xs[t], v = _gmlp(xs[t], consts, keep_v=(t == 1))
            new_v.append(v.reshape(shapes[1][0], seqs[1], A_INNER))
        elif kind == 1:
            w = dict(proj=[_sel(a, j) for a in W["b_proj"]] + [_sel(W["e"]), _sel(W["et"])],
                     post=[_sel(a, j) for a in W["b_post"]] + [_sel(W["e"]), _sel(W["et"])])
            for t in range(2):
                wkv, shift, _ = states[t]
                xs[t], sh, s = _rwkv_mixer(xs[t], shift[j], wkv[j], seqs[t], w, ln_g(1), ln_b(1))
                new_shift[t].append(sh)
                new_wkv[t].append(s)
        else:
            consts = [_sel(W[name], j) for name in ("c_w_in", "c_conv_w", "c_w_out")] + [ln_g(1), ln_b(1)]
            for t in range(2):
                xs[t], buf = _conv_mixer(xs[t], states[t][2][j], seqs[t], consts)
                new_conv[t].append(buf)
        xs, ffn_w = ffn_pair(xs, ffn_w, i, 1, ple=(_sel(W["ple_w_gate"], i), _sel(W["ple_w_proj"], i)))
    outs = [(xs[t].reshape(shapes[t]), jnp.stack(new_wkv[t]), jnp.stack(new_shift[t]), jnp.stack(new_conv[t]))
            for t in range(2)]
    return outs[0], outs[1], jnp.stack(new_v)


def _gmlp_mix_mats(w_s, b_s, seq):
    l = min(seq, CHUNK)
    ws = jnp.where(jnp.tril(jnp.ones((l, l), dtype=bool)), w_s[..., :l, :l], 0.0)
    reps = CHUNK // l
    if reps > 1:
        pos = jnp.arange(CHUNK) // l
        ws = jnp.where(pos[:, None] == pos[None, :], jnp.tile(ws, (1, 1, reps, reps)), 0.0)
    bias = jnp.tile(jnp.swapaxes(b_s[..., :l], -1, -2), (1, reps, 1))
    return ws.astype(BF16), bias


def kernel(x_prompt, x_sample, state_b_wkv, state_b_shift, state_c_conv, p_prompt, p_sample, ln_g, ln_b, ffn_w_in, ffn_w_out, ple_w_gate, ple_w_proj, a_w_in, a_b_in, a_ln_g, a_ln_b, a_w_s, a_b_s, a_w_out, b_mu, b_w_rkv, b_w0, b_w1, b_w2, b_a0, b_a1, b_a2, b_g1, b_g2, b_k_k, b_k_a, b_r_k, b_lnx_g, b_lnx_b, b_w_o, c_w_in, c_conv_w, c_w_out):
    bf = lambda w: w.astype(BF16)
    row = lambda w: w.reshape(w.shape[0], 1, -1)
    head_of_lane = jnp.arange(D_MODEL) // HEAD_SIZE
    e = (head_of_lane[:, None] == jnp.arange(LANES)[None, :]).astype(BF16)
    W = dict(
        ln_g=ln_g[:, :, None, :], ln_b=ln_b[:, :, None, :],
        ffn_w_in=ffn_w_in, ffn_w_out=ffn_w_out,
        ple_w_gate=bf(ple_w_gate), ple_w_proj=bf(ple_w_proj),
        a_w_in=bf(a_w_in), a_b_in=row(a_b_in), a_ln_g=row(a_ln_g), a_ln_b=row(a_ln_b), a_w_out=bf(a_w_out),
        a_mix_long=_gmlp_mix_mats(a_w_s, a_b_s, x_prompt.shape[1]),
        a_mix_short=_gmlp_mix_mats(a_w_s, a_b_s, x_sample.shape[1]),
        c_w_in=bf(c_w_in), c_conv_w=c_conv_w, c_w_out=bf(c_w_out),
        b_proj=[b_mu, bf(b_w_rkv), row(b_w0), bf(b_w1), bf(b_w2), row(b_a0), bf(b_a1), bf(b_a2),
                bf(b_g1), bf(b_g2), row(b_k_k), row(b_k_a), row(b_r_k)],
        b_post=[row(b_lnx_g), row(b_lnx_b), bf(b_w_o)],
        e=e, et=e.T,
    )
    bp = x_prompt.shape[0]
    n_b = b_mu.shape[0]
    n_c = c_w_in.shape[0]
    zero_wkv = jnp.zeros((n_b, bp) + state_b_wkv.shape[2:], state_b_wkv.dtype)
    zero_shift = jnp.zeros((n_b, bp, D_MODEL), state_b_shift.dtype)
    zero_conv = jnp.zeros((n_c, bp, CONV_W - 1, D_MODEL), state_c_conv.dtype)
    (y_p, wkv_p, shift_p, conv_p), (y_s, wkv_s, shift_s, conv_s), a_v_s = _run_trunks(
        x_prompt, x_sample, p_prompt, p_sample, (zero_wkv, zero_shift, zero_conv),
        (state_b_wkv, state_b_shift, state_c_conv), W)
    return (y_p, y_s, a_v_s, wkv_p, shift_p, conv_p, wkv_s, shift_s, conv_s)
```

```python
import collections
import functools
import math

import jax
import jax.numpy as jnp
from jax import lax
from jax.experimental import pallas as pl
from jax.experimental.pallas import tpu as pltpu

F32 = jnp.float32
BF16 = jnp.bfloat16

D_MODEL = 1024
DEPTH = 4
N_MIXERS = 3
CHUNK = 128
A_INNER = 2 * D_MODEL
A_HEADS = 8
A_GROUP = A_INNER // A_HEADS
HEAD_SIZE = 64
B_HEADS = D_MODEL // HEAD_SIZE
CONV_W = 3
D_FF = 2816
PLE_DIM = 256
ALPHA = (2 * DEPTH) ** 0.25
LN_EPS = 1e-5
GN_EPS = 64e-5

LANES = 128
MXU_DIM = 256
ROW_TILE = 512
SCAN_CHUNK = 64
HEAD_PAIRS = B_HEADS // 2
VMEM_LIMIT = 56 * 1024 * 1024

NN = (((1,), (0,)), ((), ()))
NT = (((1,), (1,)), ((), ()))
TN = (((0,), (0,)), ((), ()))


def _dot(a, b, dims=NN):
    return lax.dot_general(a, b, dims, preferred_element_type=F32)


def _split2(x):
    hi = x.astype(BF16)
    lo = (x - hi.astype(F32)).astype(BF16)
    return hi, lo


def _split3(x):
    hi = x.astype(BF16)
    r1 = x - hi.astype(F32)
    mid = r1.astype(BF16)
    lo = (r1 - mid.astype(F32)).astype(BF16)
    return hi, mid, lo


def _layer_norm(x, g, b, eps=LN_EPS):
    mu = jnp.mean(x, axis=-1, keepdims=True)
    xc = x - mu
    var = jnp.mean(xc * xc, axis=-1, keepdims=True)
    return xc * lax.rsqrt(var + eps) * g + b


def _sigmoid(x):
    return 1.0 / (1.0 + jnp.exp(-x))


class _Sel(collections.namedtuple("_Sel", ["arr", "idx"])):
    def const_spec(self):
        k = len(self.idx)
        rest = self.arr.shape[k:]
        idx = self.idx
        return pl.BlockSpec((None,) * k + rest, lambda *_: idx + (0,) * len(rest),
                            pipeline_mode=pl.Buffered(1))

    def row_spec(self, tm):
        k = len(self.idx)
        idx = self.idx
        return pl.BlockSpec((None,) * k + (tm, self.arr.shape[-1]), lambda i: idx + (i, 0))


def _sel(arr, *idx):
    return _Sel(arr, tuple(idx))


def _row_spec(width, tm=ROW_TILE):
    return pl.BlockSpec((tm, width), lambda i: (i, 0))


def _params(sem):
    return pltpu.CompilerParams(dimension_semantics=sem, vmem_limit_bytes=VMEM_LIMIT)


FF_BLOCKS = ((0, 6 * MXU_DIM), (6 * MXU_DIM, D_FF))
assert D_FF % MXU_DIM == 0


WO_CAST_ROWS = 128


class _TwoStreams:
    def __init__(self, n_long, n_short, tm=ROW_TILE):
        assert n_long % tm == 0 and n_short % tm == 0
        self.tm = tm
        self.long_steps = n_long // tm
        self.grid = (self.long_steps + n_short // tm,)

    def long_step(self, t):
        return jnp.minimum(t, self.long_steps - 1)

    def short_step(self, t):
        return jnp.maximum(t - self.long_steps, 0)

    def row_specs(self, width, idx_long=(), idx_short=()):
        lead_l, lead_s = (None,) * len(idx_long), (None,) * len(idx_short)
        return [pl.BlockSpec(lead_l + (self.tm, width), lambda t: idx_long + (self.long_step(t), 0)),
                pl.BlockSpec(lead_s + (self.tm, width), lambda t: idx_short + (self.short_step(t), 0))]

    def per_stream(self, body):
        is_long = pl.program_id(0) < self.long_steps
        pl.when(is_long)(lambda: body(0))
        pl.when(jnp.logical_not(is_long))(lambda: body(1))


def _ffn_kernel(with_ple, with_cast, streams, xl_ref, xs_ref, wi_ref, wo_ref, g_ref, b_ref, *rest):
    rest = list(rest)
    if with_ple:
        pl_ref, ps_ref, wg_ref, wp_ref = rest[:4]
        del rest[:4]
    if with_cast:
        next_wi_ref, next_wo_ref = rest[:2]
        del rest[:2]
    o_refs = rest[:2]

    def body(stream):
        x = (xl_ref, xs_ref)[stream][...]
        xb = x.astype(BF16)
        acc = None
        for lo, hi in FF_BLOCKS:
            gate = _dot(xb, wi_ref[:, lo:hi])
            up = _dot(xb, wi_ref[:, D_FF + lo:D_FF + hi])
            act = (gate * _sigmoid(gate) * up).astype(BF16)
            part = _dot(act, wo_ref[lo:hi, :])
            acc = part if acc is None else acc + part
        y = _layer_norm(ALPHA * x + 0.5 * acc, g_ref[...], b_ref[...])
        if with_ple:
            gate = _sigmoid(_dot(y.astype(BF16), wg_ref[...]))
            y = y + gate * _dot((pl_ref, ps_ref)[stream][...].astype(BF16), wp_ref[...])
        o_refs[stream][...] = y
        if with_cast and stream == 0:
            cast_wi_ref, cast_wo_ref = rest[2:]
            cast_wi_ref[...] = next_wi_ref[...].astype(BF16)
            cast_wo_ref[...] = next_wo_ref[...].astype(BF16)

    streams.per_stream(body)


def _ffn(x_long, x_short, wi, wo, g, b, ple=None, cast_next=None):
    streams = _TwoStreams(x_long.shape[0], x_short.shape[0])
    steps = streams.long_steps
    consts = [wi, wo, g, b]
    args = [x_long, x_short] + [c.arr for c in consts]
    specs = streams.row_specs(D_MODEL) + [c.const_spec() for c in consts]
    out_specs = streams.row_specs(D_MODEL)
    out_shape = [jax.ShapeDtypeStruct(x_long.shape, F32), jax.ShapeDtypeStruct(x_short.shape, F32)]
    if ple is not None:
        (p_long, p_short), wg, wp = ple
        args += [p_long.arr, p_short.arr, wg.arr, wp.arr]
        specs += streams.row_specs(PLE_DIM, p_long.idx, p_short.idx) + [wg.const_spec(), wp.const_spec()]
    if cast_next is not None:
        nwi, nwo = cast_next
        wi_rows = D_MODEL // steps
        wo_steps = D_FF // WO_CAST_ROWS
        assert D_MODEL % steps == 0 and wi_rows % 16 == 0 and D_FF % WO_CAST_ROWS == 0 and wo_steps <= steps
        lead = (None,) * len(nwi.idx)
        wi_idx, wo_idx = nwi.idx, nwo.idx
        wi_step = streams.long_step
        wo_step = lambda t: jnp.minimum(t, wo_steps - 1)
        args += [nwi.arr, nwo.arr]
        specs += [pl.BlockSpec(lead + (wi_rows, 2 * D_FF), lambda t: wi_idx + (wi_step(t), 0)),
                  pl.BlockSpec(lead + (WO_CAST_ROWS, D_MODEL), lambda t: wo_idx + (wo_step(t), 0))]
        out_specs += [pl.BlockSpec((wi_rows, 2 * D_FF), lambda t: (wi_step(t), 0)),
                      pl.BlockSpec((WO_CAST_ROWS, D_MODEL), lambda t: (wo_step(t), 0))]
        out_shape += [jax.ShapeDtypeStruct((D_MODEL, 2 * D_FF), BF16), jax.ShapeDtypeStruct((D_FF, D_MODEL), BF16)]
    return pl.pallas_call(
        functools.partial(_ffn_kernel, ple is not None, cast_next is not None, streams),
        grid=streams.grid,
        in_specs=specs,
        out_specs=out_specs,
        out_shape=out_shape,
        compiler_params=_params(("arbitrary",)),
        name="ffn_ple" if ple is not None else "ffn",
    )(*args)


def _gmlp_kernel(streams, xl_ref, xs_ref, wi_ref, bi_ref, lg_ref, lb_ref, wo_ref, g_ref, b_ref,
                 wsl_ref, bsl_ref, wss_ref, bss_ref, ol_ref, os_ref, v_ref, y_ref):
    def body(stream):
        x_ref, ws_ref, bs_ref, o_ref = ((xl_ref, wsl_ref, bsl_ref, ol_ref), (xs_ref, wss_ref, bss_ref, os_ref))[stream]
        hm = streams.tm // 2
        halves = [slice(h * hm, (h + 1) * hm) for h in range(2)]
        xs = [x_ref[rows, :] for rows in halves]
        zs = [_dot(x.astype(BF16), wi_ref[...]) + bi_ref[...] for x in xs]
        us, vbs = [], []
        for rows, z in zip(halves, zs):
            z = 0.5 * z * (1.0 + lax.erf(z * (1.0 / math.sqrt(2.0))))
            v = _layer_norm(z[:, A_INNER:], lg_ref[...], lb_ref[...])
            if stream == 1:
                v_ref[rows, :] = v
            us.append(z[:, :A_INNER])
            vbs.append(v.astype(BF16))
        for half, u, vb in zip(halves, us, vbs):
            for c in range(hm // CHUNK):
                rows = slice(c * CHUNK, (c + 1) * CHUNK)
                out_rows = slice(half.start + c * CHUNK, half.start + (c + 1) * CHUNK)
                for h in range(A_HEADS):
                    cols = slice(h * A_GROUP, (h + 1) * A_GROUP)
                    mixed = _dot(ws_ref[h], vb[rows, cols]) + bs_ref[:, h:h + 1]
                    y_ref[out_rows, cols] = (u[rows, cols] * mixed).astype(BF16)
        for rows, x in zip(halves, xs):
            out = _dot(y_ref[rows, :], wo_ref[...])
            o_ref[rows, :] = _layer_norm(ALPHA * x + out, g_ref[...], b_ref[...])

    streams.per_stream(body)


def _gmlp(x_long, x_short, consts, mix_long, mix_short):
    streams = _TwoStreams(x_long.shape[0], x_short.shape[0])
    consts = consts + list(mix_long) + list(mix_short)
    return pl.pallas_call(
        functools.partial(_gmlp_kernel, streams),
        grid=streams.grid,
        in_specs=streams.row_specs(D_MODEL) + [c.const_spec() for c in consts],
        out_specs=streams.row_specs(D_MODEL) + [streams.row_specs(A_INNER)[1]],
        out_shape=[jax.ShapeDtypeStruct(x_long.shape, F32), jax.ShapeDtypeStruct(x_short.shape, F32),
                   jax.ShapeDtypeStruct((x_short.shape[0], A_INNER), F32)],
        scratch_shapes=[pltpu.VMEM((streams.tm, A_INNER), BF16)],
        compiler_params=_params(("arbitrary",)),
        name="gmlp",
    )(x_long, x_short, *[c.arr for c in consts])


def _conv_tail(x, bg, conv, wo_ref, g_ref, b_ref, o_ref):
    out = _dot((bg * conv).astype(BF16), wo_ref[...])
    o_ref[...] = _layer_norm(ALPHA * x + out, g_ref[...], b_ref[...])


def _conv_long_kernel(tiles_per_seq, x_ref, buf_ref, wi_ref, cw_ref, wo_ref, g_ref, b_ref,
                      o_ref, tail_ref, carry_ref):
    x = x_ref[...]
    tm = x.shape[0]
    h3 = _dot(x.astype(BF16), wi_ref[...])
    bg = h3[:, :D_MODEL]
    z = h3[:, D_MODEL:2 * D_MODEL] * h3[:, 2 * D_MODEL:]

    @pl.when(pl.program_id(0) % tiles_per_seq == 0)
    def _():
        carry_ref[...] = buf_ref[0]

    row = lax.broadcasted_iota(jnp.int32, (tm, D_MODEL), 0)
    prev1 = carry_ref[7:8, :]
    prev2 = carry_ref[6:7, :]
    z1 = jnp.where(row == 0, prev1, pltpu.roll(z, 1, 0))
    z2 = jnp.where(row == 0, prev2, jnp.where(row == 1, prev1, pltpu.roll(z, 2, 0)))
    conv = cw_ref[0:1, :] * z2 + cw_ref[1:2, :] * z1 + cw_ref[2:3, :] * z
    tail = z[tm - 8:, :]
    carry_ref[...] = tail
    tail_ref[0] = tail
    _conv_tail(x, bg, conv, wo_ref, g_ref, b_ref, o_ref)


def _conv_short_kernel(seq, x_ref, h1_ref, h2_ref, wi_ref, cw_ref, wo_ref, g_ref, b_ref,
                       o_ref, z_ref):
    x = x_ref[...]
    tm = x.shape[0]
    h3 = _dot(x.astype(BF16), wi_ref[...])
    bg = h3[:, :D_MODEL]
    z = h3[:, D_MODEL:2 * D_MODEL] * h3[:, 2 * D_MODEL:]
    t = lax.broadcasted_iota(jnp.int32, (tm, D_MODEL), 0) % seq
    z1 = jnp.where(t >= 1, pltpu.roll(z, 1, 0), h1_ref[...])
    z2 = jnp.where(t >= 2, pltpu.roll(z, 2, 0), h2_ref[...])
    conv = cw_ref[0:1, :] * z2 + cw_ref[1:2, :] * z1 + cw_ref[2:3, :] * z
    z_ref[...] = z
    _conv_tail(x, bg, conv, wo_ref, g_ref, b_ref, o_ref)


def _first_rows(state_rows, seq):
    bn = state_rows.shape[0]
    out = jnp.zeros((bn, seq, D_MODEL), F32).at[:, 0, :].set(state_rows)
    return out.reshape(bn * seq, D_MODEL)


def _conv_mixer(x, buf_prev, seq, consts):
    n = x.shape[0]
    bn = n // seq
    tm = min(ROW_TILE, n)
    const_specs = [c.const_spec() for c in consts]
    const_args = [c.arr for c in consts]
    if seq >= tm:
        tiles_per_seq = seq // tm
        buf8 = jnp.concatenate([jnp.zeros((bn, 6, D_MODEL), F32), buf_prev], axis=1)
        out, tails = pl.pallas_call(
            functools.partial(_conv_long_kernel, tiles_per_seq),
            grid=(n // tm,),
            in_specs=[_row_spec(D_MODEL, tm),
                      pl.BlockSpec((1, 8, D_MODEL), lambda i: (i // tiles_per_seq, 0, 0))] + const_specs,
            out_specs=[_row_spec(D_MODEL, tm), pl.BlockSpec((1, 8, D_MODEL), lambda i: (i, 0, 0))],
            out_shape=[jax.ShapeDtypeStruct((n, D_MODEL), F32),
                       jax.ShapeDtypeStruct((n // tm, 8, D_MODEL), F32)],
            scratch_shapes=[pltpu.VMEM((8, D_MODEL), F32)],
            compiler_params=_params(("arbitrary",)),
            name="conv_long",
        )(x, buf8, *const_args)
        new_buf = tails[tiles_per_seq - 1::tiles_per_seq, 6:8, :]
        return out, new_buf
    assert tm % seq == 0 and seq >= CONV_W - 1
    h1 = _first_rows(buf_prev[:, 1, :], seq)
    h2 = _first_rows(buf_prev[:, 0, :], seq) + jnp.roll(h1, 1, axis=0)
    out, z = pl.pallas_call(
        functools.partial(_conv_short_kernel, seq),
        grid=(n // tm,),
        in_specs=[_row_spec(D_MODEL, tm)] * 3 + const_specs,
        out_specs=[_row_spec(D_MODEL, tm)] * 2,
        out_shape=[jax.ShapeDtypeStruct((n, D_MODEL), F32)] * 2,
        compiler_params=_params(("parallel",)),
        name="conv_short",
    )(x, h1, h2, *const_args)
    new_buf = z.reshape(bn, seq, D_MODEL)[:, seq - (CONV_W - 1):, :]
    return out, new_buf


def _head_sum_bcast(x, e_ref, et_ref):
    s = _dot(x.astype(BF16), e_ref[...])
    s_hi, s_lo = _split2(s)
    return _dot(s_hi, et_ref[...]) + _dot(s_lo, et_ref[...])


def _rwkv_proj_body(x, xprev, mu_ref, wrkv_ref, w0_ref, w1_ref, w2_ref, a0_ref, a1_ref,
                    a2_ref, g1_ref, g2_ref, kk_ref, ka_ref, rk_ref, e_ref, et_ref,
                    r_out, lw_out, k_out, v_out, c_out, b_out, bonus_out, g_out):
    xx = xprev - x
    mix = lambda i: (x + xx * mu_ref[i:i + 1, :]).astype(BF16)
    r = _dot(mix(0), wrkv_ref[0])
    k = _dot(mix(2), wrkv_ref[1])
    v = _dot(mix(3), wrkv_ref[2])
    zw = w0_ref[...] + _dot(jnp.tanh(_dot(mix(1), w1_ref[...])).astype(BF16), w2_ref[...])
    lw_out[...] = -_sigmoid(zw) * math.exp(-0.5)
    a = _sigmoid(a0_ref[...] + _dot(_dot(mix(4), a1_ref[...]).astype(BF16), a2_ref[...]))
    g_out[...] = _dot(_sigmoid(_dot(mix(5), g1_ref[...])).astype(BF16), g2_ref[...])
    kk = k * kk_ref[...]
    norm = jnp.sqrt(_head_sum_bcast(kk * kk, e_ref, et_ref))
    c = kk / jnp.maximum(norm, 1e-12)
    kmod = k * (1.0 + (a - 1.0) * ka_ref[...])
    r_out[...] = r
    k_out[...] = kmod
    v_out[...] = v
    c_out[...] = c
    b_out[...] = c * a
    bonus_out[...] = _head_sum_bcast(r * kmod * rk_ref[...], e_ref, et_ref) * v


def _rwkv_proj_long_kernel(tiles_per_seq, x_ref, shift_ref, *rest):
    carry_ref = rest[-1]
    x = x_ref[...]
    tm = x.shape[0]

    @pl.when(pl.program_id(0) % tiles_per_seq == 0)
    def _():
        carry_ref[...] = shift_ref[...]

    row = lax.broadcasted_iota(jnp.int32, (tm, D_MODEL), 0)
    xprev = jnp.where(row == 0, carry_ref[...], pltpu.roll(x, 1, 0))
    carry_ref[...] = x[tm - 1:tm, :]
    _rwkv_proj_body(x, xprev, *rest[:-1])


def _rwkv_proj_short_kernel(seq, x_ref, h1_ref, *rest):
    x = x_ref[...]
    t = lax.broadcasted_iota(jnp.int32, x.shape, 0) % seq
    xprev = jnp.where(t >= 1, pltpu.roll(x, 1, 0), h1_ref[...])
    _rwkv_proj_body(x, xprev, *rest)


def _rwkv_proj(x, shift_prev, seq, consts):
    n = x.shape[0]
    tm = min(ROW_TILE // 2, n)
    common = dict(
        grid=(n // tm,),
        out_specs=[_row_spec(D_MODEL, tm)] * 8,
        out_shape=[jax.ShapeDtypeStruct((n, D_MODEL), F32)] * 8,
    )
    const_specs = [c.const_spec() for c in consts]
    const_args = [c.arr for c in consts]
    if seq >= tm:
        tiles_per_seq = seq // tm
        return pl.pallas_call(
            functools.partial(_rwkv_proj_long_kernel, tiles_per_seq),
            in_specs=[_row_spec(D_MODEL, tm),
                      pl.BlockSpec((None, 1, D_MODEL), lambda i: (i // tiles_per_seq, 0, 0))] + const_specs,
            scratch_shapes=[pltpu.VMEM((1, D_MODEL), F32)],
            compiler_params=_params(("arbitrary",)),
            name="rwkv_proj_long", **common,
        )(x, shift_prev[:, None, :], *const_args)
    assert tm % seq == 0
    return pl.pallas_call(
        functools.partial(_rwkv_proj_short_kernel, seq),
        in_specs=[_row_spec(D_MODEL, tm)] * 2 + const_specs,
        compiler_params=_params(("parallel",)),
        name="rwkv_proj_short", **common,
    )(x, _first_rows(shift_prev, seq), *const_args)


def _pair_rows(x, lane_lo):
    zero = jnp.zeros_like(x)
    return jnp.concatenate([jnp.where(lane_lo, x, zero), jnp.where(lane_lo, zero, x)], axis=0)


def _block_diag(a, b):
    zero = jnp.zeros_like(a)
    return jnp.concatenate([jnp.concatenate([a, zero], axis=1), jnp.concatenate([zero, b], axis=1)], axis=0)


def _scan_chunk(load, store_y, n_chains, qs_fn, state_fn):
    C = SCAN_CHUNK
    pairs = range(n_chains)
    row = lax.broadcasted_iota(jnp.int32, (C, 2 * C), 0)
    col = lax.broadcasted_iota(jnp.int32, (C, 2 * C), 1) & (C - 1)
    strict = row > col
    incl = row >= col
    lane_lo = lax.broadcasted_iota(jnp.int32, (C, LANES), 1) < HEAD_SIZE
    tri = jnp.where(lax.broadcasted_iota(jnp.int32, (C, C), 0) >= lax.broadcasted_iota(jnp.int32, (C, C), 1),
                    1.0, 0.0).astype(BF16)
    r, lw, k, v, c, b = zip(*[load(p) for p in pairs])

    cum = []
    for p in pairs:
        l_hi, l_mid, l_lo = _split3(lw[p])
        cum.append(_dot(tri, l_hi) + (_dot(tri, l_mid) + _dot(tri, l_lo)))
    p_inc = [jnp.exp(cum[p]) for p in pairs]
    p_inv = [jnp.exp(-cum[p]) for p in pairs]
    p_exc = [jnp.exp(cum[p] - lw[p]) for p in pairs]
    q = [jnp.concatenate([c[p] * p_exc[p], r[p] * p_inc[p]], axis=0).astype(BF16) for p in pairs]
    bt = [(b[p] * p_inv[p]).astype(BF16) for p in pairs]
    kt = [(k[p] * p_inv[p]).astype(BF16) for p in pairs]
    vb = [v[p].astype(BF16) for p in pairs]
    kb = [jnp.concatenate([_pair_rows(bt[p], lane_lo), _pair_rows(kt[p], lane_lo)], axis=0) for p in pairs]
    gram = [_dot(q[p], kb[p], NT) for p in pairs]
    l_cb = [jnp.where(strict, gram[p][:C, :2 * C], 0.0) for p in pairs]
    l_ck = [jnp.where(strict, gram[p][:C, 2 * C:], 0.0).astype(BF16) for p in pairs]
    a_rb = [jnp.where(incl, gram[p][C:, :2 * C], 0.0) for p in pairs]
    a_rk = [jnp.where(incl, gram[p][C:, 2 * C:], 0.0) for p in pairs]
    qs = qs_fn(q)
    v_rows = [_pair_rows(vb[p], lane_lo) for p in pairs]
    u = [-(qs[p][:C] + _dot(l_ck[p], v_rows[p])) for p in pairs]
    m = [l_cb[p].astype(BF16) for p in pairs]
    u = [u[p] - _dot(m[p], _pair_rows(u[p].astype(BF16), lane_lo)) for p in pairs]
    for _ in range(int(math.log2(C)) - 1):
        m = [_dot(m[p], _pair_rows(m[p], lane_lo)).astype(BF16) for p in pairs]
        u = [u[p] + _dot(m[p], _pair_rows(u[p].astype(BF16), lane_lo)) for p in pairs]
    ub = [u[p].astype(BF16) for p in pairs]
    for p in pairs:
        a = jnp.concatenate([a_rb[p], a_rk[p]], axis=1).astype(BF16)
        uv_rows = jnp.concatenate([_pair_rows(ub[p], lane_lo), v_rows[p]], axis=0)
        store_y(p, qs[p][C:] + _dot(a, uv_rows))
    state_fn(u, v, ub, vb, bt, kt, p_inc)


def _pair_lanes(p):
    return slice(p * LANES, (p + 1) * LANES)


def _diag_blocks_mask():
    sq_row = lax.broadcasted_iota(jnp.int32, (LANES, LANES), 0) < HEAD_SIZE
    sq_col = lax.broadcasted_iota(jnp.int32, (LANES, LANES), 1) < HEAD_SIZE
    return sq_row == sq_col


def _store_pair_state(out_ref, i, p, s_pair):
    out_ref[i, 2 * p] = s_pair[:HEAD_SIZE, :HEAD_SIZE]
    out_ref[i, 2 * p + 1] = s_pair[HEAD_SIZE:, HEAD_SIZE:]


SCAN_SEQS = 4


def _rwkv_scan_kernel(r_ref, lw_ref, k_ref, v_ref, c_ref, b_ref, s0_ref, y_ref, sfin_ref, s_ref):
    C = SCAN_CHUNK
    j = pl.program_id(1)
    chains = [(s, p) for s in range(SCAN_SEQS) for p in range(HEAD_PAIRS)]

    @pl.when(j == 0)
    def _():
        for s, p in chains:
            s_ref[s, p] = _block_diag(s0_ref[s, 2 * p], s0_ref[s, 2 * p + 1])

    diag = _diag_blocks_mask()
    refs = (r_ref, lw_ref, k_ref, v_ref, c_ref, b_ref)

    def load(ch):
        s, p = chains[ch]
        return tuple(ref[s, :, _pair_lanes(p)] for ref in refs)

    def store_y(ch, y):
        s, p = chains[ch]
        y_ref[s, :, _pair_lanes(p)] = y

    def qs_fn(q):
        return [_dot(q[ch], s_ref[s, p].astype(BF16), NT) for ch, (s, p) in enumerate(chains)]

    def state_fn(u, v, ub, vb, bt, kt, p_inc):
        for ch, (s, p) in enumerate(chains):
            upd = _dot(jnp.concatenate([ub[ch], vb[ch]], axis=0), jnp.concatenate([bt[ch], kt[ch]], axis=0), TN)
            s_ref[s, p] = (s_ref[s, p] + jnp.where(diag, upd, 0.0)) * p_inc[ch][C - 1:C, :]

    _scan_chunk(load, store_y, len(chains), qs_fn, state_fn)

    @pl.when(j == pl.num_programs(1) - 1)
    def _():
        for s, p in chains:
            _store_pair_state(sfin_ref, s, p, s_ref[s, p])


def _rwkv_scan(r, lw, k, v, c, b, s0, bn, seq):
    assert bn % SCAN_SEQS == 0
    row_spec = pl.BlockSpec((SCAN_SEQS, SCAN_CHUNK, D_MODEL), lambda i, j: (i, j, 0))
    st_spec = pl.BlockSpec((SCAN_SEQS, B_HEADS, HEAD_SIZE, HEAD_SIZE), lambda i, j: (i, 0, 0, 0))
    as_seqs = lambda z: z.reshape(bn, seq, D_MODEL)
    ys, s_new = pl.pallas_call(
        _rwkv_scan_kernel,
        grid=(bn // SCAN_SEQS, seq // SCAN_CHUNK),
        in_specs=[row_spec] * 6 + [st_spec],
        out_specs=[row_spec, st_spec],
        out_shape=[jax.ShapeDtypeStruct((bn, seq, D_MODEL), F32),
                   jax.ShapeDtypeStruct((bn, B_HEADS, HEAD_SIZE, HEAD_SIZE), F32)],
        scratch_shapes=[pltpu.VMEM((SCAN_SEQS, HEAD_PAIRS, LANES, LANES), F32)],
        compiler_params=_params(("parallel", "arbitrary")),
        name="rwkv_scan",
    )(*(as_seqs(z) for z in (r, lw, k, v, c, b)), s0)
    return ys.reshape(bn * seq, D_MODEL), s_new


def _rwkv_scan_lanes_kernel(seq, r_ref, lw_ref, k_ref, v_ref, c_ref, b_ref, s0_ref, y_ref, sfin_ref,
                            cols_ref, yt_ref):
    nb = s0_ref.shape[-1]
    for a, ref in enumerate((r_ref, lw_ref, k_ref, v_ref, c_ref, b_ref)):
        for t in range(seq):
            col = ref[pl.ds(t, nb, stride=seq), :].T
            cols_ref[a, t] = jnp.exp(col) if ref is lw_ref else col
    for hh in range(2):
        feat = slice(hh * HEAD_SIZE, (hh + 1) * HEAD_SIZE)
        for vi in range(HEAD_SIZE):
            row = hh * HEAD_SIZE + vi
            s = s0_ref[hh, vi]
            for t in range(seq):
                r, w, k, c, b = (cols_ref[a, t, feat, :] for a in (0, 1, 2, 4, 5))
                sa = jnp.sum(s * c, axis=0, keepdims=True)
                s = s * w - sa * b + cols_ref[3, t, row:row + 1, :] * k
                yt_ref[t, row:row + 1, :] = jnp.sum(s * r, axis=0, keepdims=True)
            sfin_ref[hh, vi] = s
    for t in range(seq):
        y_ref[pl.ds(t, nb, stride=seq), :] = yt_ref[t].T


def _rwkv_scan_lanes(r, lw, k, v, c, b, s0, bn, seq):
    assert bn == LANES
    n = bn * seq
    row_spec = pl.BlockSpec((n, LANES), lambda p: (0, p))
    st_spec = pl.BlockSpec((2, HEAD_SIZE, HEAD_SIZE, bn), lambda p: (p, 0, 0, 0))
    ys, s_new = pl.pallas_call(
        functools.partial(_rwkv_scan_lanes_kernel, seq),
        grid=(HEAD_PAIRS,),
        in_specs=[row_spec] * 6 + [st_spec],
        out_specs=[row_spec, st_spec],
        out_shape=[jax.ShapeDtypeStruct((n, D_MODEL), F32),
                   jax.ShapeDtypeStruct((B_HEADS, HEAD_SIZE, HEAD_SIZE, bn), F32)],
        scratch_shapes=[pltpu.VMEM((6, seq, LANES, bn), F32), pltpu.VMEM((seq, LANES, bn), F32)],
        compiler_params=_params(("parallel",)),
        name="rwkv_scan_lanes",
    )(r, lw, k, v, c, b, jnp.transpose(s0, (1, 2, 3, 0)))
    return ys, jnp.transpose(s_new, (3, 0, 1, 2))


def _rwkv_post_kernel(x_ref, y_ref, bonus_ref, gate_ref, lg_ref, lb_ref, wo_ref, e_ref, et_ref,
                      g_ref, b_ref, o_ref):
    ys = y_ref[...]
    inv = 1.0 / HEAD_SIZE
    m = _head_sum_bcast(ys, e_ref, et_ref) * inv
    yc = ys - m
    var = _head_sum_bcast(yc * yc, e_ref, et_ref) * inv
    yn = yc * lax.rsqrt(var + GN_EPS) * lg_ref[...] + lb_ref[...]
    out = _dot(((yn + bonus_ref[...]) * gate_ref[...]).astype(BF16), wo_ref[...])
    o_ref[...] = _layer_norm(ALPHA * x_ref[...] + out, g_ref[...], b_ref[...])


def _rwkv_post(x, ys, bonus, gate, consts):
    n = x.shape[0]
    tm = min(ROW_TILE, n)
    return pl.pallas_call(
        _rwkv_post_kernel,
        grid=(n // tm,),
        in_specs=[_row_spec(D_MODEL, tm)] * 4 + [c.const_spec() for c in consts],
        out_specs=_row_spec(D_MODEL, tm),
        out_shape=jax.ShapeDtypeStruct((n, D_MODEL), F32),
        compiler_params=_params(("parallel",)),
        name="rwkv_post",
    )(x, ys, bonus, gate, *[c.arr for c in consts])


def _rwkv_mixer(x, shift_prev, s0, seq, w, g, b):
    n = x.shape[0]
    bn = n // seq
    r, lw, k, v, c, bb, bonus, gate = _rwkv_proj(x, shift_prev, seq, w["proj"])
    scan = _rwkv_scan if seq % SCAN_CHUNK == 0 else _rwkv_scan_lanes
    ys, s_new = scan(r, lw, k, v, c, bb, s0, bn, seq)
    out = _rwkv_post(x, ys, bonus, gate, w["post"] + [g, b])
    return out, x.reshape(bn, seq, D_MODEL)[:, -1, :], s_new


def _run_trunks(x_long, x_short, p_long, p_short, states_long, states_short, W):
    shapes = [x_long.shape, x_short.shape]
    seqs = [s[1] for s in shapes]
    assert seqs[0] >= CHUNK > seqs[1]
    xs = [x_long.reshape(-1, D_MODEL), x_short.reshape(-1, D_MODEL)]
    ps = [p_long.reshape(DEPTH, -1, PLE_DIM), p_short.reshape(DEPTH, -1, PLE_DIM)]
    states = [states_long, states_short]
    new_v, new_wkv, new_shift, new_conv = [], ([], []), ([], []), ([], [])
    ffn_w = (_sel(W["ffn_w_in"][0, 0].astype(BF16)), _sel(W["ffn_w_out"][0, 0].astype(BF16)))

    def ffn_pair(xs, ffn_w, i, s, ple=None):
        nxt = (i, s + 1) if s == 0 else (i + 1, 0)
        cast_next = None
        if nxt[0] < DEPTH:
            cast_next = (_sel(W["ffn_w_in"], *nxt), _sel(W["ffn_w_out"], *nxt))
        norm = (_sel(W["ln_g"], i, 2 * s), _sel(W["ln_b"], i, 2 * s))
        if ple is not None:
            ple = ((_sel(ps[0], i), _sel(ps[1], i)),) + ple
        res = _ffn(xs[0], xs[1], *ffn_w, *norm, ple=ple, cast_next=cast_next)
        return list(res[:2]), (None if cast_next is None else (_sel(res[2]), _sel(res[3])))

    for i in range(DEPTH):
        j, kind = divmod(i, N_MIXERS)
        ln_g = lambda s: _sel(W["ln_g"], i, s)
        ln_b = lambda s: _sel(W["ln_b"], i, s)
        xs, ffn_w = ffn_pair(xs, ffn_w, i, 0)
        if kind == 0:
            consts = [_sel(W[name], j) for name in ("a_w_in", "a_b_in", "a_ln_g", "a_ln_b", "a_w_out")]
            x_l, x_s, v = _gmlp(*xs, consts + [ln_g(1), ln_b(1)], [_sel(m, j) for m in W["a_mix_long"]],
                                [_sel(m, j) for m in W["a_mix_short"]])
            xs = [x_l, x_s]
            new_v.append(v.reshape(shapes[1][0], seqs[1], A_INNER))
        elif kind == 1:
            w = dict(proj=[_sel(a, j) for a in W["b_proj"]] + [_sel(W["e"]), _sel(W["et"])],
                     post=[_sel(a, j) for a in W["b_post"]] + [_sel(W["e"]), _sel(W["et"])])
            for t in range(2):
                wkv, shift, _ = states[t]
                xs[t], sh, s = _rwkv_mixer(xs[t], shift[j], wkv[j], seqs[t], w, ln_g(1), ln_b(1))
                new_shift[t].append(sh)
                new_wkv[t].append(s)
        else:
            consts = [_sel(W[name], j) for name in ("c_w_in", "c_conv_w", "c_w_out")] + [ln_g(1), ln_b(1)]
            for t in range(2):
                xs[t], buf = _conv_mixer(xs[t], states[t][2][j], seqs[t], consts)
                new_conv[t].append(buf)
        xs, ffn_w = ffn_pair(xs, ffn_w, i, 1, ple=(_sel(W["ple_w_gate"], i), _sel(W["ple_w_proj"], i)))
    outs = [(xs[t].reshape(shapes[t]), jnp.stack(new_wkv[t]), jnp.stack(new_shift[t]), jnp.stack(new_conv[t]))
            for t in range(2)]
    return outs[0], outs[1], jnp.stack(new_v)


def _gmlp_mix_mats(w_s, b_s, seq):
    l = min(seq, CHUNK)
    ws = jnp.where(jnp.tril(jnp.ones((l, l), dtype=bool)), w_s[..., :l, :l], 0.0)
    reps = CHUNK // l
    if reps > 1:
        pos = jnp.arange(CHUNK) // l
        ws = jnp.where(pos[:, None] == pos[None, :], jnp.tile(ws, (1, 1, reps, reps)), 0.0)
    bias = jnp.tile(jnp.swapaxes(b_s[..., :l], -1, -2), (1, reps, 1))
    return ws.astype(BF16), bias


def kernel(x_prompt, x_sample, state_b_wkv, state_b_shift, state_c_conv, p_prompt, p_sample, ln_g, ln_b, ffn_w_in, ffn_w_out, ple_w_gate, ple_w_proj, a_w_in, a_b_in, a_ln_g, a_ln_b, a_w_s, a_b_s, a_w_out, b_mu, b_w_rkv, b_w0, b_w1, b_w2, b_a0, b_a1, b_a2, b_g1, b_g2, b_k_k, b_k_a, b_r_k, b_lnx_g, b_lnx_b, b_w_o, c_w_in, c_conv_w, c_w_out):
    bf = lambda w: w.astype(BF16)
    row = lambda w: w.reshape(w.shape[0], 1, -1)
    head_of_lane = jnp.arange(D_MODEL) // HEAD_SIZE
    e = (head_of_lane[:, None] == jnp.arange(LANES)[None, :]).astype(BF16)
    W = dict(
        ln_g=ln_g[:, :, None, :], ln_b=ln_b[:, :, None, :],
        ffn_w_in=ffn_w_in, ffn_w_out=ffn_w_out,
        ple_w_gate=bf(ple_w_gate), ple_w_proj=bf(ple_w_proj),
        a_w_in=bf(a_w_in), a_b_in=row(a_b_in), a_ln_g=row(a_ln_g), a_ln_b=row(a_ln_b), a_w_out=bf(a_w_out),
        a_mix_long=_gmlp_mix_mats(a_w_s, a_b_s, x_prompt.shape[1]),
        a_mix_short=_gmlp_mix_mats(a_w_s, a_b_s, x_sample.shape[1]),
        c_w_in=bf(c_w_in), c_conv_w=c_conv_w, c_w_out=bf(c_w_out),
        b_proj=[b_mu, bf(b_w_rkv), row(b_w0), bf(b_w1), bf(b_w2), row(b_a0), bf(b_a1), bf(b_a2),
                bf(b_g1), bf(b_g2), row(b_k_k), row(b_k_a), row(b_r_k)],
        b_post=[row(b_lnx_g), row(b_lnx_b), bf(b_w_o)],
        e=e, et=e.T,
    )
    bp = x_prompt.shape[0]
    n_b = b_mu.shape[0]
    n_c = c_w_in.shape[0]
    zero_wkv = jnp.zeros((n_b, bp) + state_b_wkv.shape[2:], state_b_wkv.dtype)
    zero_shift = jnp.zeros((n_b, bp, D_MODEL), state_b_shift.dtype)
    zero_conv = jnp.zeros((n_c, bp, CONV_W - 1, D_MODEL), state_c_conv.dtype)
    (y_p, wkv_p, shift_p, conv_p), (y_s, wkv_s, shift_s, conv_s), a_v_s = _run_trunks(
        x_prompt, x_sample, p_prompt, p_sample, (zero_wkv, zero_shift, zero_conv),
        (state_b_wkv, state_b_shift, state_c_conv), W)
    return (y_p, y_s, a_v_s, wkv_p, shift_p, conv_p, wkv_s, shift_s, conv_s)
```

```python
import collections
import functools
import math

import jax
import jax.numpy as jnp
from jax import lax
from jax.experimental import pallas as pl
from jax.experimental.pallas import tpu as pltpu

F32 = jnp.float32
BF16 = jnp.bfloat16

D_MODEL = 1024
DEPTH = 4
N_MIXERS = 3
CHUNK = 128
A_INNER = 2 * D_MODEL
A_HEADS = 8
A_GROUP = A_INNER // A_HEADS
HEAD_SIZE = 64
B_HEADS = D_MODEL // HEAD_SIZE
CONV_W = 3
D_FF = 2816
PLE_DIM = 256
ALPHA = (2 * DEPTH) ** 0.25
LN_EPS = 1e-5
GN_EPS = 64e-5

LANES = 128
MXU_DIM = 256
ROW_TILE = 512
SCAN_CHUNK = 64
HEAD_PAIRS = B_HEADS // 2
VMEM_LIMIT = 56 * 1024 * 1024

NN = (((1,), (0,)), ((), ()))
NT = (((1,), (1,)), ((), ()))
TN = (((0,), (0,)), ((), ()))


def _dot(a, b, dims=NN):
    return lax.dot_general(a, b, dims, preferred_element_type=F32)


def _split2(x):
    hi = x.astype(BF16)
    lo = (x - hi.astype(F32)).astype(BF16)
    return hi, lo


def _split3(x):
    hi = x.astype(BF16)
    r1 = x - hi.astype(F32)
    mid = r1.astype(BF16)
    lo = (r1 - mid.astype(F32)).astype(BF16)
    return hi, mid, lo


def _layer_norm(x, g, b, eps=LN_EPS):
    mu = jnp.mean(x, axis=-1, keepdims=True)
    xc = x - mu
    var = jnp.mean(xc * xc, axis=-1, keepdims=True)
    return xc * lax.rsqrt(var + eps) * g + b


def _sigmoid(x):
    return 1.0 / (1.0 + jnp.exp(-x))


class _Sel(collections.namedtuple("_Sel", ["arr", "idx"])):
    def const_spec(self):
        k = len(self.idx)
        rest = self.arr.shape[k:]
        idx = self.idx
        return pl.BlockSpec((None,) * k + rest, lambda *_: idx + (0,) * len(rest),
                            pipeline_mode=pl.Buffered(1))

    def row_spec(self, tm):
        k = len(self.idx)
        idx = self.idx
        return pl.BlockSpec((None,) * k + (tm, self.arr.shape[-1]), lambda i: idx + (i, 0))


def _sel(arr, *idx):
    return _Sel(arr, tuple(idx))


def _row_spec(width, tm=ROW_TILE):
    return pl.BlockSpec((tm, width), lambda i: (i, 0))


def _params(sem):
    return pltpu.CompilerParams(dimension_semantics=sem, vmem_limit_bytes=VMEM_LIMIT)


FF_BLOCKS = ((0, 6 * MXU_DIM), (6 * MXU_DIM, D_FF))
assert D_FF % MXU_DIM == 0


class _TwoStreams:
    def __init__(self, n_long, n_short, tm=ROW_TILE):
        assert n_long % tm == 0 and n_short % tm == 0
        self.tm = tm
        self.long_steps = n_long // tm
        self.grid = (self.long_steps + n_short // tm,)

    def long_step(self, t):
        return jnp.minimum(t, self.long_steps - 1)

    def short_step(self, t):
        return jnp.maximum(t - self.long_steps, 0)

    def row_specs(self, width, idx_long=(), idx_short=()):
        lead_l, lead_s = (None,) * len(idx_long), (None,) * len(idx_short)
        return [pl.BlockSpec(lead_l + (self.tm, width), lambda t: idx_long + (self.long_step(t), 0)),
                pl.BlockSpec(lead_s + (self.tm, width), lambda t: idx_short + (self.short_step(t), 0))]

    def per_stream(self, body):
        is_long = pl.program_id(0) < self.long_steps
        pl.when(is_long)(lambda: body(0))
        pl.when(jnp.logical_not(is_long))(lambda: body(1))


def _ffn_kernel(with_ple, n_cast, streams, xl_ref, xs_ref, wi_ref, wo_ref, g_ref, b_ref, *rest):
    rest = list(rest)
    if with_ple:
        pl_ref, ps_ref, wg_ref, wp_ref = rest[:4]
        del rest[:4]
    cast_in = rest[:n_cast]
    del rest[:n_cast]
    o_refs, cast_out = rest[:2], rest[2:]

    def body(stream):
        x = (xl_ref, xs_ref)[stream][...]
        xb = x.astype(BF16)
        acc = None
        for lo, hi in FF_BLOCKS:
            gate = _dot(xb, wi_ref[:, lo:hi])
            up = _dot(xb, wi_ref[:, D_FF + lo:D_FF + hi])
            act = (gate * _sigmoid(gate) * up).astype(BF16)
            part = _dot(act, wo_ref[lo:hi, :])
            acc = part if acc is None else acc + part
        y = _layer_norm(ALPHA * x + 0.5 * acc, g_ref[...], b_ref[...])
        if with_ple:
            gate = _sigmoid(_dot(y.astype(BF16), wg_ref[...]))
            y = y + gate * _dot((pl_ref, ps_ref)[stream][...].astype(BF16), wp_ref[...])
        o_refs[stream][...] = y
        if stream == 0:
            for src_ref, dst_ref in zip(cast_in, cast_out):
                dst_ref[...] = src_ref[...].astype(BF16)

    streams.per_stream(body)


def _cast_slab_rows(rows, steps):
    for r in range(16, rows + 1, 16):
        if rows % r == 0 and rows // r <= steps:
            return r
    raise ValueError((rows, steps))


def _ffn(x_long, x_short, wi, wo, g, b, ple=None, cast_next=()):
    streams = _TwoStreams(x_long.shape[0], x_short.shape[0])
    consts = [wi, wo, g, b]
    args = [x_long, x_short] + [c.arr for c in consts]
    specs = streams.row_specs(D_MODEL) + [c.const_spec() for c in consts]
    out_specs = streams.row_specs(D_MODEL)
    out_shape = [jax.ShapeDtypeStruct(x_long.shape, F32), jax.ShapeDtypeStruct(x_short.shape, F32)]
    if ple is not None:
        (p_long, p_short), wg, wp = ple
        args += [p_long.arr, p_short.arr, wg.arr, wp.arr]
        specs += streams.row_specs(PLE_DIM, p_long.idx, p_short.idx) + [wg.const_spec(), wp.const_spec()]
    cast_specs = []
    for w in cast_next:
        rows, cols = w.arr.shape[len(w.idx):]
        slab = _cast_slab_rows(rows, streams.long_steps)
        spec = lambda lead, idx, last=rows // slab - 1, slab=slab, cols=cols: pl.BlockSpec(
            (None,) * lead + (slab, cols), lambda t: idx + (jnp.minimum(t, last), 0))
        args.append(w.arr)
        specs.append(spec(len(w.idx), w.idx))
        cast_specs.append(spec(0, ()))
        out_shape.append(jax.ShapeDtypeStruct((rows, cols), BF16))
    out_specs += cast_specs
    return pl.pallas_call(
        functools.partial(_ffn_kernel, ple is not None, len(cast_next), streams),
        grid=streams.grid,
        in_specs=specs,
        out_specs=out_specs,
        out_shape=out_shape,
        compiler_params=_params(("arbitrary",)),
        name="ffn_ple" if ple is not None else "ffn",
    )(*args)


def _gmlp_kernel(streams, xl_ref, xs_ref, wi_ref, bi_ref, lg_ref, lb_ref, wo_ref, g_ref, b_ref,
                 wsl_ref, bsl_ref, wss_ref, bss_ref, ol_ref, os_ref, v_ref, y_ref):
    def body(stream):
        x_ref, ws_ref, bs_ref, o_ref = ((xl_ref, wsl_ref, bsl_ref, ol_ref), (xs_ref, wss_ref, bss_ref, os_ref))[stream]
        hm = streams.tm // 2
        halves = [slice(h * hm, (h + 1) * hm) for h in range(2)]
        xs = [x_ref[rows, :] for rows in halves]
        zs = [_dot(x.astype(BF16), wi_ref[...]) + bi_ref[...] for x in xs]
        us, vbs = [], []
        for rows, z in zip(halves, zs):
            z = 0.5 * z * (1.0 + lax.erf(z * (1.0 / math.sqrt(2.0))))
            v = _layer_norm(z[:, A_INNER:], lg_ref[...], lb_ref[...])
            if stream == 1:
                v_ref[rows, :] = v
            us.append(z[:, :A_INNER])
            vbs.append(v.astype(BF16))
        for half, u, vb in zip(halves, us, vbs):
            for c in range(hm // CHUNK):
                rows = slice(c * CHUNK, (c + 1) * CHUNK)
                out_rows = slice(half.start + c * CHUNK, half.start + (c + 1) * CHUNK)
                for h in range(A_HEADS):
                    cols = slice(h * A_GROUP, (h + 1) * A_GROUP)
                    mixed = _dot(ws_ref[h], vb[rows, cols]) + bs_ref[:, h:h + 1]
                    y_ref[out_rows, cols] = (u[rows, cols] * mixed).astype(BF16)
        for rows, x in zip(halves, xs):
            out = _dot(y_ref[rows, :], wo_ref[...])
            o_ref[rows, :] = _layer_norm(ALPHA * x + out, g_ref[...], b_ref[...])

    streams.per_stream(body)


def _gmlp(x_long, x_short, consts, mix_long, mix_short):
    streams = _TwoStreams(x_long.shape[0], x_short.shape[0])
    consts = consts + list(mix_long) + list(mix_short)
    return pl.pallas_call(
        functools.partial(_gmlp_kernel, streams),
        grid=streams.grid,
        in_specs=streams.row_specs(D_MODEL) + [c.const_spec() for c in consts],
        out_specs=streams.row_specs(D_MODEL) + [streams.row_specs(A_INNER)[1]],
        out_shape=[jax.ShapeDtypeStruct(x_long.shape, F32), jax.ShapeDtypeStruct(x_short.shape, F32),
                   jax.ShapeDtypeStruct((x_short.shape[0], A_INNER), F32)],
        scratch_shapes=[pltpu.VMEM((streams.tm, A_INNER), BF16)],
        compiler_params=_params(("arbitrary",)),
        name="gmlp",
    )(x_long, x_short, *[c.arr for c in consts])


def _conv_tail(x, bg, conv, wo_ref, g_ref, b_ref, o_ref):
    out = _dot((bg * conv).astype(BF16), wo_ref[...])
    o_ref[...] = _layer_norm(ALPHA * x + out, g_ref[...], b_ref[...])


def _conv_long_kernel(tiles_per_seq, x_ref, buf_ref, wi_ref, cw_ref, wo_ref, g_ref, b_ref,
                      o_ref, tail_ref, carry_ref):
    x = x_ref[...]
    tm = x.shape[0]
    h3 = _dot(x.astype(BF16), wi_ref[...])
    bg = h3[:, :D_MODEL]
    z = h3[:, D_MODEL:2 * D_MODEL] * h3[:, 2 * D_MODEL:]

    @pl.when(pl.program_id(0) % tiles_per_seq == 0)
    def _():
        carry_ref[...] = buf_ref[0]

    row = lax.broadcasted_iota(jnp.int32, (tm, D_MODEL), 0)
    prev1 = carry_ref[7:8, :]
    prev2 = carry_ref[6:7, :]
    z1 = jnp.where(row == 0, prev1, pltpu.roll(z, 1, 0))
    z2 = jnp.where(row == 0, prev2, jnp.where(row == 1, prev1, pltpu.roll(z, 2, 0)))
    conv = cw_ref[0:1, :] * z2 + cw_ref[1:2, :] * z1 + cw_ref[2:3, :] * z
    tail = z[tm - 8:, :]
    carry_ref[...] = tail
    tail_ref[0] = tail
    _conv_tail(x, bg, conv, wo_ref, g_ref, b_ref, o_ref)


def _conv_short_kernel(seq, x_ref, h1_ref, h2_ref, wi_ref, cw_ref, wo_ref, g_ref, b_ref,
                       o_ref, z_ref):
    x = x_ref[...]
    tm = x.shape[0]
    h3 = _dot(x.astype(BF16), wi_ref[...])
    bg = h3[:, :D_MODEL]
    z = h3[:, D_MODEL:2 * D_MODEL] * h3[:, 2 * D_MODEL:]
    t = lax.broadcasted_iota(jnp.int32, (tm, D_MODEL), 0) % seq
    z1 = jnp.where(t >= 1, pltpu.roll(z, 1, 0), h1_ref[...])
    z2 = jnp.where(t >= 2, pltpu.roll(z, 2, 0), h2_ref[...])
    conv = cw_ref[0:1, :] * z2 + cw_ref[1:2, :] * z1 + cw_ref[2:3, :] * z
    z_ref[...] = z
    _conv_tail(x, bg, conv, wo_ref, g_ref, b_ref, o_ref)


def _first_rows(state_rows, seq):
    bn = state_rows.shape[0]
    out = jnp.zeros((bn, seq, D_MODEL), F32).at[:, 0, :].set(state_rows)
    return out.reshape(bn * seq, D_MODEL)


def _conv_mixer(x, buf_prev, seq, consts):
    n = x.shape[0]
    bn = n // seq
    tm = min(ROW_TILE, n)
    const_specs = [c.const_spec() for c in consts]
    const_args = [c.arr for c in consts]
    if seq >= tm:
        tiles_per_seq = seq // tm
        buf8 = jnp.concatenate([jnp.zeros((bn, 6, D_MODEL), F32), buf_prev], axis=1)
        out, tails = pl.pallas_call(
            functools.partial(_conv_long_kernel, tiles_per_seq),
            grid=(n // tm,),
            in_specs=[_row_spec(D_MODEL, tm),
                      pl.BlockSpec((1, 8, D_MODEL), lambda i: (i // tiles_per_seq, 0, 0))] + const_specs,
            out_specs=[_row_spec(D_MODEL, tm), pl.BlockSpec((1, 8, D_MODEL), lambda i: (i, 0, 0))],
            out_shape=[jax.ShapeDtypeStruct((n, D_MODEL), F32),
                       jax.ShapeDtypeStruct((n // tm, 8, D_MODEL), F32)],
            scratch_shapes=[pltpu.VMEM((8, D_MODEL), F32)],
            compiler_params=_params(("arbitrary",)),
            name="conv_long",
        )(x, buf8, *const_args)
        new_buf = tails[tiles_per_seq - 1::tiles_per_seq, 6:8, :]
        return out, new_buf
    assert tm % seq == 0 and seq >= CONV_W - 1
    h1 = _first_rows(buf_prev[:, 1, :], seq)
    h2 = _first_rows(buf_prev[:, 0, :], seq) + jnp.roll(h1, 1, axis=0)
    out, z = pl.pallas_call(
        functools.partial(_conv_short_kernel, seq),
        grid=(n // tm,),
        in_specs=[_row_spec(D_MODEL, tm)] * 3 + const_specs,
        out_specs=[_row_spec(D_MODEL, tm)] * 2,
        out_shape=[jax.ShapeDtypeStruct((n, D_MODEL), F32)] * 2,
        compiler_params=_params(("parallel",)),
        name="conv_short",
    )(x, h1, h2, *const_args)
    new_buf = z.reshape(bn, seq, D_MODEL)[:, seq - (CONV_W - 1):, :]
    return out, new_buf


def _head_sum_bcast(x, e_ref, et_ref):
    s = _dot(x.astype(BF16), e_ref[...])
    s_hi, s_lo = _split2(s)
    return _dot(s_hi, et_ref[...]) + _dot(s_lo, et_ref[...])


def _rwkv_proj_body(x, xprev, mu_ref, wr_ref, wk_ref, wv_ref, w0_ref, w1_ref, w2_ref, a0_ref, a1_ref,
                    a2_ref, g1_ref, g2_ref, kk_ref, ka_ref, rk_ref, e_ref, et_ref,
                    r_out, lw_out, k_out, v_out, c_out, b_out, bonus_out, g_out):
    xx = xprev - x
    mix = lambda i: (x + xx * mu_ref[i:i + 1, :]).astype(BF16)
    r = _dot(mix(0), wr_ref[...])
    k = _dot(mix(2), wk_ref[...])
    v = _dot(mix(3), wv_ref[...])
    zw = w0_ref[...] + _dot(jnp.tanh(_dot(mix(1), w1_ref[...])).astype(BF16), w2_ref[...])
    lw_out[...] = -_sigmoid(zw) * math.exp(-0.5)
    a = _sigmoid(a0_ref[...] + _dot(_dot(mix(4), a1_ref[...]).astype(BF16), a2_ref[...]))
    g_out[...] = _dot(_sigmoid(_dot(mix(5), g1_ref[...])).astype(BF16), g2_ref[...])
    kk = k * kk_ref[...]
    norm = jnp.sqrt(_head_sum_bcast(kk * kk, e_ref, et_ref))
    c = kk / jnp.maximum(norm, 1e-12)
    kmod = k * (1.0 + (a - 1.0) * ka_ref[...])
    r_out[...] = r
    k_out[...] = kmod
    v_out[...] = v
    c_out[...] = c
    b_out[...] = c * a
    bonus_out[...] = _head_sum_bcast(r * kmod * rk_ref[...], e_ref, et_ref) * v


def _rwkv_proj_long_kernel(tiles_per_seq, x_ref, shift_ref, *rest):
    carry_ref = rest[-1]
    x = x_ref[...]
    tm = x.shape[0]

    @pl.when(pl.program_id(0) % tiles_per_seq == 0)
    def _():
        carry_ref[...] = shift_ref[...]

    row = lax.broadcasted_iota(jnp.int32, (tm, D_MODEL), 0)
    xprev = jnp.where(row == 0, carry_ref[...], pltpu.roll(x, 1, 0))
    carry_ref[...] = x[tm - 1:tm, :]
    _rwkv_proj_body(x, xprev, *rest[:-1])


def _rwkv_proj_short_kernel(seq, x_ref, h1_ref, *rest):
    x = x_ref[...]
    t = lax.broadcasted_iota(jnp.int32, x.shape, 0) % seq
    xprev = jnp.where(t >= 1, pltpu.roll(x, 1, 0), h1_ref[...])
    _rwkv_proj_body(x, xprev, *rest)


def _rwkv_proj(x, shift_prev, seq, consts):
    n = x.shape[0]
    tm = min(ROW_TILE // 2, n)
    common = dict(
        grid=(n // tm,),
        out_specs=[_row_spec(D_MODEL, tm)] * 8,
        out_shape=[jax.ShapeDtypeStruct((n, D_MODEL), F32)] * 8,
    )
    const_specs = [c.const_spec() for c in consts]
    const_args = [c.arr for c in consts]
    if seq >= tm:
        tiles_per_seq = seq // tm
        return pl.pallas_call(
            functools.partial(_rwkv_proj_long_kernel, tiles_per_seq),
            in_specs=[_row_spec(D_MODEL, tm),
                      pl.BlockSpec((None, 1, D_MODEL), lambda i: (i // tiles_per_seq, 0, 0))] + const_specs,
            scratch_shapes=[pltpu.VMEM((1, D_MODEL), F32)],
            compiler_params=_params(("arbitrary",)),
            name="rwkv_proj_long", **common,
        )(x, shift_prev[:, None, :], *const_args)
    assert tm % seq == 0
    return pl.pallas_call(
        functools.partial(_rwkv_proj_short_kernel, seq),
        in_specs=[_row_spec(D_MODEL, tm)] * 2 + const_specs,
        compiler_params=_params(("parallel",)),
        name="rwkv_proj_short", **common,
    )(x, _first_rows(shift_prev, seq), *const_args)


def _pair_rows(x, lane_lo):
    zero = jnp.zeros_like(x)
    return jnp.concatenate([jnp.where(lane_lo, x, zero), jnp.where(lane_lo, zero, x)], axis=0)


def _block_diag(a, b):
    zero = jnp.zeros_like(a)
    return jnp.concatenate([jnp.concatenate([a, zero], axis=1), jnp.concatenate([zero, b], axis=1)], axis=0)


def _scan_chunk(load, store_y, n_chains, qs_fn, state_fn):
    C = SCAN_CHUNK
    pairs = range(n_chains)
    row = lax.broadcasted_iota(jnp.int32, (C, 2 * C), 0)
    col = lax.broadcasted_iota(jnp.int32, (C, 2 * C), 1) & (C - 1)
    strict = row > col
    incl = row >= col
    lane_lo = lax.broadcasted_iota(jnp.int32, (C, LANES), 1) < HEAD_SIZE
    tri = jnp.where(lax.broadcasted_iota(jnp.int32, (C, C), 0) >= lax.broadcasted_iota(jnp.int32, (C, C), 1),
                    1.0, 0.0).astype(BF16)
    r, lw, k, v, c, b = zip(*[load(p) for p in pairs])

    cum = []
    for p in pairs:
        l_hi, l_mid, l_lo = _split3(lw[p])
        cum.append(_dot(tri, l_hi) + (_dot(tri, l_mid) + _dot(tri, l_lo)))
    p_inc = [jnp.exp(cum[p]) for p in pairs]
    p_inv = [jnp.exp(-cum[p]) for p in pairs]
    p_exc = [jnp.exp(cum[p] - lw[p]) for p in pairs]
    q = [jnp.concatenate([c[p] * p_exc[p], r[p] * p_inc[p]], axis=0).astype(BF16) for p in pairs]
    bt = [(b[p] * p_inv[p]).astype(BF16) for p in pairs]
    kt = [(k[p] * p_inv[p]).astype(BF16) for p in pairs]
    vb = [v[p].astype(BF16) for p in pairs]
    kb = [jnp.concatenate([_pair_rows(bt[p], lane_lo), _pair_rows(kt[p], lane_lo)], axis=0) for p in pairs]
    gram = [_dot(q[p], kb[p], NT) for p in pairs]
    l_cb = [jnp.where(strict, gram[p][:C, :2 * C], 0.0) for p in pairs]
    l_ck = [jnp.where(strict, gram[p][:C, 2 * C:], 0.0).astype(BF16) for p in pairs]
    a_rb = [jnp.where(incl, gram[p][C:, :2 * C], 0.0) for p in pairs]
    a_rk = [jnp.where(incl, gram[p][C:, 2 * C:], 0.0) for p in pairs]
    qs = qs_fn(q)
    v_rows = [_pair_rows(vb[p], lane_lo) for p in pairs]
    u = [-(qs[p][:C] + _dot(l_ck[p], v_rows[p])) for p in pairs]
    m = [l_cb[p].astype(BF16) for p in pairs]
    u = [u[p] - _dot(m[p], _pair_rows(u[p].astype(BF16), lane_lo)) for p in pairs]
    for _ in range(int(math.log2(C)) - 1):
        m = [_dot(m[p], _pair_rows(m[p], lane_lo)).astype(BF16) for p in pairs]
        u = [u[p] + _dot(m[p], _pair_rows(u[p].astype(BF16), lane_lo)) for p in pairs]
    ub = [u[p].astype(BF16) for p in pairs]
    for p in pairs:
        a = jnp.concatenate([a_rb[p], a_rk[p]], axis=1).astype(BF16)
        uv_rows = jnp.concatenate([_pair_rows(ub[p], lane_lo), v_rows[p]], axis=0)
        store_y(p, qs[p][C:] + _dot(a, uv_rows))
    state_fn(u, v, ub, vb, bt, kt, p_inc)


def _pair_lanes(p):
    return slice(p * LANES, (p + 1) * LANES)


def _diag_blocks_mask():
    sq_row = lax.broadcasted_iota(jnp.int32, (LANES, LANES), 0) < HEAD_SIZE
    sq_col = lax.broadcasted_iota(jnp.int32, (LANES, LANES), 1) < HEAD_SIZE
    return sq_row == sq_col


def _store_pair_state(out_ref, i, p, s_pair):
    out_ref[i, 2 * p] = s_pair[:HEAD_SIZE, :HEAD_SIZE]
    out_ref[i, 2 * p + 1] = s_pair[HEAD_SIZE:, HEAD_SIZE:]


SCAN_SEQS = 4


def _rwkv_scan_kernel(r_ref, lw_ref, k_ref, v_ref, c_ref, b_ref, s0_ref, y_ref, sfin_ref, s_ref):
    C = SCAN_CHUNK
    j = pl.program_id(1)
    chains = [(s, p) for s in range(SCAN_SEQS) for p in range(HEAD_PAIRS)]

    @pl.when(j == 0)
    def _():
        for s, p in chains:
            s_ref[s, p] = _block_diag(s0_ref[s, 2 * p], s0_ref[s, 2 * p + 1])

    diag = _diag_blocks_mask()
    refs = (r_ref, lw_ref, k_ref, v_ref, c_ref, b_ref)

    def load(ch):
        s, p = chains[ch]
        return tuple(ref[s, :, _pair_lanes(p)] for ref in refs)

    def store_y(ch, y):
        s, p = chains[ch]
        y_ref[s, :, _pair_lanes(p)] = y

    def qs_fn(q):
        return [_dot(q[ch], s_ref[s, p].astype(BF16), NT) for ch, (s, p) in enumerate(chains)]

    def state_fn(u, v, ub, vb, bt, kt, p_inc):
        for ch, (s, p) in enumerate(chains):
            upd = _dot(jnp.concatenate([ub[ch], vb[ch]], axis=0), jnp.concatenate([bt[ch], kt[ch]], axis=0), TN)
            s_ref[s, p] = (s_ref[s, p] + jnp.where(diag, upd, 0.0)) * p_inc[ch][C - 1:C, :]

    _scan_chunk(load, store_y, len(chains), qs_fn, state_fn)

    @pl.when(j == pl.num_programs(1) - 1)
    def _():
        for s, p in chains:
            _store_pair_state(sfin_ref, s, p, s_ref[s, p])


def _rwkv_scan(r, lw, k, v, c, b, s0, bn, seq):
    assert bn % SCAN_SEQS == 0
    row_spec = pl.BlockSpec((SCAN_SEQS, SCAN_CHUNK, D_MODEL), lambda i, j: (i, j, 0))
    st_spec = pl.BlockSpec((SCAN_SEQS, B_HEADS, HEAD_SIZE, HEAD_SIZE), lambda i, j: (i, 0, 0, 0))
    as_seqs = lambda z: z.reshape(bn, seq, D_MODEL)
    ys, s_new = pl.pallas_call(
        _rwkv_scan_kernel,
        grid=(bn // SCAN_SEQS, seq // SCAN_CHUNK),
        in_specs=[row_spec] * 6 + [st_spec],
        out_specs=[row_spec, st_spec],
        out_shape=[jax.ShapeDtypeStruct((bn, seq, D_MODEL), F32),
                   jax.ShapeDtypeStruct((bn, B_HEADS, HEAD_SIZE, HEAD_SIZE), F32)],
        scratch_shapes=[pltpu.VMEM((SCAN_SEQS, HEAD_PAIRS, LANES, LANES), F32)],
        compiler_params=_params(("parallel", "arbitrary")),
        name="rwkv_scan",
    )(*(as_seqs(z) for z in (r, lw, k, v, c, b)), s0)
    return ys.reshape(bn * seq, D_MODEL), s_new


def _rwkv_scan_lanes_kernel(seq, r_ref, lw_ref, k_ref, v_ref, c_ref, b_ref, s0_ref, y_ref, sfin_ref,
                            cols_ref, yt_ref):
    nb = s0_ref.shape[-1]
    for a, ref in enumerate((r_ref, lw_ref, k_ref, v_ref, c_ref, b_ref)):
        for t in range(seq):
            col = ref[pl.ds(t, nb, stride=seq), :].T
            cols_ref[a, t] = jnp.exp(col) if ref is lw_ref else col
    for hh in range(2):
        feat = slice(hh * HEAD_SIZE, (hh + 1) * HEAD_SIZE)
        for vi in range(HEAD_SIZE):
            row = hh * HEAD_SIZE + vi
            s = s0_ref[hh, vi]
            for t in range(seq):
                r, w, k, c, b = (cols_ref[a, t, feat, :] for a in (0, 1, 2, 4, 5))
                sa = jnp.sum(s * c, axis=0, keepdims=True)
                s = s * w - sa * b + cols_ref[3, t, row:row + 1, :] * k
                yt_ref[t, row:row + 1, :] = jnp.sum(s * r, axis=0, keepdims=True)
            sfin_ref[hh, vi] = s
    for t in range(seq):
        y_ref[pl.ds(t, nb, stride=seq), :] = yt_ref[t].T


def _rwkv_scan_lanes(r, lw, k, v, c, b, s0, bn, seq):
    assert bn == LANES
    n = bn * seq
    row_spec = pl.BlockSpec((n, LANES), lambda p: (0, p))
    st_spec = pl.BlockSpec((2, HEAD_SIZE, HEAD_SIZE, bn), lambda p: (p, 0, 0, 0))
    ys, s_new = pl.pallas_call(
        functools.partial(_rwkv_scan_lanes_kernel, seq),
        grid=(HEAD_PAIRS,),
        in_specs=[row_spec] * 6 + [st_spec],
        out_specs=[row_spec, st_spec],
        out_shape=[jax.ShapeDtypeStruct((n, D_MODEL), F32),
                   jax.ShapeDtypeStruct((B_HEADS, HEAD_SIZE, HEAD_SIZE, bn), F32)],
        scratch_shapes=[pltpu.VMEM((6, seq, LANES, bn), F32), pltpu.VMEM((seq, LANES, bn), F32)],
        compiler_params=_params(("parallel",)),
        name="rwkv_scan_lanes",
    )(r, lw, k, v, c, b, jnp.transpose(s0, (1, 2, 3, 0)))
    return ys, jnp.transpose(s_new, (3, 0, 1, 2))


def _rwkv_post_kernel(x_ref, y_ref, bonus_ref, gate_ref, lg_ref, lb_ref, wo_ref, e_ref, et_ref,
                      g_ref, b_ref, o_ref):
    ys = y_ref[...]
    inv = 1.0 / HEAD_SIZE
    m = _head_sum_bcast(ys, e_ref, et_ref) * inv
    yc = ys - m
    var = _head_sum_bcast(yc * yc, e_ref, et_ref) * inv
    yn = yc * lax.rsqrt(var + GN_EPS) * lg_ref[...] + lb_ref[...]
    out = _dot(((yn + bonus_ref[...]) * gate_ref[...]).astype(BF16), wo_ref[...])
    o_ref[...] = _layer_norm(ALPHA * x_ref[...] + out, g_ref[...], b_ref[...])


def _rwkv_post(x, ys, bonus, gate, consts):
    n = x.shape[0]
    tm = min(ROW_TILE, n)
    return pl.pallas_call(
        _rwkv_post_kernel,
        grid=(n // tm,),
        in_specs=[_row_spec(D_MODEL, tm)] * 4 + [c.const_spec() for c in consts],
        out_specs=_row_spec(D_MODEL, tm),
        out_shape=jax.ShapeDtypeStruct((n, D_MODEL), F32),
        compiler_params=_params(("parallel",)),
        name="rwkv_post",
    )(x, ys, bonus, gate, *[c.arr for c in consts])


def _rwkv_mixer(x, shift_prev, s0, seq, w, g, b):
    n = x.shape[0]
    bn = n // seq
    r, lw, k, v, c, bb, bonus, gate = _rwkv_proj(x, shift_prev, seq, w["proj"])
    scan = _rwkv_scan if seq % SCAN_CHUNK == 0 else _rwkv_scan_lanes
    ys, s_new = scan(r, lw, k, v, c, bb, s0, bn, seq)
    out = _rwkv_post(x, ys, bonus, gate, w["post"] + [g, b])
    return out, x.reshape(bn, seq, D_MODEL)[:, -1, :], s_new


def _run_trunks(x_long, x_short, p_long, p_short, states_long, states_short, W):
    shapes = [x_long.shape, x_short.shape]
    seqs = [s[1] for s in shapes]
    assert seqs[0] >= CHUNK > seqs[1]
    xs = [x_long.reshape(-1, D_MODEL), x_short.reshape(-1, D_MODEL)]
    ps = [p_long.reshape(DEPTH, -1, PLE_DIM), p_short.reshape(DEPTH, -1, PLE_DIM)]
    states = [states_long, states_short]
    new_v, new_wkv, new_shift, new_conv = [], ([], []), ([], []), ([], [])
    ffn_w = [_sel(W["ffn_w_in"][0, 0].astype(BF16)), _sel(W["ffn_w_out"][0, 0].astype(BF16))]

    def mixer_mats(i):
        j, kind = divmod(i, N_MIXERS)
        if kind == 0:
            return [_sel(W["a_w_in"], j), _sel(W["a_w_out"], j)]
        if kind == 1:
            return [_sel(W["b_w_rkv"], j, m) for m in range(3)] + [_sel(W["b_w_o"], j)]
        return [_sel(W["c_w_in"], j), _sel(W["c_w_out"], j)]

    def ffn_pair(xs, ffn_w, i, s, cast_next, ple=None):
        norm = (_sel(W["ln_g"], i, 2 * s), _sel(W["ln_b"], i, 2 * s))
        if ple is not None:
            ple = ((_sel(ps[0], i), _sel(ps[1], i)),) + tuple(ple)
        res = _ffn(xs[0], xs[1], *ffn_w, *norm, ple=ple, cast_next=cast_next)
        return list(res[:2]), [_sel(a) for a in res[2:]]

    for i in range(DEPTH):
        j, kind = divmod(i, N_MIXERS)
        ln_g = lambda s: _sel(W["ln_g"], i, s)
        ln_b = lambda s: _sel(W["ln_b"], i, s)
        second = [_sel(W[name], i, 1) for name in ("ffn_w_in", "ffn_w_out")] + \
                 [_sel(W[name], i) for name in ("ple_w_gate", "ple_w_proj")]
        xs, cast = ffn_pair(xs, ffn_w, i, 0, second + mixer_mats(i))
        ffn_w, ple_w, mats = cast[:2], cast[2:4], cast[4:]
        if kind == 0:
            consts = [mats[0]] + [_sel(W[name], j) for name in ("a_b_in", "a_ln_g", "a_ln_b")] + [mats[1]]
            x_l, x_s, v = _gmlp(*xs, consts + [ln_g(1), ln_b(1)], [_sel(m, j) for m in W["a_mix_long"]],
                                [_sel(m, j) for m in W["a_mix_short"]])
            xs = [x_l, x_s]
            new_v.append(v.reshape(shapes[1][0], seqs[1], A_INNER))
        elif kind == 1:
            proj = W["b_proj"]
            w = dict(proj=[_sel(proj[0], j)] + mats[:3] + [_sel(a, j) for a in proj[1:]] + [_sel(W["e"]), _sel(W["et"])],
                     post=[_sel(a, j) for a in W["b_post"]] + [mats[3], _sel(W["e"]), _sel(W["et"])])
            for t in range(2):
                wkv, shift, _ = states[t]
                xs[t], sh, s = _rwkv_mixer(xs[t], shift[j], wkv[j], seqs[t], w, ln_g(1), ln_b(1))
                new_shift[t].append(sh)
                new_wkv[t].append(s)
        else:
            consts = [mats[0], _sel(W["c_conv_w"], j), mats[1], ln_g(1), ln_b(1)]
            for t in range(2):
                xs[t], buf = _conv_mixer(xs[t], states[t][2][j], seqs[t], consts)
                new_conv[t].append(buf)
        first_next = [_sel(W[name], i + 1, 0) for name in ("ffn_w_in", "ffn_w_out")] if i + 1 < DEPTH else []
        xs, ffn_w = ffn_pair(xs, ffn_w, i, 1, first_next, ple=ple_w)
    outs = [(xs[t].reshape(shapes[t]), jnp.stack(new_wkv[t]), jnp.stack(new_shift[t]), jnp.stack(new_conv[t]))
            for t in range(2)]
    return outs[0], outs[1], jnp.stack(new_v)


def _gmlp_mix_mats(w_s, b_s, seq):
    l = min(seq, CHUNK)
    ws = jnp.where(jnp.tril(jnp.ones((l, l), dtype=bool)), w_s[..., :l, :l], 0.0)
    reps = CHUNK // l
    if reps > 1:
        pos = jnp.arange(CHUNK) // l
        ws = jnp.where(pos[:, None] == pos[None, :], jnp.tile(ws, (1, 1, reps, reps)), 0.0)
    bias = jnp.tile(jnp.swapaxes(b_s[..., :l], -1, -2), (1, reps, 1))
    return ws.astype(BF16), bias


def kernel(x_prompt, x_sample, state_b_wkv, state_b_shift, state_c_conv, p_prompt, p_sample, ln_g, ln_b, ffn_w_in, ffn_w_out, ple_w_gate, ple_w_proj, a_w_in, a_b_in, a_ln_g, a_ln_b, a_w_s, a_b_s, a_w_out, b_mu, b_w_rkv, b_w0, b_w1, b_w2, b_a0, b_a1, b_a2, b_g1, b_g2, b_k_k, b_k_a, b_r_k, b_lnx_g, b_lnx_b, b_w_o, c_w_in, c_conv_w, c_w_out):
    bf = lambda w: w.astype(BF16)
    row = lambda w: w.reshape(w.shape[0], 1, -1)
    head_of_lane = jnp.arange(D_MODEL) // HEAD_SIZE
    e = (head_of_lane[:, None] == jnp.arange(LANES)[None, :]).astype(BF16)
    W = dict(
        ln_g=ln_g[:, :, None, :], ln_b=ln_b[:, :, None, :],
        ffn_w_in=ffn_w_in, ffn_w_out=ffn_w_out,
        ple_w_gate=ple_w_gate, ple_w_proj=ple_w_proj,
        a_w_in=a_w_in, a_b_in=row(a_b_in), a_ln_g=row(a_ln_g), a_ln_b=row(a_ln_b), a_w_out=a_w_out,
        a_mix_long=_gmlp_mix_mats(a_w_s, a_b_s, x_prompt.shape[1]),
        a_mix_short=_gmlp_mix_mats(a_w_s, a_b_s, x_sample.shape[1]),
        c_w_in=c_w_in, c_conv_w=c_conv_w, c_w_out=c_w_out,
        b_w_rkv=b_w_rkv, b_w_o=b_w_o,
        b_proj=[b_mu, row(b_w0), bf(b_w1), bf(b_w2), row(b_a0), bf(b_a1), bf(b_a2),
                bf(b_g1), bf(b_g2), row(b_k_k), row(b_k_a), row(b_r_k)],
        b_post=[row(b_lnx_g), row(b_lnx_b)],
        e=e, et=e.T,
    )
    bp = x_prompt.shape[0]
    n_b = b_mu.shape[0]
    n_c = c_w_in.shape[0]
    zero_wkv = jnp.zeros((n_b, bp) + state_b_wkv.shape[2:], state_b_wkv.dtype)
    zero_shift = jnp.zeros((n_b, bp, D_MODEL), state_b_shift.dtype)
    zero_conv = jnp.zeros((n_c, bp, CONV_W - 1, D_MODEL), state_c_conv.dtype)
    (y_p, wkv_p, shift_p, conv_p), (y_s, wkv_s, shift_s, conv_s), a_v_s = _run_trunks(
        x_prompt, x_sample, p_prompt, p_sample, (zero_wkv, zero_shift, zero_conv),
        (state_b_wkv, state_b_shift, state_c_conv), W)
    return (y_p, y_s, a_v_s, wkv_p, shift_p, conv_p, wkv_s, shift_s, conv_s)
```

```python
import collections
import functools
import math

import jax
import jax.numpy as jnp
from jax import lax
from jax.experimental import pallas as pl
from jax.experimental.pallas import tpu as pltpu

F32 = jnp.float32
BF16 = jnp.bfloat16

D_MODEL = 1024
DEPTH = 4
N_MIXERS = 3
CHUNK = 128
A_INNER = 2 * D_MODEL
A_HEADS = 8
A_GROUP = A_INNER // A_HEADS
HEAD_SIZE = 64
B_HEADS = D_MODEL // HEAD_SIZE
CONV_W = 3
D_FF = 2816
PLE_DIM = 256
ALPHA = (2 * DEPTH) ** 0.25
LN_EPS = 1e-5
GN_EPS = 64e-5

LANES = 128
MXU_DIM = 256
ROW_TILE = 512
SCAN_CHUNK = 64
HEAD_PAIRS = B_HEADS // 2
VMEM_LIMIT = 56 * 1024 * 1024

NN = (((1,), (0,)), ((), ()))
NT = (((1,), (1,)), ((), ()))
TN = (((0,), (0,)), ((), ()))


def _dot(a, b, dims=NN):
    return lax.dot_general(a, b, dims, preferred_element_type=F32)


def _split2(x):
    hi = x.astype(BF16)
    lo = (x - hi.astype(F32)).astype(BF16)
    return hi, lo


def _split3(x):
    hi = x.astype(BF16)
    r1 = x - hi.astype(F32)
    mid = r1.astype(BF16)
    lo = (r1 - mid.astype(F32)).astype(BF16)
    return hi, mid, lo


def _layer_norm(x, g, b, eps=LN_EPS):
    mu = jnp.mean(x, axis=-1, keepdims=True)
    xc = x - mu
    var = jnp.mean(xc * xc, axis=-1, keepdims=True)
    return xc * lax.rsqrt(var + eps) * g + b


def _sigmoid(x):
    return 1.0 / (1.0 + jnp.exp(-x))


class _Sel(collections.namedtuple("_Sel", ["arr", "idx"])):
    def const_spec(self):
        k = len(self.idx)
        rest = self.arr.shape[k:]
        idx = self.idx
        return pl.BlockSpec((None,) * k + rest, lambda *_: idx + (0,) * len(rest),
                            pipeline_mode=pl.Buffered(1))

    def row_spec(self, tm):
        k = len(self.idx)
        idx = self.idx
        return pl.BlockSpec((None,) * k + (tm, self.arr.shape[-1]), lambda i: idx + (i, 0))


def _sel(arr, *idx):
    return _Sel(arr, tuple(idx))


def _row_spec(width, tm=ROW_TILE):
    return pl.BlockSpec((tm, width), lambda i: (i, 0))


def _params(sem):
    return pltpu.CompilerParams(dimension_semantics=sem, vmem_limit_bytes=VMEM_LIMIT)


FF_BLOCKS = ((0, 6 * MXU_DIM), (6 * MXU_DIM, D_FF))
assert D_FF % MXU_DIM == 0


class _TwoStreams:
    def __init__(self, n_long, n_short, tm=ROW_TILE):
        assert n_long % tm == 0 and n_short % tm == 0
        self.tm = tm
        self.long_steps = n_long // tm
        self.grid = (self.long_steps + n_short // tm,)

    def long_step(self, t):
        return jnp.minimum(t, self.long_steps - 1)

    def short_step(self, t):
        return jnp.maximum(t - self.long_steps, 0)

    def row_specs(self, width, idx_long=(), idx_short=()):
        lead_l, lead_s = (None,) * len(idx_long), (None,) * len(idx_short)
        return [pl.BlockSpec(lead_l + (self.tm, width), lambda t: idx_long + (self.long_step(t), 0)),
                pl.BlockSpec(lead_s + (self.tm, width), lambda t: idx_short + (self.short_step(t), 0))]

    def per_stream(self, body):
        is_long = pl.program_id(0) < self.long_steps
        pl.when(is_long)(lambda: body(0))
        pl.when(jnp.logical_not(is_long))(lambda: body(1))


def _ffn_kernel(with_ple, n_cast, streams, xl_ref, xs_ref, wi_ref, wo_ref, g_ref, b_ref, *rest):
    rest = list(rest)
    if with_ple:
        pl_ref, ps_ref, wg_ref, wp_ref = rest[:4]
        del rest[:4]
    cast_in = rest[:n_cast]
    del rest[:n_cast]
    o_refs, cast_out = rest[:2], rest[2:]

    def body(stream):
        x = (xl_ref, xs_ref)[stream][...]
        xb = x.astype(BF16)
        acc = None
        for lo, hi in FF_BLOCKS:
            gate = _dot(xb, wi_ref[:, lo:hi])
            up = _dot(xb, wi_ref[:, D_FF + lo:D_FF + hi])
            act = (gate * _sigmoid(gate) * up).astype(BF16)
            part = _dot(act, wo_ref[lo:hi, :])
            acc = part if acc is None else acc + part
        y = _layer_norm(ALPHA * x + 0.5 * acc, g_ref[...], b_ref[...])
        if with_ple:
            gate = _sigmoid(_dot(y.astype(BF16), wg_ref[...]))
            y = y + gate * _dot((pl_ref, ps_ref)[stream][...].astype(BF16), wp_ref[...])
        o_refs[stream][...] = y
        if stream == 0:
            for src_ref, dst_ref in zip(cast_in, cast_out):
                dst_ref[...] = src_ref[...].astype(BF16)

    streams.per_stream(body)


def _cast_slab_rows(rows, steps):
    for r in range(16, rows + 1, 16):
        if rows % r == 0 and rows // r <= steps:
            return r
    raise ValueError((rows, steps))


def _ffn(x_long, x_short, wi, wo, g, b, ple=None, cast_next=()):
    streams = _TwoStreams(x_long.shape[0], x_short.shape[0])
    consts = [wi, wo, g, b]
    args = [x_long, x_short] + [c.arr for c in consts]
    specs = streams.row_specs(D_MODEL) + [c.const_spec() for c in consts]
    out_specs = streams.row_specs(D_MODEL)
    out_shape = [jax.ShapeDtypeStruct(x_long.shape, F32), jax.ShapeDtypeStruct(x_short.shape, F32)]
    if ple is not None:
        (p_long, p_short), wg, wp = ple
        args += [p_long.arr, p_short.arr, wg.arr, wp.arr]
        specs += streams.row_specs(PLE_DIM, p_long.idx, p_short.idx) + [wg.const_spec(), wp.const_spec()]
    cast_specs = []
    for w in cast_next:
        rows, cols = w.arr.shape[len(w.idx):]
        slab = _cast_slab_rows(rows, streams.long_steps)
        spec = lambda lead, idx, last=rows // slab - 1, slab=slab, cols=cols: pl.BlockSpec(
            (None,) * lead + (slab, cols), lambda t: idx + (jnp.minimum(t, last), 0))
        args.append(w.arr)
        specs.append(spec(len(w.idx), w.idx))
        cast_specs.append(spec(0, ()))
        out_shape.append(jax.ShapeDtypeStruct((rows, cols), BF16))
    out_specs += cast_specs
    return pl.pallas_call(
        functools.partial(_ffn_kernel, ple is not None, len(cast_next), streams),
        grid=streams.grid,
        in_specs=specs,
        out_specs=out_specs,
        out_shape=out_shape,
        compiler_params=_params(("arbitrary",)),
        name="ffn_ple" if ple is not None else "ffn",
    )(*args)


def _gmlp_kernel(streams, xl_ref, xs_ref, wi_ref, bi_ref, lg_ref, lb_ref, wo_ref, g_ref, b_ref,
                 wsl_ref, bsl_ref, wss_ref, bss_ref, ol_ref, os_ref, v_ref, y_ref):
    def body(stream):
        x_ref, ws_ref, bs_ref, o_ref = ((xl_ref, wsl_ref, bsl_ref, ol_ref), (xs_ref, wss_ref, bss_ref, os_ref))[stream]
        hm = streams.tm // 2
        halves = [slice(h * hm, (h + 1) * hm) for h in range(2)]
        xs = [x_ref[rows, :] for rows in halves]
        zs = [_dot(x.astype(BF16), wi_ref[...]) + bi_ref[...] for x in xs]
        us, vbs = [], []
        for rows, z in zip(halves, zs):
            z = 0.5 * z * (1.0 + lax.erf(z * (1.0 / math.sqrt(2.0))))
            v = _layer_norm(z[:, A_INNER:], lg_ref[...], lb_ref[...])
            if stream == 1:
                v_ref[rows, :] = v
            us.append(z[:, :A_INNER])
            vbs.append(v.astype(BF16))
        for half, u, vb in zip(halves, us, vbs):
            for c in range(hm // CHUNK):
                rows = slice(c * CHUNK, (c + 1) * CHUNK)
                out_rows = slice(half.start + c * CHUNK, half.start + (c + 1) * CHUNK)
                for h in range(A_HEADS):
                    cols = slice(h * A_GROUP, (h + 1) * A_GROUP)
                    mixed = _dot(ws_ref[h], vb[rows, cols]) + bs_ref[:, h:h + 1]
                    y_ref[out_rows, cols] = (u[rows, cols] * mixed).astype(BF16)
        for rows, x in zip(halves, xs):
            out = _dot(y_ref[rows, :], wo_ref[...])
            o_ref[rows, :] = _layer_norm(ALPHA * x + out, g_ref[...], b_ref[...])

    streams.per_stream(body)


def _gmlp(x_long, x_short, consts, mix_long, mix_short):
    streams = _TwoStreams(x_long.shape[0], x_short.shape[0])
    consts = consts + list(mix_long) + list(mix_short)
    return pl.pallas_call(
        functools.partial(_gmlp_kernel, streams),
        grid=streams.grid,
        in_specs=streams.row_specs(D_MODEL) + [c.const_spec() for c in consts],
        out_specs=streams.row_specs(D_MODEL) + [streams.row_specs(A_INNER)[1]],
        out_shape=[jax.ShapeDtypeStruct(x_long.shape, F32), jax.ShapeDtypeStruct(x_short.shape, F32),
                   jax.ShapeDtypeStruct((x_short.shape[0], A_INNER), F32)],
        scratch_shapes=[pltpu.VMEM((streams.tm, A_INNER), BF16)],
        compiler_params=_params(("arbitrary",)),
        name="gmlp",
    )(x_long, x_short, *[c.arr for c in consts])


def _conv_tail(x, bg, conv, wo_ref, g_ref, b_ref, o_ref):
    out = _dot((bg * conv).astype(BF16), wo_ref[...])
    o_ref[...] = _layer_norm(ALPHA * x + out, g_ref[...], b_ref[...])


def _conv_long_kernel(tiles_per_seq, x_ref, buf_ref, wi_ref, cw_ref, wo_ref, g_ref, b_ref,
                      o_ref, tail_ref, carry_ref):
    x = x_ref[...]
    tm = x.shape[0]
    h3 = _dot(x.astype(BF16), wi_ref[...])
    bg = h3[:, :D_MODEL]
    z = h3[:, D_MODEL:2 * D_MODEL] * h3[:, 2 * D_MODEL:]

    @pl.when(pl.program_id(0) % tiles_per_seq == 0)
    def _():
        carry_ref[...] = buf_ref[0]

    row = lax.broadcasted_iota(jnp.int32, (tm, D_MODEL), 0)
    prev1 = carry_ref[7:8, :]
    prev2 = carry_ref[6:7, :]
    z1 = jnp.where(row == 0, prev1, pltpu.roll(z, 1, 0))
    z2 = jnp.where(row == 0, prev2, jnp.where(row == 1, prev1, pltpu.roll(z, 2, 0)))
    conv = cw_ref[0:1, :] * z2 + cw_ref[1:2, :] * z1 + cw_ref[2:3, :] * z
    tail = z[tm - 8:, :]
    carry_ref[...] = tail
    tail_ref[0] = tail
    _conv_tail(x, bg, conv, wo_ref, g_ref, b_ref, o_ref)


def _conv_short_kernel(seq, x_ref, h1_ref, h2_ref, wi_ref, cw_ref, wo_ref, g_ref, b_ref,
                       o_ref, z_ref):
    x = x_ref[...]
    tm = x.shape[0]
    h3 = _dot(x.astype(BF16), wi_ref[...])
    bg = h3[:, :D_MODEL]
    z = h3[:, D_MODEL:2 * D_MODEL] * h3[:, 2 * D_MODEL:]
    t = lax.broadcasted_iota(jnp.int32, (tm, D_MODEL), 0) % seq
    z1 = jnp.where(t >= 1, pltpu.roll(z, 1, 0), h1_ref[...])
    z2 = jnp.where(t >= 2, pltpu.roll(z, 2, 0), h2_ref[...])
    conv = cw_ref[0:1, :] * z2 + cw_ref[1:2, :] * z1 + cw_ref[2:3, :] * z
    z_ref[...] = z
    _conv_tail(x, bg, conv, wo_ref, g_ref, b_ref, o_ref)


def _first_rows(state_rows, seq):
    bn = state_rows.shape[0]
    out = jnp.zeros((bn, seq, D_MODEL), F32).at[:, 0, :].set(state_rows)
    return out.reshape(bn * seq, D_MODEL)


def _conv_mixer(x, buf_prev, seq, consts):
    n = x.shape[0]
    bn = n // seq
    tm = min(ROW_TILE, n)
    const_specs = [c.const_spec() for c in consts]
    const_args = [c.arr for c in consts]
    if seq >= tm:
        tiles_per_seq = seq // tm
        buf8 = jnp.concatenate([jnp.zeros((bn, 6, D_MODEL), F32), buf_prev], axis=1)
        out, tails = pl.pallas_call(
            functools.partial(_conv_long_kernel, tiles_per_seq),
            grid=(n // tm,),
            in_specs=[_row_spec(D_MODEL, tm),
                      pl.BlockSpec((1, 8, D_MODEL), lambda i: (i // tiles_per_seq, 0, 0))] + const_specs,
            out_specs=[_row_spec(D_MODEL, tm), pl.BlockSpec((1, 8, D_MODEL), lambda i: (i, 0, 0))],
            out_shape=[jax.ShapeDtypeStruct((n, D_MODEL), F32),
                       jax.ShapeDtypeStruct((n // tm, 8, D_MODEL), F32)],
            scratch_shapes=[pltpu.VMEM((8, D_MODEL), F32)],
            compiler_params=_params(("arbitrary",)),
            name="conv_long",
        )(x, buf8, *const_args)
        new_buf = tails[tiles_per_seq - 1::tiles_per_seq, 6:8, :]
        return out, new_buf
    assert tm % seq == 0 and seq >= CONV_W - 1
    h1 = _first_rows(buf_prev[:, 1, :], seq)
    h2 = _first_rows(buf_prev[:, 0, :], seq) + jnp.roll(h1, 1, axis=0)
    out, z = pl.pallas_call(
        functools.partial(_conv_short_kernel, seq),
        grid=(n // tm,),
        in_specs=[_row_spec(D_MODEL, tm)] * 3 + const_specs,
        out_specs=[_row_spec(D_MODEL, tm)] * 2,
        out_shape=[jax.ShapeDtypeStruct((n, D_MODEL), F32)] * 2,
        compiler_params=_params(("parallel",)),
        name="conv_short",
    )(x, h1, h2, *const_args)
    new_buf = z.reshape(bn, seq, D_MODEL)[:, seq - (CONV_W - 1):, :]
    return out, new_buf


def _head_sum_bcast(x, e_ref, et_ref):
    s = _dot(x.astype(BF16), e_ref[...])
    s_hi, s_lo = _split2(s)
    return _dot(s_hi, et_ref[...]) + _dot(s_lo, et_ref[...])


def _rwkv_proj_body(x, xprev, mu_ref, wr_ref, wk_ref, wv_ref, w0_ref, w1_ref, w2_ref, a0_ref, a1_ref,
                    a2_ref, g1_ref, g2_ref, kk_ref, ka_ref, rk_ref, e_ref, et_ref,
                    r_out, lw_out, k_out, v_out, c_out, b_out, bonus_out, g_out):
    xx = xprev - x
    mix = lambda i: (x + xx * mu_ref[i:i + 1, :]).astype(BF16)
    r = _dot(mix(0), wr_ref[...])
    k = _dot(mix(2), wk_ref[...])
    v = _dot(mix(3), wv_ref[...])
    zw = w0_ref[...] + _dot(jnp.tanh(_dot(mix(1), w1_ref[...])).astype(BF16), w2_ref[...])
    lw_out[...] = -_sigmoid(zw) * math.exp(-0.5)
    a = _sigmoid(a0_ref[...] + _dot(_dot(mix(4), a1_ref[...]).astype(BF16), a2_ref[...]))
    g_out[...] = _dot(_sigmoid(_dot(mix(5), g1_ref[...])).astype(BF16), g2_ref[...])
    kk = k * kk_ref[...]
    norm = jnp.sqrt(_head_sum_bcast(kk * kk, e_ref, et_ref))
    c = kk / jnp.maximum(norm, 1e-12)
    kmod = k * (1.0 + (a - 1.0) * ka_ref[...])
    r_out[...] = r
    k_out[...] = kmod
    v_out[...] = v
    c_out[...] = c
    b_out[...] = c * a
    bonus_out[...] = _head_sum_bcast(r * kmod * rk_ref[...], e_ref, et_ref) * v


def _rwkv_proj_long_kernel(tiles_per_seq, x_ref, shift_ref, *rest):
    carry_ref = rest[-1]
    x = x_ref[...]
    tm = x.shape[0]

    @pl.when(pl.program_id(0) % tiles_per_seq == 0)
    def _():
        carry_ref[...] = shift_ref[...]

    row = lax.broadcasted_iota(jnp.int32, (tm, D_MODEL), 0)
    xprev = jnp.where(row == 0, carry_ref[...], pltpu.roll(x, 1, 0))
    carry_ref[...] = x[tm - 1:tm, :]
    _rwkv_proj_body(x, xprev, *rest[:-1])


def _rwkv_proj_short_kernel(seq, x_ref, h1_ref, *rest):
    x = x_ref[...]
    t = lax.broadcasted_iota(jnp.int32, x.shape, 0) % seq
    xprev = jnp.where(t >= 1, pltpu.roll(x, 1, 0), h1_ref[...])
    _rwkv_proj_body(x, xprev, *rest)


def _rwkv_proj(x, shift_prev, seq, consts):
    n = x.shape[0]
    tm = min(ROW_TILE // 2, n)
    common = dict(
        grid=(n // tm,),
        out_specs=[_row_spec(D_MODEL, tm)] * 8,
        out_shape=[jax.ShapeDtypeStruct((n, D_MODEL), F32)] * 8,
    )
    const_specs = [c.const_spec() for c in consts]
    const_args = [c.arr for c in consts]
    if seq >= tm:
        tiles_per_seq = seq // tm
        return pl.pallas_call(
            functools.partial(_rwkv_proj_long_kernel, tiles_per_seq),
            in_specs=[_row_spec(D_MODEL, tm),
                      pl.BlockSpec((None, 1, D_MODEL), lambda i: (i // tiles_per_seq, 0, 0))] + const_specs,
            scratch_shapes=[pltpu.VMEM((1, D_MODEL), F32)],
            compiler_params=_params(("arbitrary",)),
            name="rwkv_proj_long", **common,
        )(x, shift_prev[:, None, :], *const_args)
    assert tm % seq == 0
    return pl.pallas_call(
        functools.partial(_rwkv_proj_short_kernel, seq),
        in_specs=[_row_spec(D_MODEL, tm)] * 2 + const_specs,
        compiler_params=_params(("parallel",)),
        name="rwkv_proj_short", **common,
    )(x, _first_rows(shift_prev, seq), *const_args)


def _pair_rows(x, lane_lo):
    zero = jnp.zeros_like(x)
    return jnp.concatenate([jnp.where(lane_lo, x, zero), jnp.where(lane_lo, zero, x)], axis=0)


def _block_diag(a, b):
    zero = jnp.zeros_like(a)
    return jnp.concatenate([jnp.concatenate([a, zero], axis=1), jnp.concatenate([zero, b], axis=1)], axis=0)


def _scan_chunk(load, store_y, states, state_fn):
    C = SCAN_CHUNK
    pairs = range(len(states))
    row = lax.broadcasted_iota(jnp.int32, (C, 2 * C), 0)
    col = lax.broadcasted_iota(jnp.int32, (C, 2 * C), 1) & (C - 1)
    strict = row > col
    incl = row >= col
    lane_lo = lax.broadcasted_iota(jnp.int32, (C, LANES), 1) < HEAD_SIZE
    tri = jnp.where(lax.broadcasted_iota(jnp.int32, (C, C), 0) >= lax.broadcasted_iota(jnp.int32, (C, C), 1),
                    1.0, 0.0).astype(BF16)
    r, lw, k, v, c, b = zip(*[load(p) for p in pairs])

    cum = []
    for p in pairs:
        parts = _dot(tri, jnp.concatenate(_split3(lw[p]), axis=1))
        cum.append(parts[:, :LANES] + (parts[:, LANES:2 * LANES] + parts[:, 2 * LANES:]))
    p_inc = [jnp.exp(cum[p]) for p in pairs]
    p_inv = [jnp.exp(-cum[p]) for p in pairs]
    p_exc = [jnp.exp(cum[p] - lw[p]) for p in pairs]
    q = [jnp.concatenate([c[p] * p_exc[p], r[p] * p_inc[p]], axis=0).astype(BF16) for p in pairs]
    bt = [(b[p] * p_inv[p]).astype(BF16) for p in pairs]
    kt = [(k[p] * p_inv[p]).astype(BF16) for p in pairs]
    vb = [v[p].astype(BF16) for p in pairs]
    kb = [jnp.concatenate([_pair_rows(bt[p], lane_lo), _pair_rows(kt[p], lane_lo)], axis=0) for p in pairs]
    gram = [_dot(q[p], jnp.concatenate([kb[p], states[p]], axis=0), NT) for p in pairs]
    l_cb = [jnp.where(strict, gram[p][:C, :2 * C], 0.0) for p in pairs]
    l_ck = [jnp.where(strict, gram[p][:C, 2 * C:4 * C], 0.0).astype(BF16) for p in pairs]
    a_rb = [jnp.where(incl, gram[p][C:, :2 * C], 0.0) for p in pairs]
    a_rk = [jnp.where(incl, gram[p][C:, 2 * C:4 * C], 0.0) for p in pairs]
    qs = [gram[p][:, 4 * C:] for p in pairs]
    v_rows = [_pair_rows(vb[p], lane_lo) for p in pairs]
    u = [-(qs[p][:C] + _dot(l_ck[p], v_rows[p])) for p in pairs]
    m = [l_cb[p].astype(BF16) for p in pairs]
    for level in range(int(math.log2(C)) - 1):
        both = [_dot(m[p], jnp.concatenate([_pair_rows(m[p], lane_lo), _pair_rows(u[p].astype(BF16), lane_lo)],
                                           axis=1)) for p in pairs]
        if level == 0:
            u = [u[p] - both[p][:, 2 * C:] for p in pairs]
        else:
            u = [u[p] + both[p][:, 2 * C:] for p in pairs]
        m = [both[p][:, :2 * C].astype(BF16) for p in pairs]
    u = [u[p] + _dot(m[p], _pair_rows(u[p].astype(BF16), lane_lo)) for p in pairs]
    ub = [u[p].astype(BF16) for p in pairs]
    for p in pairs:
        a = jnp.concatenate([a_rb[p], a_rk[p]], axis=1).astype(BF16)
        uv_rows = jnp.concatenate([_pair_rows(ub[p], lane_lo), v_rows[p]], axis=0)
        store_y(p, qs[p][C:] + _dot(a, uv_rows))
    state_fn(u, v, ub, vb, bt, kt, p_inc)


def _pair_lanes(p):
    return slice(p * LANES, (p + 1) * LANES)


def _diag_blocks_mask():
    sq_row = lax.broadcasted_iota(jnp.int32, (LANES, LANES), 0) < HEAD_SIZE
    sq_col = lax.broadcasted_iota(jnp.int32, (LANES, LANES), 1) < HEAD_SIZE
    return sq_row == sq_col


def _store_pair_state(out_ref, i, p, s_pair):
    out_ref[i, 2 * p] = s_pair[:HEAD_SIZE, :HEAD_SIZE]
    out_ref[i, 2 * p + 1] = s_pair[HEAD_SIZE:, HEAD_SIZE:]


SCAN_SEQS = 4


def _rwkv_scan_kernel(r_ref, lw_ref, k_ref, v_ref, c_ref, b_ref, s0_ref, y_ref, sfin_ref, s_ref):
    C = SCAN_CHUNK
    j = pl.program_id(1)
    chains = [(s, p) for s in range(SCAN_SEQS) for p in range(HEAD_PAIRS)]

    @pl.when(j == 0)
    def _():
        for s, p in chains:
            s_ref[s, p] = _block_diag(s0_ref[s, 2 * p], s0_ref[s, 2 * p + 1])

    diag = _diag_blocks_mask()
    refs = (r_ref, lw_ref, k_ref, v_ref, c_ref, b_ref)

    def load(ch):
        s, p = chains[ch]
        return tuple(ref[s, :, _pair_lanes(p)] for ref in refs)

    def store_y(ch, y):
        s, p = chains[ch]
        y_ref[s, :, _pair_lanes(p)] = y

    def state_fn(u, v, ub, vb, bt, kt, p_inc):
        for ch, (s, p) in enumerate(chains):
            upd = _dot(jnp.concatenate([ub[ch], vb[ch]], axis=0), jnp.concatenate([bt[ch], kt[ch]], axis=0), TN)
            s_ref[s, p] = (s_ref[s, p] + jnp.where(diag, upd, 0.0)) * p_inc[ch][C - 1:C, :]

    _scan_chunk(load, store_y, [s_ref[s, p].astype(BF16) for s, p in chains], state_fn)

    @pl.when(j == pl.num_programs(1) - 1)
    def _():
        for s, p in chains:
            _store_pair_state(sfin_ref, s, p, s_ref[s, p])


def _rwkv_scan(r, lw, k, v, c, b, s0, bn, seq):
    assert bn % SCAN_SEQS == 0
    row_spec = pl.BlockSpec((SCAN_SEQS, SCAN_CHUNK, D_MODEL), lambda i, j: (i, j, 0))
    st_spec = pl.BlockSpec((SCAN_SEQS, B_HEADS, HEAD_SIZE, HEAD_SIZE), lambda i, j: (i, 0, 0, 0))
    as_seqs = lambda z: z.reshape(bn, seq, D_MODEL)
    ys, s_new = pl.pallas_call(
        _rwkv_scan_kernel,
        grid=(bn // SCAN_SEQS, seq // SCAN_CHUNK),
        in_specs=[row_spec] * 6 + [st_spec],
        out_specs=[row_spec, st_spec],
        out_shape=[jax.ShapeDtypeStruct((bn, seq, D_MODEL), F32),
                   jax.ShapeDtypeStruct((bn, B_HEADS, HEAD_SIZE, HEAD_SIZE), F32)],
        scratch_shapes=[pltpu.VMEM((SCAN_SEQS, HEAD_PAIRS, LANES, LANES), F32)],
        compiler_params=_params(("parallel", "arbitrary")),
        name="rwkv_scan",
    )(*(as_seqs(z) for z in (r, lw, k, v, c, b)), s0)
    return ys.reshape(bn * seq, D_MODEL), s_new


def _rwkv_scan_lanes_kernel(seq, r_ref, lw_ref, k_ref, v_ref, c_ref, b_ref, s0_ref, y_ref, sfin_ref,
                            cols_ref, yt_ref):
    nb = s0_ref.shape[-1]
    for a, ref in enumerate((r_ref, lw_ref, k_ref, v_ref, c_ref, b_ref)):
        for t in range(seq):
            col = ref[pl.ds(t, nb, stride=seq), :].T
            cols_ref[a, t] = jnp.exp(col) if ref is lw_ref else col
    for hh in range(2):
        feat = slice(hh * HEAD_SIZE, (hh + 1) * HEAD_SIZE)
        for vi in range(HEAD_SIZE):
            row = hh * HEAD_SIZE + vi
            s = s0_ref[hh, vi]
            for t in range(seq):
                r, w, k, c, b = (cols_ref[a, t, feat, :] for a in (0, 1, 2, 4, 5))
                sa = jnp.sum(s * c, axis=0, keepdims=True)
                s = s * w - sa * b + cols_ref[3, t, row:row + 1, :] * k
                yt_ref[t, row:row + 1, :] = jnp.sum(s * r, axis=0, keepdims=True)
            sfin_ref[hh, vi] = s
    for t in range(seq):
        y_ref[pl.ds(t, nb, stride=seq), :] = yt_ref[t].T


def _rwkv_scan_lanes(r, lw, k, v, c, b, s0, bn, seq):
    assert bn == LANES
    n = bn * seq
    row_spec = pl.BlockSpec((n, LANES), lambda p: (0, p))
    st_spec = pl.BlockSpec((2, HEAD_SIZE, HEAD_SIZE, bn), lambda p: (p, 0, 0, 0))
    ys, s_new = pl.pallas_call(
        functools.partial(_rwkv_scan_lanes_kernel, seq),
        grid=(HEAD_PAIRS,),
        in_specs=[row_spec] * 6 + [st_spec],
        out_specs=[row_spec, st_spec],
        out_shape=[jax.ShapeDtypeStruct((n, D_MODEL), F32),
                   jax.ShapeDtypeStruct((B_HEADS, HEAD_SIZE, HEAD_SIZE, bn), F32)],
        scratch_shapes=[pltpu.VMEM((6, seq, LANES, bn), F32), pltpu.VMEM((seq, LANES, bn), F32)],
        compiler_params=_params(("parallel",)),
        name="rwkv_scan_lanes",
    )(r, lw, k, v, c, b, jnp.transpose(s0, (1, 2, 3, 0)))
    return ys, jnp.transpose(s_new, (3, 0, 1, 2))


def _rwkv_post_kernel(x_ref, y_ref, bonus_ref, gate_ref, lg_ref, lb_ref, wo_ref, e_ref, et_ref,
                      g_ref, b_ref, o_ref):
    ys = y_ref[...]
    inv = 1.0 / HEAD_SIZE
    m = _head_sum_bcast(ys, e_ref, et_ref) * inv
    yc = ys - m
    var = _head_sum_bcast(yc * yc, e_ref, et_ref) * inv
    yn = yc * lax.rsqrt(var + GN_EPS) * lg_ref[...] + lb_ref[...]
    out = _dot(((yn + bonus_ref[...]) * gate_ref[...]).astype(BF16), wo_ref[...])
    o_ref[...] = _layer_norm(ALPHA * x_ref[...] + out, g_ref[...], b_ref[...])


def _rwkv_post(x, ys, bonus, gate, consts):
    n = x.shape[0]
    tm = min(ROW_TILE, n)
    return pl.pallas_call(
        _rwkv_post_kernel,
        grid=(n // tm,),
        in_specs=[_row_spec(D_MODEL, tm)] * 4 + [c.const_spec() for c in consts],
        out_specs=_row_spec(D_MODEL, tm),
        out_shape=jax.ShapeDtypeStruct((n, D_MODEL), F32),
        compiler_params=_params(("parallel",)),
        name="rwkv_post",
    )(x, ys, bonus, gate, *[c.arr for c in consts])


def _rwkv_mixer(x, shift_prev, s0, seq, w, g, b):
    n = x.shape[0]
    bn = n // seq
    r, lw, k, v, c, bb, bonus, gate = _rwkv_proj(x, shift_prev, seq, w["proj"])
    scan = _rwkv_scan if seq % SCAN_CHUNK == 0 else _rwkv_scan_lanes
    ys, s_new = scan(r, lw, k, v, c, bb, s0, bn, seq)
    out = _rwkv_post(x, ys, bonus, gate, w["post"] + [g, b])
    return out, x.reshape(bn, seq, D_MODEL)[:, -1, :], s_new


def _run_trunks(x_long, x_short, p_long, p_short, states_long, states_short, W):
    shapes = [x_long.shape, x_short.shape]
    seqs = [s[1] for s in shapes]
    assert seqs[0] >= CHUNK > seqs[1]
    xs = [x_long.reshape(-1, D_MODEL), x_short.reshape(-1, D_MODEL)]
    ps = [p_long.reshape(DEPTH, -1, PLE_DIM), p_short.reshape(DEPTH, -1, PLE_DIM)]
    states = [states_long, states_short]
    new_v, new_wkv, new_shift, new_conv = [], ([], []), ([], []), ([], [])
    ffn_w = [_sel(W["ffn_w_in"][0, 0].astype(BF16)), _sel(W["ffn_w_out"][0, 0].astype(BF16))]

    def mixer_mats(i):
        j, kind = divmod(i, N_MIXERS)
        if kind == 0:
            return [_sel(W["a_w_in"], j), _sel(W["a_w_out"], j)]
        if kind == 1:
            return [_sel(W["b_w_rkv"], j, m) for m in range(3)] + [_sel(W["b_w_o"], j)]
        return [_sel(W["c_w_in"], j), _sel(W["c_w_out"], j)]

    def ffn_pair(xs, ffn_w, i, s, cast_next, ple=None):
        norm = (_sel(W["ln_g"], i, 2 * s), _sel(W["ln_b"], i, 2 * s))
        if ple is not None:
            ple = ((_sel(ps[0], i), _sel(ps[1], i)),) + tuple(ple)
        res = _ffn(xs[0], xs[1], *ffn_w, *norm, ple=ple, cast_next=cast_next)
        return list(res[:2]), [_sel(a) for a in res[2:]]

    for i in range(DEPTH):
        j, kind = divmod(i, N_MIXERS)
        ln_g = lambda s: _sel(W["ln_g"], i, s)
        ln_b = lambda s: _sel(W["ln_b"], i, s)
        second = [_sel(W[name], i, 1) for name in ("ffn_w_in", "ffn_w_out")] + \
                 [_sel(W[name], i) for name in ("ple_w_gate", "ple_w_proj")]
        xs, cast = ffn_pair(xs, ffn_w, i, 0, second + mixer_mats(i))
        ffn_w, ple_w, mats = cast[:2], cast[2:4], cast[4:]
        if kind == 0:
            consts = [mats[0]] + [_sel(W[name], j) for name in ("a_b_in", "a_ln_g", "a_ln_b")] + [mats[1]]
            x_l, x_s, v = _gmlp(*xs, consts + [ln_g(1), ln_b(1)], [_sel(m, j) for m in W["a_mix_long"]],
                                [_sel(m, j) for m in W["a_mix_short"]])
            xs = [x_l, x_s]
            new_v.append(v.reshape(shapes[1][0], seqs[1], A_INNER))
        elif kind == 1:
            proj = W["b_proj"]
            w = dict(proj=[_sel(proj[0], j)] + mats[:3] + [_sel(a, j) for a in proj[1:]] + [_sel(W["e"]), _sel(W["et"])],
                     post=[_sel(a, j) for a in W["b_post"]] + [mats[3], _sel(W["e"]), _sel(W["et"])])
            for t in range(2):
                wkv, shift, _ = states[t]
                xs[t], sh, s = _rwkv_mixer(xs[t], shift[j], wkv[j], seqs[t], w, ln_g(1), ln_b(1))
                new_shift[t].append(sh)
                new_wkv[t].append(s)
        else:
            consts = [mats[0], _sel(W["c_conv_w"], j), mats[1], ln_g(1), ln_b(1)]
            for t in range(2):
                xs[t], buf = _conv_mixer(xs[t], states[t][2][j], seqs[t], consts)
                new_conv[t].append(buf)
        first_next = [_sel(W[name], i + 1, 0) for name in ("ffn_w_in", "ffn_w_out")] if i + 1 < DEPTH else []
        xs, ffn_w = ffn_pair(xs, ffn_w, i, 1, first_next, ple=ple_w)
    outs = [(xs[t].reshape(shapes[t]), jnp.stack(new_wkv[t]), jnp.stack(new_shift[t]), jnp.stack(new_conv[t]))
            for t in range(2)]
    return outs[0], outs[1], jnp.stack(new_v)


def _gmlp_mix_mats(w_s, b_s, seq):
    l = min(seq, CHUNK)
    ws = jnp.where(jnp.tril(jnp.ones((l, l), dtype=bool)), w_s[..., :l, :l], 0.0)
    reps = CHUNK // l
    if reps > 1:
        pos = jnp.arange(CHUNK) // l
        ws = jnp.where(pos[:, None] == pos[None, :], jnp.tile(ws, (1, 1, reps, reps)), 0.0)
    bias = jnp.tile(jnp.swapaxes(b_s[..., :l], -1, -2), (1, reps, 1))
    return ws.astype(BF16), bias


def kernel(x_prompt, x_sample, state_b_wkv, state_b_shift, state_c_conv, p_prompt, p_sample, ln_g, ln_b, ffn_w_in, ffn_w_out, ple_w_gate, ple_w_proj, a_w_in, a_b_in, a_ln_g, a_ln_b, a_w_s, a_b_s, a_w_out, b_mu, b_w_rkv, b_w0, b_w1, b_w2, b_a0, b_a1, b_a2, b_g1, b_g2, b_k_k, b_k_a, b_r_k, b_lnx_g, b_lnx_b, b_w_o, c_w_in, c_conv_w, c_w_out):
    bf = lambda w: w.astype(BF16)
    row = lambda w: w.reshape(w.shape[0], 1, -1)
    head_of_lane = jnp.arange(D_MODEL) // HEAD_SIZE
    e = (head_of_lane[:, None] == jnp.arange(LANES)[None, :]).astype(BF16)
    W = dict(
        ln_g=ln_g[:, :, None, :], ln_b=ln_b[:, :, None, :],
        ffn_w_in=ffn_w_in, ffn_w_out=ffn_w_out,
        ple_w_gate=ple_w_gate, ple_w_proj=ple_w_proj,
        a_w_in=a_w_in, a_b_in=row(a_b_in), a_ln_g=row(a_ln_g), a_ln_b=row(a_ln_b), a_w_out=a_w_out,
        a_mix_long=_gmlp_mix_mats(a_w_s, a_b_s, x_prompt.shape[1]),
        a_mix_short=_gmlp_mix_mats(a_w_s, a_b_s, x_sample.shape[1]),
        c_w_in=c_w_in, c_conv_w=c_conv_w, c_w_out=c_w_out,
        b_w_rkv=b_w_rkv, b_w_o=b_w_o,
        b_proj=[b_mu, row(b_w0), bf(b_w1), bf(b_w2), row(b_a0), bf(b_a1), bf(b_a2),
                bf(b_g1), bf(b_g2), row(b_k_k), row(b_k_a), row(b_r_k)],
        b_post=[row(b_lnx_g), row(b_lnx_b)],
        e=e, et=e.T,
    )
    bp = x_prompt.shape[0]
    n_b = b_mu.shape[0]
    n_c = c_w_in.shape[0]
    zero_wkv = jnp.zeros((n_b, bp) + state_b_wkv.shape[2:], state_b_wkv.dtype)
    zero_shift = jnp.zeros((n_b, bp, D_MODEL), state_b_shift.dtype)
    zero_conv = jnp.zeros((n_c, bp, CONV_W - 1, D_MODEL), state_c_conv.dtype)
    (y_p, wkv_p, shift_p, conv_p), (y_s, wkv_s, shift_s, conv_s), a_v_s = _run_trunks(
        x_prompt, x_sample, p_prompt, p_sample, (zero_wkv, zero_shift, zero_conv),
        (state_b_wkv, state_b_shift, state_c_conv), W)
    return (y_p, y_s, a_v_s, wkv_p, shift_p, conv_p, wkv_s, shift_s, conv_s)
```

```python
import collections
import functools
import math

import jax
import jax.numpy as jnp
from jax import lax
from jax.experimental import pallas as pl
from jax.experimental.pallas import tpu as pltpu

F32 = jnp.float32
BF16 = jnp.bfloat16

D_MODEL = 1024
DEPTH = 4
N_MIXERS = 3
CHUNK = 128
A_INNER = 2 * D_MODEL
A_HEADS = 8
A_GROUP = A_INNER // A_HEADS
HEAD_SIZE = 64
B_HEADS = D_MODEL // HEAD_SIZE
CONV_W = 3
D_FF = 2816
PLE_DIM = 256
ALPHA = (2 * DEPTH) ** 0.25
LN_EPS = 1e-5
GN_EPS = 64e-5

LANES = 128
MXU_DIM = 256
ROW_TILE = 512
SCAN_CHUNK = 64
HEAD_PAIRS = B_HEADS // 2
VMEM_LIMIT = 56 * 1024 * 1024

NN = (((1,), (0,)), ((), ()))
NT = (((1,), (1,)), ((), ()))
TN = (((0,), (0,)), ((), ()))


def _dot(a, b, dims=NN):
    return lax.dot_general(a, b, dims, preferred_element_type=F32)


def _split2(x):
    hi = x.astype(BF16)
    lo = (x - hi.astype(F32)).astype(BF16)
    return hi, lo


def _split3(x):
    hi = x.astype(BF16)
    r1 = x - hi.astype(F32)
    mid = r1.astype(BF16)
    lo = (r1 - mid.astype(F32)).astype(BF16)
    return hi, mid, lo


def _layer_norm(x, g, b, eps=LN_EPS):
    mu = jnp.mean(x, axis=-1, keepdims=True)
    xc = x - mu
    var = jnp.mean(xc * xc, axis=-1, keepdims=True)
    return xc * lax.rsqrt(var + eps) * g + b


def _sigmoid(x):
    return 1.0 / (1.0 + jnp.exp(-x))


class _Sel(collections.namedtuple("_Sel", ["arr", "idx"])):
    def const_spec(self):
        k = len(self.idx)
        rest = self.arr.shape[k:]
        idx = self.idx
        return pl.BlockSpec((None,) * k + rest, lambda *_: idx + (0,) * len(rest),
                            pipeline_mode=pl.Buffered(1))

    def row_spec(self, tm):
        k = len(self.idx)
        idx = self.idx
        return pl.BlockSpec((None,) * k + (tm, self.arr.shape[-1]), lambda i: idx + (i, 0))


def _sel(arr, *idx):
    return _Sel(arr, tuple(idx))


def _row_spec(width, tm=ROW_TILE):
    return pl.BlockSpec((tm, width), lambda i: (i, 0))


def _params(sem):
    return pltpu.CompilerParams(dimension_semantics=sem, vmem_limit_bytes=VMEM_LIMIT)


FF_BLOCKS = ((0, 6 * MXU_DIM), (6 * MXU_DIM, D_FF))
assert D_FF % MXU_DIM == 0


class _TwoStreams:
    def __init__(self, n_long, n_short, tm=ROW_TILE):
        assert n_long % tm == 0 and n_short % tm == 0
        self.tm = tm
        self.long_steps = n_long // tm
        self.grid = (self.long_steps + n_short // tm,)

    def long_step(self, t):
        return jnp.minimum(t, self.long_steps - 1)

    def short_step(self, t):
        return jnp.maximum(t - self.long_steps, 0)

    def row_specs(self, width, idx_long=(), idx_short=()):
        lead_l, lead_s = (None,) * len(idx_long), (None,) * len(idx_short)
        return [pl.BlockSpec(lead_l + (self.tm, width), lambda t: idx_long + (self.long_step(t), 0)),
                pl.BlockSpec(lead_s + (self.tm, width), lambda t: idx_short + (self.short_step(t), 0))]

    def per_stream(self, body):
        is_long = pl.program_id(0) < self.long_steps
        pl.when(is_long)(lambda: body(0))
        pl.when(jnp.logical_not(is_long))(lambda: body(1))


def _ffn_kernel(with_ple, n_cast, streams, xl_ref, xs_ref, wi_ref, wo_ref, g_ref, b_ref, *rest):
    rest = list(rest)
    if with_ple:
        pl_ref, ps_ref, wg_ref, wp_ref = rest[:4]
        del rest[:4]
    cast_in = rest[:n_cast]
    del rest[:n_cast]
    o_refs, cast_out = rest[:2], rest[2:]

    def body(stream):
        x = (xl_ref, xs_ref)[stream][...]
        xb = x.astype(BF16)
        acc = None
        for lo, hi in FF_BLOCKS:
            gate = _dot(xb, wi_ref[:, lo:hi])
            up = _dot(xb, wi_ref[:, D_FF + lo:D_FF + hi])
            act = (gate * _sigmoid(gate) * up).astype(BF16)
            part = _dot(act, wo_ref[lo:hi, :])
            acc = part if acc is None else acc + part
        y = _layer_norm(ALPHA * x + 0.5 * acc, g_ref[...], b_ref[...])
        if with_ple:
            gate = _sigmoid(_dot(y.astype(BF16), wg_ref[...]))
            y = y + gate * _dot((pl_ref, ps_ref)[stream][...].astype(BF16), wp_ref[...])
        o_refs[stream][...] = y
        if stream == 0:
            for src_ref, dst_ref in zip(cast_in, cast_out):
                dst_ref[...] = src_ref[...].astype(BF16)

    streams.per_stream(body)


def _cast_slab_rows(rows, steps):
    for r in range(16, rows + 1, 16):
        if rows % r == 0 and rows // r <= steps:
            return r
    raise ValueError((rows, steps))


def _ffn(x_long, x_short, wi, wo, g, b, ple=None, cast_next=()):
    streams = _TwoStreams(x_long.shape[0], x_short.shape[0])
    consts = [wi, wo, g, b]
    args = [x_long, x_short] + [c.arr for c in consts]
    specs = streams.row_specs(D_MODEL) + [c.const_spec() for c in consts]
    out_specs = streams.row_specs(D_MODEL)
    out_shape = [jax.ShapeDtypeStruct(x_long.shape, F32), jax.ShapeDtypeStruct(x_short.shape, F32)]
    if ple is not None:
        (p_long, p_short), wg, wp = ple
        args += [p_long.arr, p_short.arr, wg.arr, wp.arr]
        specs += streams.row_specs(PLE_DIM, p_long.idx, p_short.idx) + [wg.const_spec(), wp.const_spec()]
    cast_specs = []
    for w in cast_next:
        rows, cols = w.arr.shape[len(w.idx):]
        slab = _cast_slab_rows(rows, streams.long_steps)
        spec = lambda lead, idx, last=rows // slab - 1, slab=slab, cols=cols: pl.BlockSpec(
            (None,) * lead + (slab, cols), lambda t: idx + (jnp.minimum(t, last), 0))
        args.append(w.arr)
        specs.append(spec(len(w.idx), w.idx))
        cast_specs.append(spec(0, ()))
        out_shape.append(jax.ShapeDtypeStruct((rows, cols), BF16))
    out_specs += cast_specs
    return pl.pallas_call(
        functools.partial(_ffn_kernel, ple is not None, len(cast_next), streams),
        grid=streams.grid,
        in_specs=specs,
        out_specs=out_specs,
        out_shape=out_shape,
        compiler_params=_params(("arbitrary",)),
        name="ffn_ple" if ple is not None else "ffn",
    )(*args)


def _gmlp_kernel(streams, xl_ref, xs_ref, wi_ref, bi_ref, lg_ref, lb_ref, wo_ref, g_ref, b_ref,
                 wsl_ref, bsl_ref, wss_ref, bss_ref, ol_ref, os_ref, v_ref, y_ref):
    def body(stream):
        x_ref, ws_ref, bs_ref, o_ref = ((xl_ref, wsl_ref, bsl_ref, ol_ref), (xs_ref, wss_ref, bss_ref, os_ref))[stream]
        hm = streams.tm // 2
        halves = [slice(h * hm, (h + 1) * hm) for h in range(2)]
        xs = [x_ref[rows, :] for rows in halves]
        zs = [_dot(x.astype(BF16), wi_ref[...]) + bi_ref[...] for x in xs]
        us, vbs = [], []
        for rows, z in zip(halves, zs):
            z = 0.5 * z * (1.0 + lax.erf(z * (1.0 / math.sqrt(2.0))))
            v = _layer_norm(z[:, A_INNER:], lg_ref[...], lb_ref[...])
            if stream == 1:
                v_ref[rows, :] = v
            us.append(z[:, :A_INNER])
            vbs.append(v.astype(BF16))
        for half, u, vb in zip(halves, us, vbs):
            for c in range(hm // CHUNK):
                rows = slice(c * CHUNK, (c + 1) * CHUNK)
                out_rows = slice(half.start + c * CHUNK, half.start + (c + 1) * CHUNK)
                for h in range(A_HEADS):
                    cols = slice(h * A_GROUP, (h + 1) * A_GROUP)
                    mixed = _dot(ws_ref[h], vb[rows, cols]) + bs_ref[:, h:h + 1]
                    y_ref[out_rows, cols] = (u[rows, cols] * mixed).astype(BF16)
        for rows, x in zip(halves, xs):
            out = _dot(y_ref[rows, :], wo_ref[...])
            o_ref[rows, :] = _layer_norm(ALPHA * x + out, g_ref[...], b_ref[...])

    streams.per_stream(body)


def _gmlp(x_long, x_short, consts, mix_long, mix_short):
    streams = _TwoStreams(x_long.shape[0], x_short.shape[0])
    consts = consts + list(mix_long) + list(mix_short)
    return pl.pallas_call(
        functools.partial(_gmlp_kernel, streams),
        grid=streams.grid,
        in_specs=streams.row_specs(D_MODEL) + [c.const_spec() for c in consts],
        out_specs=streams.row_specs(D_MODEL) + [streams.row_specs(A_INNER)[1]],
        out_shape=[jax.ShapeDtypeStruct(x_long.shape, F32), jax.ShapeDtypeStruct(x_short.shape, F32),
                   jax.ShapeDtypeStruct((x_short.shape[0], A_INNER), F32)],
        scratch_shapes=[pltpu.VMEM((streams.tm, A_INNER), BF16)],
        compiler_params=_params(("arbitrary",)),
        name="gmlp",
    )(x_long, x_short, *[c.arr for c in consts])


def _conv_tail(x, bg, conv, wo_ref, g_ref, b_ref, o_ref):
    out = _dot((bg * conv).astype(BF16), wo_ref[...])
    o_ref[...] = _layer_norm(ALPHA * x + out, g_ref[...], b_ref[...])


def _conv_long_kernel(tiles_per_seq, x_ref, buf_ref, wi_ref, cw_ref, wo_ref, g_ref, b_ref,
                      o_ref, tail_ref, carry_ref):
    x = x_ref[...]
    tm = x.shape[0]
    h3 = _dot(x.astype(BF16), wi_ref[...])
    bg = h3[:, :D_MODEL]
    z = h3[:, D_MODEL:2 * D_MODEL] * h3[:, 2 * D_MODEL:]

    @pl.when(pl.program_id(0) % tiles_per_seq == 0)
    def _():
        carry_ref[...] = buf_ref[0]

    row = lax.broadcasted_iota(jnp.int32, (tm, D_MODEL), 0)
    prev1 = carry_ref[7:8, :]
    prev2 = carry_ref[6:7, :]
    z1 = jnp.where(row == 0, prev1, pltpu.roll(z, 1, 0))
    z2 = jnp.where(row == 0, prev2, jnp.where(row == 1, prev1, pltpu.roll(z, 2, 0)))
    conv = cw_ref[0:1, :] * z2 + cw_ref[1:2, :] * z1 + cw_ref[2:3, :] * z
    tail = z[tm - 8:, :]
    carry_ref[...] = tail
    tail_ref[0] = tail
    _conv_tail(x, bg, conv, wo_ref, g_ref, b_ref, o_ref)


def _conv_short_kernel(seq, x_ref, h1_ref, h2_ref, wi_ref, cw_ref, wo_ref, g_ref, b_ref,
                       o_ref, z_ref):
    x = x_ref[...]
    tm = x.shape[0]
    h3 = _dot(x.astype(BF16), wi_ref[...])
    bg = h3[:, :D_MODEL]
    z = h3[:, D_MODEL:2 * D_MODEL] * h3[:, 2 * D_MODEL:]
    t = lax.broadcasted_iota(jnp.int32, (tm, D_MODEL), 0) % seq
    z1 = jnp.where(t >= 1, pltpu.roll(z, 1, 0), h1_ref[...])
    z2 = jnp.where(t >= 2, pltpu.roll(z, 2, 0), h2_ref[...])
    conv = cw_ref[0:1, :] * z2 + cw_ref[1:2, :] * z1 + cw_ref[2:3, :] * z
    z_ref[...] = z
    _conv_tail(x, bg, conv, wo_ref, g_ref, b_ref, o_ref)


def _first_rows(state_rows, seq):
    bn = state_rows.shape[0]
    out = jnp.zeros((bn, seq, D_MODEL), F32).at[:, 0, :].set(state_rows)
    return out.reshape(bn * seq, D_MODEL)


def _conv_mixer(x, buf_prev, seq, consts):
    n = x.shape[0]
    bn = n // seq
    tm = min(ROW_TILE, n)
    const_specs = [c.const_spec() for c in consts]
    const_args = [c.arr for c in consts]
    if seq >= tm:
        tiles_per_seq = seq // tm
        buf8 = jnp.concatenate([jnp.zeros((bn, 6, D_MODEL), F32), buf_prev], axis=1)
        out, tails = pl.pallas_call(
            functools.partial(_conv_long_kernel, tiles_per_seq),
            grid=(n // tm,),
            in_specs=[_row_spec(D_MODEL, tm),
                      pl.BlockSpec((1, 8, D_MODEL), lambda i: (i // tiles_per_seq, 0, 0))] + const_specs,
            out_specs=[_row_spec(D_MODEL, tm), pl.BlockSpec((1, 8, D_MODEL), lambda i: (i, 0, 0))],
            out_shape=[jax.ShapeDtypeStruct((n, D_MODEL), F32),
                       jax.ShapeDtypeStruct((n // tm, 8, D_MODEL), F32)],
            scratch_shapes=[pltpu.VMEM((8, D_MODEL), F32)],
            compiler_params=_params(("arbitrary",)),
            name="conv_long",
        )(x, buf8, *const_args)
        new_buf = tails[tiles_per_seq - 1::tiles_per_seq, 6:8, :]
        return out, new_buf
    assert tm % seq == 0 and seq >= CONV_W - 1
    h1 = _first_rows(buf_prev[:, 1, :], seq)
    h2 = _first_rows(buf_prev[:, 0, :], seq) + jnp.roll(h1, 1, axis=0)
    out, z = pl.pallas_call(
        functools.partial(_conv_short_kernel, seq),
        grid=(n // tm,),
        in_specs=[_row_spec(D_MODEL, tm)] * 3 + const_specs,
        out_specs=[_row_spec(D_MODEL, tm)] * 2,
        out_shape=[jax.ShapeDtypeStruct((n, D_MODEL), F32)] * 2,
        compiler_params=_params(("parallel",)),
        name="conv_short",
    )(x, h1, h2, *const_args)
    new_buf = z.reshape(bn, seq, D_MODEL)[:, seq - (CONV_W - 1):, :]
    return out, new_buf


def _head_sum_bcast(x, e_ref, et_ref):
    s = _dot(x.astype(BF16), e_ref[...])
    return _dot(jnp.concatenate(_split2(s), axis=1), et_ref[...])


def _rwkv_proj_body(x, xprev, mu_ref, wr_ref, wk_ref, wv_ref, w0_ref, w1_ref, w2_ref, a0_ref, a1_ref,
                    a2_ref, g1_ref, g2_ref, kk_ref, ka_ref, rk_ref, e_ref, et_ref,
                    r_out, lw_out, k_out, v_out, c_out, b_out, bonus_out, g_out):
    xx = xprev - x
    mix = lambda i: (x + xx * mu_ref[i:i + 1, :]).astype(BF16)
    r = _dot(mix(0), wr_ref[...])
    k = _dot(mix(2), wk_ref[...])
    v = _dot(mix(3), wv_ref[...])
    zw = w0_ref[...] + _dot(jnp.tanh(_dot(mix(1), w1_ref[...])).astype(BF16), w2_ref[...])
    lw_out[...] = -_sigmoid(zw) * math.exp(-0.5)
    a = _sigmoid(a0_ref[...] + _dot(_dot(mix(4), a1_ref[...]).astype(BF16), a2_ref[...]))
    g_out[...] = _dot(_sigmoid(_dot(mix(5), g1_ref[...])).astype(BF16), g2_ref[...])
    kk = k * kk_ref[...]
    norm = jnp.sqrt(_head_sum_bcast(kk * kk, e_ref, et_ref))
    c = kk / jnp.maximum(norm, 1e-12)
    kmod = k * (1.0 + (a - 1.0) * ka_ref[...])
    r_out[...] = r
    k_out[...] = kmod
    v_out[...] = v
    c_out[...] = c
    b_out[...] = c * a
    bonus_out[...] = _head_sum_bcast(r * kmod * rk_ref[...], e_ref, et_ref) * v


def _rwkv_proj_long_kernel(tiles_per_seq, x_ref, shift_ref, *rest):
    carry_ref = rest[-1]
    x = x_ref[...]
    tm = x.shape[0]

    @pl.when(pl.program_id(0) % tiles_per_seq == 0)
    def _():
        carry_ref[...] = shift_ref[...]

    row = lax.broadcasted_iota(jnp.int32, (tm, D_MODEL), 0)
    xprev = jnp.where(row == 0, carry_ref[...], pltpu.roll(x, 1, 0))
    carry_ref[...] = x[tm - 1:tm, :]
    _rwkv_proj_body(x, xprev, *rest[:-1])


def _rwkv_proj_short_kernel(seq, x_ref, h1_ref, *rest):
    x = x_ref[...]
    t = lax.broadcasted_iota(jnp.int32, x.shape, 0) % seq
    xprev = jnp.where(t >= 1, pltpu.roll(x, 1, 0), h1_ref[...])
    _rwkv_proj_body(x, xprev, *rest)


def _rwkv_proj(x, shift_prev, seq, consts):
    n = x.shape[0]
    tm = min(ROW_TILE // 2, n)
    common = dict(
        grid=(n // tm,),
        out_specs=[_row_spec(D_MODEL, tm)] * 8,
        out_shape=[jax.ShapeDtypeStruct((n, D_MODEL), F32)] * 8,
    )
    const_specs = [c.const_spec() for c in consts]
    const_args = [c.arr for c in consts]
    if seq >= tm:
        tiles_per_seq = seq // tm
        return pl.pallas_call(
            functools.partial(_rwkv_proj_long_kernel, tiles_per_seq),
            in_specs=[_row_spec(D_MODEL, tm),
                      pl.BlockSpec((None, 1, D_MODEL), lambda i: (i // tiles_per_seq, 0, 0))] + const_specs,
            scratch_shapes=[pltpu.VMEM((1, D_MODEL), F32)],
            compiler_params=_params(("arbitrary",)),
            name="rwkv_proj_long", **common,
        )(x, shift_prev[:, None, :], *const_args)
    assert tm % seq == 0
    return pl.pallas_call(
        functools.partial(_rwkv_proj_short_kernel, seq),
        in_specs=[_row_spec(D_MODEL, tm)] * 2 + const_specs,
        compiler_params=_params(("parallel",)),
        name="rwkv_proj_short", **common,
    )(x, _first_rows(shift_prev, seq), *const_args)


def _pair_rows(x, lane_lo):
    zero = jnp.zeros_like(x)
    return jnp.concatenate([jnp.where(lane_lo, x, zero), jnp.where(lane_lo, zero, x)], axis=0)


def _block_diag(a, b):
    zero = jnp.zeros_like(a)
    return jnp.concatenate([jnp.concatenate([a, zero], axis=1), jnp.concatenate([zero, b], axis=1)], axis=0)


def _scan_chunk(load, store_y, states, state_fn):
    C = SCAN_CHUNK
    pairs = range(len(states))
    row = lax.broadcasted_iota(jnp.int32, (C, 2 * C), 0)
    col = lax.broadcasted_iota(jnp.int32, (C, 2 * C), 1) & (C - 1)
    strict = row > col
    incl = row >= col
    lane_lo = lax.broadcasted_iota(jnp.int32, (C, LANES), 1) < HEAD_SIZE
    tri = jnp.where(lax.broadcasted_iota(jnp.int32, (C, C), 0) >= lax.broadcasted_iota(jnp.int32, (C, C), 1),
                    1.0, 0.0).astype(BF16)
    r, lw, k, v, c, b = zip(*[load(p) for p in pairs])

    cum = []
    for p in pairs:
        parts = _dot(tri, jnp.concatenate(_split3(lw[p]), axis=1))
        cum.append(parts[:, :LANES] + (parts[:, LANES:2 * LANES] + parts[:, 2 * LANES:]))
    p_inc = [jnp.exp(cum[p]) for p in pairs]
    p_inv = [jnp.exp(-cum[p]) for p in pairs]
    p_exc = [jnp.exp(cum[p] - lw[p]) for p in pairs]
    q = [jnp.concatenate([c[p] * p_exc[p], r[p] * p_inc[p]], axis=0).astype(BF16) for p in pairs]
    bt = [(b[p] * p_inv[p]).astype(BF16) for p in pairs]
    kt = [(k[p] * p_inv[p]).astype(BF16) for p in pairs]
    vb = [v[p].astype(BF16) for p in pairs]
    kb = [jnp.concatenate([_pair_rows(bt[p], lane_lo), _pair_rows(kt[p], lane_lo)], axis=0) for p in pairs]
    gram = [_dot(q[p], jnp.concatenate([kb[p], states[p]], axis=0), NT) for p in pairs]
    l_cb = [jnp.where(strict, gram[p][:C, :2 * C], 0.0) for p in pairs]
    l_ck = [jnp.where(strict, gram[p][:C, 2 * C:4 * C], 0.0).astype(BF16) for p in pairs]
    a_rb = [jnp.where(incl, gram[p][C:, :2 * C], 0.0) for p in pairs]
    a_rk = [jnp.where(incl, gram[p][C:, 2 * C:4 * C], 0.0) for p in pairs]
    qs = [gram[p][:, 4 * C:] for p in pairs]
    v_rows = [_pair_rows(vb[p], lane_lo) for p in pairs]
    u = [-(qs[p][:C] + _dot(l_ck[p], v_rows[p])) for p in pairs]
    m = [l_cb[p].astype(BF16) for p in pairs]
    for level in range(int(math.log2(C)) - 1):
        both = [_dot(m[p], jnp.concatenate([_pair_rows(m[p], lane_lo), _pair_rows(u[p].astype(BF16), lane_lo)],
                                           axis=1)) for p in pairs]
        if level == 0:
            u = [u[p] - both[p][:, 2 * C:] for p in pairs]
        else:
            u = [u[p] + both[p][:, 2 * C:] for p in pairs]
        m = [both[p][:, :2 * C].astype(BF16) for p in pairs]
    u = [u[p] + _dot(m[p], _pair_rows(u[p].astype(BF16), lane_lo)) for p in pairs]
    ub = [u[p].astype(BF16) for p in pairs]
    for p in pairs:
        a = jnp.concatenate([a_rb[p], a_rk[p]], axis=1).astype(BF16)
        uv_rows = jnp.concatenate([_pair_rows(ub[p], lane_lo), v_rows[p]], axis=0)
        store_y(p, qs[p][C:] + _dot(a, uv_rows))
    state_fn(u, v, ub, vb, bt, kt, p_inc)


def _pair_lanes(p):
    return slice(p * LANES, (p + 1) * LANES)


def _diag_blocks_mask():
    sq_row = lax.broadcasted_iota(jnp.int32, (LANES, LANES), 0) < HEAD_SIZE
    sq_col = lax.broadcasted_iota(jnp.int32, (LANES, LANES), 1) < HEAD_SIZE
    return sq_row == sq_col


def _store_pair_state(out_ref, i, p, s_pair):
    out_ref[i, 2 * p] = s_pair[:HEAD_SIZE, :HEAD_SIZE]
    out_ref[i, 2 * p + 1] = s_pair[HEAD_SIZE:, HEAD_SIZE:]


SCAN_SEQS = 8


def _rwkv_scan_kernel(r_ref, lw_ref, k_ref, v_ref, c_ref, b_ref, s0_ref, y_ref, sfin_ref, s_ref):
    C = SCAN_CHUNK
    j = pl.program_id(1)
    chains = [(s, p) for s in range(SCAN_SEQS) for p in range(HEAD_PAIRS)]

    @pl.when(j == 0)
    def _():
        for s, p in chains:
            s_ref[s, p] = _block_diag(s0_ref[s, 2 * p], s0_ref[s, 2 * p + 1])

    diag = _diag_blocks_mask()
    refs = (r_ref, lw_ref, k_ref, v_ref, c_ref, b_ref)

    def load(ch):
        s, p = chains[ch]
        return tuple(ref[s, :, _pair_lanes(p)] for ref in refs)

    def store_y(ch, y):
        s, p = chains[ch]
        y_ref[s, :, _pair_lanes(p)] = y

    def state_fn(u, v, ub, vb, bt, kt, p_inc):
        for ch, (s, p) in enumerate(chains):
            upd = _dot(jnp.concatenate([ub[ch], vb[ch]], axis=0), jnp.concatenate([bt[ch], kt[ch]], axis=0), TN)
            s_ref[s, p] = (s_ref[s, p] + jnp.where(diag, upd, 0.0)) * p_inc[ch][C - 1:C, :]

    _scan_chunk(load, store_y, [s_ref[s, p].astype(BF16) for s, p in chains], state_fn)

    @pl.when(j == pl.num_programs(1) - 1)
    def _():
        for s, p in chains:
            _store_pair_state(sfin_ref, s, p, s_ref[s, p])


def _rwkv_scan(r, lw, k, v, c, b, s0, bn, seq):
    assert bn % SCAN_SEQS == 0
    row_spec = pl.BlockSpec((SCAN_SEQS, SCAN_CHUNK, D_MODEL), lambda i, j: (i, j, 0))
    st_spec = pl.BlockSpec((SCAN_SEQS, B_HEADS, HEAD_SIZE, HEAD_SIZE), lambda i, j: (i, 0, 0, 0))
    as_seqs = lambda z: z.reshape(bn, seq, D_MODEL)
    ys, s_new = pl.pallas_call(
        _rwkv_scan_kernel,
        grid=(bn // SCAN_SEQS, seq // SCAN_CHUNK),
        in_specs=[row_spec] * 6 + [st_spec],
        out_specs=[row_spec, st_spec],
        out_shape=[jax.ShapeDtypeStruct((bn, seq, D_MODEL), F32),
                   jax.ShapeDtypeStruct((bn, B_HEADS, HEAD_SIZE, HEAD_SIZE), F32)],
        scratch_shapes=[pltpu.VMEM((SCAN_SEQS, HEAD_PAIRS, LANES, LANES), F32)],
        compiler_params=_params(("parallel", "arbitrary")),
        name="rwkv_scan",
    )(*(as_seqs(z) for z in (r, lw, k, v, c, b)), s0)
    return ys.reshape(bn * seq, D_MODEL), s_new


def _rwkv_scan_lanes_kernel(seq, r_ref, lw_ref, k_ref, v_ref, c_ref, b_ref, s0_ref, y_ref, sfin_ref,
                            cols_ref, yt_ref):
    nb = s0_ref.shape[-1]
    for a, ref in enumerate((r_ref, lw_ref, k_ref, v_ref, c_ref, b_ref)):
        for t in range(seq):
            col = ref[pl.ds(t, nb, stride=seq), :].T
            cols_ref[a, t] = jnp.exp(col) if ref is lw_ref else col
    for hh in range(2):
        feat = slice(hh * HEAD_SIZE, (hh + 1) * HEAD_SIZE)
        for vi in range(HEAD_SIZE):
            row = hh * HEAD_SIZE + vi
            s = s0_ref[hh, vi]
            for t in range(seq):
                r, w, k, c, b = (cols_ref[a, t, feat, :] for a in (0, 1, 2, 4, 5))
                sa = jnp.sum(s * c, axis=0, keepdims=True)
                s = s * w - sa * b + cols_ref[3, t, row:row + 1, :] * k
                yt_ref[t, row:row + 1, :] = jnp.sum(s * r, axis=0, keepdims=True)
            sfin_ref[hh, vi] = s
    for t in range(seq):
        y_ref[pl.ds(t, nb, stride=seq), :] = yt_ref[t].T


def _rwkv_scan_lanes(r, lw, k, v, c, b, s0, bn, seq):
    assert bn == LANES
    n = bn * seq
    row_spec = pl.BlockSpec((n, LANES), lambda p: (0, p))
    st_spec = pl.BlockSpec((2, HEAD_SIZE, HEAD_SIZE, bn), lambda p: (p, 0, 0, 0))
    ys, s_new = pl.pallas_call(
        functools.partial(_rwkv_scan_lanes_kernel, seq),
        grid=(HEAD_PAIRS,),
        in_specs=[row_spec] * 6 + [st_spec],
        out_specs=[row_spec, st_spec],
        out_shape=[jax.ShapeDtypeStruct((n, D_MODEL), F32),
                   jax.ShapeDtypeStruct((B_HEADS, HEAD_SIZE, HEAD_SIZE, bn), F32)],
        scratch_shapes=[pltpu.VMEM((6, seq, LANES, bn), F32), pltpu.VMEM((seq, LANES, bn), F32)],
        compiler_params=_params(("parallel",)),
        name="rwkv_scan_lanes",
    )(r, lw, k, v, c, b, jnp.transpose(s0, (1, 2, 3, 0)))
    return ys, jnp.transpose(s_new, (3, 0, 1, 2))


def _rwkv_post_kernel(x_ref, y_ref, bonus_ref, gate_ref, lg_ref, lb_ref, wo_ref, e_ref, et_ref,
                      g_ref, b_ref, o_ref):
    ys = y_ref[...]
    inv = 1.0 / HEAD_SIZE
    m = _head_sum_bcast(ys, e_ref, et_ref) * inv
    yc = ys - m
    var = _head_sum_bcast(yc * yc, e_ref, et_ref) * inv
    yn = yc * lax.rsqrt(var + GN_EPS) * lg_ref[...] + lb_ref[...]
    out = _dot(((yn + bonus_ref[...]) * gate_ref[...]).astype(BF16), wo_ref[...])
    o_ref[...] = _layer_norm(ALPHA * x_ref[...] + out, g_ref[...], b_ref[...])


def _rwkv_post(x, ys, bonus, gate, consts):
    n = x.shape[0]
    tm = min(ROW_TILE, n)
    return pl.pallas_call(
        _rwkv_post_kernel,
        grid=(n // tm,),
        in_specs=[_row_spec(D_MODEL, tm)] * 4 + [c.const_spec() for c in consts],
        out_specs=_row_spec(D_MODEL, tm),
        out_shape=jax.ShapeDtypeStruct((n, D_MODEL), F32),
        compiler_params=_params(("parallel",)),
        name="rwkv_post",
    )(x, ys, bonus, gate, *[c.arr for c in consts])


def _rwkv_mixer(x, shift_prev, s0, seq, w, g, b):
    n = x.shape[0]
    bn = n // seq
    r, lw, k, v, c, bb, bonus, gate = _rwkv_proj(x, shift_prev, seq, w["proj"])
    scan = _rwkv_scan if seq % SCAN_CHUNK == 0 else _rwkv_scan_lanes
    ys, s_new = scan(r, lw, k, v, c, bb, s0, bn, seq)
    out = _rwkv_post(x, ys, bonus, gate, w["post"] + [g, b])
    return out, x.reshape(bn, seq, D_MODEL)[:, -1, :], s_new


def _run_trunks(x_long, x_short, p_long, p_short, states_long, states_short, W):
    shapes = [x_long.shape, x_short.shape]
    seqs = [s[1] for s in shapes]
    assert seqs[0] >= CHUNK > seqs[1]
    xs = [x_long.reshape(-1, D_MODEL), x_short.reshape(-1, D_MODEL)]
    ps = [p_long.reshape(DEPTH, -1, PLE_DIM), p_short.reshape(DEPTH, -1, PLE_DIM)]
    states = [states_long, states_short]
    new_v, new_wkv, new_shift, new_conv = [], ([], []), ([], []), ([], [])
    ffn_w = [_sel(W["ffn_w_in"][0, 0].astype(BF16)), _sel(W["ffn_w_out"][0, 0].astype(BF16))]

    def mixer_mats(i):
        j, kind = divmod(i, N_MIXERS)
        if kind == 0:
            return [_sel(W["a_w_in"], j), _sel(W["a_w_out"], j)]
        if kind == 1:
            return [_sel(W["b_w_rkv"], j, m) for m in range(3)] + [_sel(W["b_w_o"], j)]
        return [_sel(W["c_w_in"], j), _sel(W["c_w_out"], j)]

    def ffn_pair(xs, ffn_w, i, s, cast_next, ple=None):
        norm = (_sel(W["ln_g"], i, 2 * s), _sel(W["ln_b"], i, 2 * s))
        if ple is not None:
            ple = ((_sel(ps[0], i), _sel(ps[1], i)),) + tuple(ple)
        res = _ffn(xs[0], xs[1], *ffn_w, *norm, ple=ple, cast_next=cast_next)
        return list(res[:2]), [_sel(a) for a in res[2:]]

    for i in range(DEPTH):
        j, kind = divmod(i, N_MIXERS)
        ln_g = lambda s: _sel(W["ln_g"], i, s)
        ln_b = lambda s: _sel(W["ln_b"], i, s)
        second = [_sel(W[name], i, 1) for name in ("ffn_w_in", "ffn_w_out")] + \
                 [_sel(W[name], i) for name in ("ple_w_gate", "ple_w_proj")]
        xs, cast = ffn_pair(xs, ffn_w, i, 0, second + mixer_mats(i))
        ffn_w, ple_w, mats = cast[:2], cast[2:4], cast[4:]
        if kind == 0:
            consts = [mats[0]] + [_sel(W[name], j) for name in ("a_b_in", "a_ln_g", "a_ln_b")] + [mats[1]]
            x_l, x_s, v = _gmlp(*xs, consts + [ln_g(1), ln_b(1)], [_sel(m, j) for m in W["a_mix_long"]],
                                [_sel(m, j) for m in W["a_mix_short"]])
            xs = [x_l, x_s]
            new_v.append(v.reshape(shapes[1][0], seqs[1], A_INNER))
        elif kind == 1:
            proj = W["b_proj"]
            w = dict(proj=[_sel(proj[0], j)] + mats[:3] + [_sel(a, j) for a in proj[1:]] + [_sel(W["e"]), _sel(W["et"])],
                     post=[_sel(a, j) for a in W["b_post"]] + [mats[3], _sel(W["e"]), _sel(W["et"])])
            for t in range(2):
                wkv, shift, _ = states[t]
                xs[t], sh, s = _rwkv_mixer(xs[t], shift[j], wkv[j], seqs[t], w, ln_g(1), ln_b(1))
                new_shift[t].append(sh)
                new_wkv[t].append(s)
        else:
            consts = [mats[0], _sel(W["c_conv_w"], j), mats[1], ln_g(1), ln_b(1)]
            for t in range(2):
                xs[t], buf = _conv_mixer(xs[t], states[t][2][j], seqs[t], consts)
                new_conv[t].append(buf)
        first_next = [_sel(W[name], i + 1, 0) for name in ("ffn_w_in", "ffn_w_out")] if i + 1 < DEPTH else []
        xs, ffn_w = ffn_pair(xs, ffn_w, i, 1, first_next, ple=ple_w)
    outs = [(xs[t].reshape(shapes[t]), jnp.stack(new_wkv[t]), jnp.stack(new_shift[t]), jnp.stack(new_conv[t]))
            for t in range(2)]
    return outs[0], outs[1], jnp.stack(new_v)


def _gmlp_mix_mats(w_s, b_s, seq):
    l = min(seq, CHUNK)
    ws = jnp.where(jnp.tril(jnp.ones((l, l), dtype=bool)), w_s[..., :l, :l], 0.0)
    reps = CHUNK // l
    if reps > 1:
        pos = jnp.arange(CHUNK) // l
        ws = jnp.where(pos[:, None] == pos[None, :], jnp.tile(ws, (1, 1, reps, reps)), 0.0)
    bias = jnp.tile(jnp.swapaxes(b_s[..., :l], -1, -2), (1, reps, 1))
    return ws.astype(BF16), bias


def kernel(x_prompt, x_sample, state_b_wkv, state_b_shift, state_c_conv, p_prompt, p_sample, ln_g, ln_b, ffn_w_in, ffn_w_out, ple_w_gate, ple_w_proj, a_w_in, a_b_in, a_ln_g, a_ln_b, a_w_s, a_b_s, a_w_out, b_mu, b_w_rkv, b_w0, b_w1, b_w2, b_a0, b_a1, b_a2, b_g1, b_g2, b_k_k, b_k_a, b_r_k, b_lnx_g, b_lnx_b, b_w_o, c_w_in, c_conv_w, c_w_out):
    bf = lambda w: w.astype(BF16)
    row = lambda w: w.reshape(w.shape[0], 1, -1)
    head_of_lane = jnp.arange(D_MODEL) // HEAD_SIZE
    e = (head_of_lane[:, None] == jnp.arange(LANES)[None, :]).astype(BF16)
    W = dict(
        ln_g=ln_g[:, :, None, :], ln_b=ln_b[:, :, None, :],
        ffn_w_in=ffn_w_in, ffn_w_out=ffn_w_out,
        ple_w_gate=ple_w_gate, ple_w_proj=ple_w_proj,
        a_w_in=a_w_in, a_b_in=row(a_b_in), a_ln_g=row(a_ln_g), a_ln_b=row(a_ln_b), a_w_out=a_w_out,
        a_mix_long=_gmlp_mix_mats(a_w_s, a_b_s, x_prompt.shape[1]),
        a_mix_short=_gmlp_mix_mats(a_w_s, a_b_s, x_sample.shape[1]),
        c_w_in=c_w_in, c_conv_w=c_conv_w, c_w_out=c_w_out,
        b_w_rkv=b_w_rkv, b_w_o=b_w_o,
        b_proj=[b_mu, row(b_w0), bf(b_w1), bf(b_w2), row(b_a0), bf(b_a1), bf(b_a2),
                bf(b_g1), bf(b_g2), row(b_k_k), row(b_k_a), row(b_r_k)],
        b_post=[row(b_lnx_g), row(b_lnx_b)],
        e=e, et=jnp.concatenate([e.T, e.T], axis=0),
    )
    bp = x_prompt.shape[0]
    n_b = b_mu.shape[0]
    n_c = c_w_in.shape[0]
    zero_wkv = jnp.zeros((n_b, bp) + state_b_wkv.shape[2:], state_b_wkv.dtype)
    zero_shift = jnp.zeros((n_b, bp, D_MODEL), state_b_shift.dtype)
    zero_conv = jnp.zeros((n_c, bp, CONV_W - 1, D_MODEL), state_c_conv.dtype)
    (y_p, wkv_p, shift_p, conv_p), (y_s, wkv_s, shift_s, conv_s), a_v_s = _run_trunks(
        x_prompt, x_sample, p_prompt, p_sample, (zero_wkv, zero_shift, zero_conv),
        (state_b_wkv, state_b_shift, state_c_conv), W)
    return (y_p, y_s, a_v_s, wkv_p, shift_p, conv_p, wkv_s, shift_s, conv_s)
```

```python
import collections
import functools
import math

import jax
import jax.numpy as jnp
from jax import lax
from jax.experimental import pallas as pl
from jax.experimental.pallas import tpu as pltpu

F32 = jnp.float32
BF16 = jnp.bfloat16

D_MODEL = 1024
DEPTH = 4
N_MIXERS = 3
CHUNK = 128
A_INNER = 2 * D_MODEL
A_HEADS = 8
A_GROUP = A_INNER // A_HEADS
HEAD_SIZE = 64
B_HEADS = D_MODEL // HEAD_SIZE
CONV_W = 3
D_FF = 2816
PLE_DIM = 256
ALPHA = (2 * DEPTH) ** 0.25
LN_EPS = 1e-5
GN_EPS = 64e-5

LANES = 128
MXU_DIM = 256
ROW_TILE = 512
SCAN_CHUNK = 64
HEAD_PAIRS = B_HEADS // 2
VMEM_LIMIT = 56 * 1024 * 1024

NN = (((1,), (0,)), ((), ()))
NT = (((1,), (1,)), ((), ()))
TN = (((0,), (0,)), ((), ()))


def _dot(a, b, dims=NN):
    return lax.dot_general(a, b, dims, preferred_element_type=F32)


def _split2(x):
    hi = x.astype(BF16)
    lo = (x - hi.astype(F32)).astype(BF16)
    return hi, lo


def _split3(x):
    hi = x.astype(BF16)
    r1 = x - hi.astype(F32)
    mid = r1.astype(BF16)
    lo = (r1 - mid.astype(F32)).astype(BF16)
    return hi, mid, lo


def _layer_norm(x, g, b, eps=LN_EPS):
    mu = jnp.mean(x, axis=-1, keepdims=True)
    xc = x - mu
    var = jnp.mean(xc * xc, axis=-1, keepdims=True)
    return xc * lax.rsqrt(var + eps) * g + b


def _sigmoid(x):
    return 1.0 / (1.0 + jnp.exp(-x))


class _Sel(collections.namedtuple("_Sel", ["arr", "idx"])):
    def const_spec(self):
        k = len(self.idx)
        rest = self.arr.shape[k:]
        idx = self.idx
        return pl.BlockSpec((None,) * k + rest, lambda *_: idx + (0,) * len(rest),
                            pipeline_mode=pl.Buffered(1))

    def row_spec(self, tm):
        k = len(self.idx)
        idx = self.idx
        return pl.BlockSpec((None,) * k + (tm, self.arr.shape[-1]), lambda i: idx + (i, 0))


def _sel(arr, *idx):
    return _Sel(arr, tuple(idx))


def _row_spec(width, tm=ROW_TILE):
    return pl.BlockSpec((tm, width), lambda i: (i, 0))


def _params(sem):
    return pltpu.CompilerParams(dimension_semantics=sem, vmem_limit_bytes=VMEM_LIMIT)


FF_BLOCKS = ((0, 6 * MXU_DIM), (6 * MXU_DIM, D_FF))
assert D_FF % MXU_DIM == 0


class _TwoStreams:
    def __init__(self, n_long, n_short, tm=ROW_TILE):
        assert n_long % tm == 0 and n_short % tm == 0
        self.tm = tm
        self.long_steps = n_long // tm
        self.grid = (self.long_steps + n_short // tm,)

    def long_step(self, t):
        return jnp.minimum(t, self.long_steps - 1)

    def short_step(self, t):
        return jnp.maximum(t - self.long_steps, 0)

    def row_specs(self, width, idx_long=(), idx_short=()):
        lead_l, lead_s = (None,) * len(idx_long), (None,) * len(idx_short)
        return [pl.BlockSpec(lead_l + (self.tm, width), lambda t: idx_long + (self.long_step(t), 0)),
                pl.BlockSpec(lead_s + (self.tm, width), lambda t: idx_short + (self.short_step(t), 0))]

    def per_stream(self, body):
        is_long = pl.program_id(0) < self.long_steps
        pl.when(is_long)(lambda: body(0))
        pl.when(jnp.logical_not(is_long))(lambda: body(1))


def _ffn_kernel(with_ple, n_cast, streams, xl_ref, xs_ref, wi_ref, wo_ref, g_ref, b_ref, *rest):
    rest = list(rest)
    if with_ple:
        pl_ref, ps_ref, wg_ref, wp_ref = rest[:4]
        del rest[:4]
    cast_in = rest[:n_cast]
    del rest[:n_cast]
    o_refs, cast_out = rest[:2], rest[2:]

    def body(stream):
        x = (xl_ref, xs_ref)[stream][...]
        xb = x.astype(BF16)
        acc = None
        for lo, hi in FF_BLOCKS:
            gate = _dot(xb, wi_ref[:, lo:hi])
            up = _dot(xb, wi_ref[:, D_FF + lo:D_FF + hi])
            act = (gate * _sigmoid(gate) * up).astype(BF16)
            part = _dot(act, wo_ref[lo:hi, :])
            acc = part if acc is None else acc + part
        y = _layer_norm(ALPHA * x + 0.5 * acc, g_ref[...], b_ref[...])
        if with_ple:
            gate = _sigmoid(_dot(y.astype(BF16), wg_ref[...]))
            y = y + gate * _dot((pl_ref, ps_ref)[stream][...].astype(BF16), wp_ref[...])
        o_refs[stream][...] = y
        if stream == 0:
            for src_ref, dst_ref in zip(cast_in, cast_out):
                dst_ref[...] = src_ref[...].astype(BF16)

    streams.per_stream(body)


def _cast_slab_rows(rows, steps):
    for r in range(16, rows + 1, 16):
        if rows % r == 0 and rows // r <= steps:
            return r
    raise ValueError((rows, steps))


def _ffn(x_long, x_short, wi, wo, g, b, ple=None, cast_next=()):
    streams = _TwoStreams(x_long.shape[0], x_short.shape[0])
    consts = [wi, wo, g, b]
    args = [x_long, x_short] + [c.arr for c in consts]
    specs = streams.row_specs(D_MODEL) + [c.const_spec() for c in consts]
    out_specs = streams.row_specs(D_MODEL)
    out_shape = [jax.ShapeDtypeStruct(x_long.shape, F32), jax.ShapeDtypeStruct(x_short.shape, F32)]
    if ple is not None:
        (p_long, p_short), wg, wp = ple
        args += [p_long.arr, p_short.arr, wg.arr, wp.arr]
        specs += streams.row_specs(PLE_DIM, p_long.idx, p_short.idx) + [wg.const_spec(), wp.const_spec()]
    cast_specs = []
    for w in cast_next:
        rows, cols = w.arr.shape[len(w.idx):]
        slab = _cast_slab_rows(rows, streams.long_steps)
        spec = lambda lead, idx, last=rows // slab - 1, slab=slab, cols=cols: pl.BlockSpec(
            (None,) * lead + (slab, cols), lambda t: idx + (jnp.minimum(t, last), 0))
        args.append(w.arr)
        specs.append(spec(len(w.idx), w.idx))
        cast_specs.append(spec(0, ()))
        out_shape.append(jax.ShapeDtypeStruct((rows, cols), BF16))
    out_specs += cast_specs
    return pl.pallas_call(
        functools.partial(_ffn_kernel, ple is not None, len(cast_next), streams),
        grid=streams.grid,
        in_specs=specs,
        out_specs=out_specs,
        out_shape=out_shape,
        compiler_params=_params(("arbitrary",)),
        name="ffn_ple" if ple is not None else "ffn",
    )(*args)


def _gmlp_kernel(streams, xl_ref, xs_ref, wi_ref, bi_ref, lg_ref, lb_ref, wo_ref, g_ref, b_ref,
                 wsl_ref, bsl_ref, wss_ref, bss_ref, ol_ref, os_ref, v_ref, y_ref):
    def body(stream):
        x_ref, ws_ref, bs_ref, o_ref = ((xl_ref, wsl_ref, bsl_ref, ol_ref), (xs_ref, wss_ref, bss_ref, os_ref))[stream]
        hm = streams.tm // 2
        halves = [slice(h * hm, (h + 1) * hm) for h in range(2)]
        xs = [x_ref[rows, :] for rows in halves]
        zs = [_dot(x.astype(BF16), wi_ref[...]) + bi_ref[...] for x in xs]
        us, vbs = [], []
        for rows, z in zip(halves, zs):
            z = 0.5 * z * (1.0 + lax.erf(z * (1.0 / math.sqrt(2.0))))
            v = _layer_norm(z[:, A_INNER:], lg_ref[...], lb_ref[...])
            if stream == 1:
                v_ref[rows, :] = v
            us.append(z[:, :A_INNER])
            vbs.append(v.astype(BF16))
        for half, u, vb in zip(halves, us, vbs):
            for c in range(hm // CHUNK):
                rows = slice(c * CHUNK, (c + 1) * CHUNK)
                out_rows = slice(half.start + c * CHUNK, half.start + (c + 1) * CHUNK)
                for h in range(A_HEADS):
                    cols = slice(h * A_GROUP, (h + 1) * A_GROUP)
                    mixed = _dot(ws_ref[h], vb[rows, cols]) + bs_ref[:, h:h + 1]
                    y_ref[out_rows, cols] = (u[rows, cols] * mixed).astype(BF16)
        for rows, x in zip(halves, xs):
            out = _dot(y_ref[rows, :], wo_ref[...])
            o_ref[rows, :] = _layer_norm(ALPHA * x + out, g_ref[...], b_ref[...])

    streams.per_stream(body)


def _gmlp(x_long, x_short, consts, mix_long, mix_short):
    streams = _TwoStreams(x_long.shape[0], x_short.shape[0])
    consts = consts + list(mix_long) + list(mix_short)
    return pl.pallas_call(
        functools.partial(_gmlp_kernel, streams),
        grid=streams.grid,
        in_specs=streams.row_specs(D_MODEL) + [c.const_spec() for c in consts],
        out_specs=streams.row_specs(D_MODEL) + [streams.row_specs(A_INNER)[1]],
        out_shape=[jax.ShapeDtypeStruct(x_long.shape, F32), jax.ShapeDtypeStruct(x_short.shape, F32),
                   jax.ShapeDtypeStruct((x_short.shape[0], A_INNER), F32)],
        scratch_shapes=[pltpu.VMEM((streams.tm, A_INNER), BF16)],
        compiler_params=_params(("arbitrary",)),
        name="gmlp",
    )(x_long, x_short, *[c.arr for c in consts])


def _conv_tail(x, bg, conv, wo_ref, g_ref, b_ref, o_ref):
    out = _dot((bg * conv).astype(BF16), wo_ref[...])
    o_ref[...] = _layer_norm(ALPHA * x + out, g_ref[...], b_ref[...])


def _conv_long_kernel(tiles_per_seq, x_ref, buf_ref, wi_ref, cw_ref, wo_ref, g_ref, b_ref,
                      o_ref, tail_ref, carry_ref):
    @pl.when(pl.program_id(0) % tiles_per_seq == 0)
    def _():
        carry_ref[...] = buf_ref[0]

    hm = x_ref.shape[0] // 2
    halves = [slice(h * hm, (h + 1) * hm) for h in range(2)]
    xs = [x_ref[rows, :] for rows in halves]
    h3s = [_dot(x.astype(BF16), wi_ref[...]) for x in xs]
    row = lax.broadcasted_iota(jnp.int32, (hm, D_MODEL), 0)
    prev2, prev1 = carry_ref[6:7, :], carry_ref[7:8, :]
    gated = []
    for h3 in h3s:
        z = h3[:, D_MODEL:2 * D_MODEL] * h3[:, 2 * D_MODEL:]
        z1 = jnp.where(row == 0, prev1, pltpu.roll(z, 1, 0))
        z2 = jnp.where(row == 0, prev2, jnp.where(row == 1, prev1, pltpu.roll(z, 2, 0)))
        conv = cw_ref[0:1, :] * z2 + cw_ref[1:2, :] * z1 + cw_ref[2:3, :] * z
        gated.append((h3[:, :D_MODEL] * conv).astype(BF16))
        prev2, prev1 = z[hm - 2:hm - 1, :], z[hm - 1:hm, :]
        tail = z[hm - 8:, :]
    carry_ref[...] = tail
    tail_ref[0] = tail
    for rows, x, gt in zip(halves, xs, gated):
        o_ref[rows, :] = _layer_norm(ALPHA * x + _dot(gt, wo_ref[...]), g_ref[...], b_ref[...])


def _conv_short_kernel(seq, x_ref, h1_ref, h2_ref, wi_ref, cw_ref, wo_ref, g_ref, b_ref,
                       o_ref, z_ref):
    x = x_ref[...]
    tm = x.shape[0]
    h3 = _dot(x.astype(BF16), wi_ref[...])
    bg = h3[:, :D_MODEL]
    z = h3[:, D_MODEL:2 * D_MODEL] * h3[:, 2 * D_MODEL:]
    t = lax.broadcasted_iota(jnp.int32, (tm, D_MODEL), 0) % seq
    z1 = jnp.where(t >= 1, pltpu.roll(z, 1, 0), h1_ref[...])
    z2 = jnp.where(t >= 2, pltpu.roll(z, 2, 0), h2_ref[...])
    conv = cw_ref[0:1, :] * z2 + cw_ref[1:2, :] * z1 + cw_ref[2:3, :] * z
    z_ref[...] = z
    _conv_tail(x, bg, conv, wo_ref, g_ref, b_ref, o_ref)


def _first_rows(state_rows, seq):
    bn = state_rows.shape[0]
    out = jnp.zeros((bn, seq, D_MODEL), F32).at[:, 0, :].set(state_rows)
    return out.reshape(bn * seq, D_MODEL)


def _conv_mixer(x, buf_prev, seq, consts):
    n = x.shape[0]
    bn = n // seq
    tm = min(ROW_TILE, n)
    const_specs = [c.const_spec() for c in consts]
    const_args = [c.arr for c in consts]
    if seq >= tm:
        tiles_per_seq = seq // tm
        buf8 = jnp.concatenate([jnp.zeros((bn, 6, D_MODEL), F32), buf_prev], axis=1)
        out, tails = pl.pallas_call(
            functools.partial(_conv_long_kernel, tiles_per_seq),
            grid=(n // tm,),
            in_specs=[_row_spec(D_MODEL, tm),
                      pl.BlockSpec((1, 8, D_MODEL), lambda i: (i // tiles_per_seq, 0, 0))] + const_specs,
            out_specs=[_row_spec(D_MODEL, tm), pl.BlockSpec((1, 8, D_MODEL), lambda i: (i, 0, 0))],
            out_shape=[jax.ShapeDtypeStruct((n, D_MODEL), F32),
                       jax.ShapeDtypeStruct((n // tm, 8, D_MODEL), F32)],
            scratch_shapes=[pltpu.VMEM((8, D_MODEL), F32)],
            compiler_params=_params(("arbitrary",)),
            name="conv_long",
        )(x, buf8, *const_args)
        new_buf = tails[tiles_per_seq - 1::tiles_per_seq, 6:8, :]
        return out, new_buf
    assert tm % seq == 0 and seq >= CONV_W - 1
    h1 = _first_rows(buf_prev[:, 1, :], seq)
    h2 = _first_rows(buf_prev[:, 0, :], seq) + jnp.roll(h1, 1, axis=0)
    out, z = pl.pallas_call(
        functools.partial(_conv_short_kernel, seq),
        grid=(n // tm,),
        in_specs=[_row_spec(D_MODEL, tm)] * 3 + const_specs,
        out_specs=[_row_spec(D_MODEL, tm)] * 2,
        out_shape=[jax.ShapeDtypeStruct((n, D_MODEL), F32)] * 2,
        compiler_params=_params(("parallel",)),
        name="conv_short",
    )(x, h1, h2, *const_args)
    new_buf = z.reshape(bn, seq, D_MODEL)[:, seq - (CONV_W - 1):, :]
    return out, new_buf


def _head_sum_bcast(x, e_ref, et_ref):
    s = _dot(x.astype(BF16), e_ref[...])
    return _dot(jnp.concatenate(_split2(s), axis=1), et_ref[...])


def _rwkv_proj_body(x, xprev, mu_ref, wr_ref, wk_ref, wv_ref, w0_ref, w1_ref, w2_ref, a0_ref, a1_ref,
                    a2_ref, g1_ref, g2_ref, kk_ref, ka_ref, rk_ref, e_ref, et_ref,
                    r_out, lw_out, k_out, v_out, c_out, b_out, bonus_out, g_out):
    xx = xprev - x
    mix = lambda i: (x + xx * mu_ref[i:i + 1, :]).astype(BF16)
    r = _dot(mix(0), wr_ref[...])
    k = _dot(mix(2), wk_ref[...])
    v = _dot(mix(3), wv_ref[...])
    zw = w0_ref[...] + _dot(jnp.tanh(_dot(mix(1), w1_ref[...])).astype(BF16), w2_ref[...])
    lw_out[...] = -_sigmoid(zw) * math.exp(-0.5)
    a = _sigmoid(a0_ref[...] + _dot(_dot(mix(4), a1_ref[...]).astype(BF16), a2_ref[...]))
    g_out[...] = _dot(_sigmoid(_dot(mix(5), g1_ref[...])).astype(BF16), g2_ref[...])
    kk = k * kk_ref[...]
    norm = jnp.sqrt(_head_sum_bcast(kk * kk, e_ref, et_ref))
    c = kk / jnp.maximum(norm, 1e-12)
    kmod = k * (1.0 + (a - 1.0) * ka_ref[...])
    r_out[...] = r
    k_out[...] = kmod
    v_out[...] = v
    c_out[...] = c
    b_out[...] = c * a
    bonus_out[...] = _head_sum_bcast(r * kmod * rk_ref[...], e_ref, et_ref) * v


def _rwkv_proj_long_kernel(tiles_per_seq, x_ref, shift_ref, *rest):
    carry_ref = rest[-1]
    x = x_ref[...]
    tm = x.shape[0]

    @pl.when(pl.program_id(0) % tiles_per_seq == 0)
    def _():
        carry_ref[...] = shift_ref[...]

    row = lax.broadcasted_iota(jnp.int32, (tm, D_MODEL), 0)
    xprev = jnp.where(row == 0, carry_ref[...], pltpu.roll(x, 1, 0))
    carry_ref[...] = x[tm - 1:tm, :]
    _rwkv_proj_body(x, xprev, *rest[:-1])


def _rwkv_proj_short_kernel(seq, x_ref, h1_ref, *rest):
    x = x_ref[...]
    t = lax.broadcasted_iota(jnp.int32, x.shape, 0) % seq
    xprev = jnp.where(t >= 1, pltpu.roll(x, 1, 0), h1_ref[...])
    _rwkv_proj_body(x, xprev, *rest)


def _rwkv_proj(x, shift_prev, seq, consts):
    n = x.shape[0]
    tm = min(ROW_TILE // 2, n)
    common = dict(
        grid=(n // tm,),
        out_specs=[_row_spec(D_MODEL, tm)] * 8,
        out_shape=[jax.ShapeDtypeStruct((n, D_MODEL), F32)] * 8,
    )
    const_specs = [c.const_spec() for c in consts]
    const_args = [c.arr for c in consts]
    if seq >= tm:
        tiles_per_seq = seq // tm
        return pl.pallas_call(
            functools.partial(_rwkv_proj_long_kernel, tiles_per_seq),
            in_specs=[_row_spec(D_MODEL, tm),
                      pl.BlockSpec((None, 1, D_MODEL), lambda i: (i // tiles_per_seq, 0, 0))] + const_specs,
            scratch_shapes=[pltpu.VMEM((1, D_MODEL), F32)],
            compiler_params=_params(("arbitrary",)),
            name="rwkv_proj_long", **common,
        )(x, shift_prev[:, None, :], *const_args)
    assert tm % seq == 0
    return pl.pallas_call(
        functools.partial(_rwkv_proj_short_kernel, seq),
        in_specs=[_row_spec(D_MODEL, tm)] * 2 + const_specs,
        compiler_params=_params(("parallel",)),
        name="rwkv_proj_short", **common,
    )(x, _first_rows(shift_prev, seq), *const_args)


def _pair_rows(x, lane_lo):
    zero = jnp.zeros_like(x)
    return jnp.concatenate([jnp.where(lane_lo, x, zero), jnp.where(lane_lo, zero, x)], axis=0)


def _block_diag(a, b):
    zero = jnp.zeros_like(a)
    return jnp.concatenate([jnp.concatenate([a, zero], axis=1), jnp.concatenate([zero, b], axis=1)], axis=0)


def _scan_chunk(load, store_y, states, state_fn):
    C = SCAN_CHUNK
    pairs = range(len(states))
    row = lax.broadcasted_iota(jnp.int32, (C, 2 * C), 0)
    col = lax.broadcasted_iota(jnp.int32, (C, 2 * C), 1) & (C - 1)
    strict = row > col
    incl = row >= col
    lane_lo = lax.broadcasted_iota(jnp.int32, (C, LANES), 1) < HEAD_SIZE
    tri = jnp.where(lax.broadcasted_iota(jnp.int32, (C, C), 0) >= lax.broadcasted_iota(jnp.int32, (C, C), 1),
                    1.0, 0.0).astype(BF16)
    r, lw, k, v, c, b = zip(*[load(p) for p in pairs])

    cum = []
    for p in pairs:
        parts = _dot(tri, jnp.concatenate(_split3(lw[p]), axis=1))
        cum.append(parts[:, :LANES] + (parts[:, LANES:2 * LANES] + parts[:, 2 * LANES:]))
    p_inc = [jnp.exp(cum[p]) for p in pairs]
    p_inv = [jnp.exp(-cum[p]) for p in pairs]
    p_exc = [jnp.exp(cum[p] - lw[p]) for p in pairs]
    q = [jnp.concatenate([c[p] * p_exc[p], r[p] * p_inc[p]], axis=0).astype(BF16) for p in pairs]
    bt = [(b[p] * p_inv[p]).astype(BF16) for p in pairs]
    kt = [(k[p] * p_inv[p]).astype(BF16) for p in pairs]
    vb = [v[p].astype(BF16) for p in pairs]
    kb = [jnp.concatenate([_pair_rows(bt[p], lane_lo), _pair_rows(kt[p], lane_lo)], axis=0) for p in pairs]
    gram = [_dot(q[p], jnp.concatenate([kb[p], states[p]], axis=0), NT) for p in pairs]
    l_cb = [jnp.where(strict, gram[p][:C, :2 * C], 0.0) for p in pairs]
    l_ck = [jnp.where(strict, gram[p][:C, 2 * C:4 * C], 0.0).astype(BF16) for p in pairs]
    a_rb = [jnp.where(incl, gram[p][C:, :2 * C], 0.0) for p in pairs]
    a_rk = [jnp.where(incl, gram[p][C:, 2 * C:4 * C], 0.0) for p in pairs]
    qs = [gram[p][:, 4 * C:] for p in pairs]
    v_rows = [_pair_rows(vb[p], lane_lo) for p in pairs]
    u = [-(qs[p][:C] + _dot(l_ck[p], v_rows[p])) for p in pairs]
    m = [l_cb[p].astype(BF16) for p in pairs]
    for level in range(int(math.log2(C)) - 1):
        both = [_dot(m[p], jnp.concatenate([_pair_rows(m[p], lane_lo), _pair_rows(u[p].astype(BF16), lane_lo)],
                                           axis=1)) for p in pairs]
        if level == 0:
            u = [u[p] - both[p][:, 2 * C:] for p in pairs]
        else:
            u = [u[p] + both[p][:, 2 * C:] for p in pairs]
        m = [both[p][:, :2 * C].astype(BF16) for p in pairs]
    u = [u[p] + _dot(m[p], _pair_rows(u[p].astype(BF16), lane_lo)) for p in pairs]
    ub = [u[p].astype(BF16) for p in pairs]
    for p in pairs:
        a = jnp.concatenate([a_rb[p], a_rk[p]], axis=1).astype(BF16)
        uv_rows = jnp.concatenate([_pair_rows(ub[p], lane_lo), v_rows[p]], axis=0)
        store_y(p, qs[p][C:] + _dot(a, uv_rows))
    state_fn(u, v, ub, vb, bt, kt, p_inc)


def _pair_lanes(p):
    return slice(p * LANES, (p + 1) * LANES)


def _diag_blocks_mask():
    sq_row = lax.broadcasted_iota(jnp.int32, (LANES, LANES), 0) < HEAD_SIZE
    sq_col = lax.broadcasted_iota(jnp.int32, (LANES, LANES), 1) < HEAD_SIZE
    return sq_row == sq_col


def _store_pair_state(out_ref, i, p, s_pair):
    out_ref[i, 2 * p] = s_pair[:HEAD_SIZE, :HEAD_SIZE]
    out_ref[i, 2 * p + 1] = s_pair[HEAD_SIZE:, HEAD_SIZE:]


SCAN_SEQS = 8


def _rwkv_scan_kernel(r_ref, lw_ref, k_ref, v_ref, c_ref, b_ref, s0_ref, y_ref, sfin_ref, s_ref):
    C = SCAN_CHUNK
    j = pl.program_id(1)
    chains = [(s, p) for s in range(SCAN_SEQS) for p in range(HEAD_PAIRS)]

    @pl.when(j == 0)
    def _():
        for s, p in chains:
            s_ref[s, p] = _block_diag(s0_ref[s, 2 * p], s0_ref[s, 2 * p + 1])

    diag = _diag_blocks_mask()
    refs = (r_ref, lw_ref, k_ref, v_ref, c_ref, b_ref)

    def load(ch):
        s, p = chains[ch]
        return tuple(ref[s, :, _pair_lanes(p)] for ref in refs)

    def store_y(ch, y):
        s, p = chains[ch]
        y_ref[s, :, _pair_lanes(p)] = y

    def state_fn(u, v, ub, vb, bt, kt, p_inc):
        for ch, (s, p) in enumerate(chains):
            upd = _dot(jnp.concatenate([ub[ch], vb[ch]], axis=0), jnp.concatenate([bt[ch], kt[ch]], axis=0), TN)
            s_ref[s, p] = (s_ref[s, p] + jnp.where(diag, upd, 0.0)) * p_inc[ch][C - 1:C, :]

    _scan_chunk(load, store_y, [s_ref[s, p].astype(BF16) for s, p in chains], state_fn)

    @pl.when(j == pl.num_programs(1) - 1)
    def _():
        for s, p in chains:
            _store_pair_state(sfin_ref, s, p, s_ref[s, p])


def _rwkv_scan(r, lw, k, v, c, b, s0, bn, seq):
    assert bn % SCAN_SEQS == 0
    row_spec = pl.BlockSpec((SCAN_SEQS, SCAN_CHUNK, D_MODEL), lambda i, j: (i, j, 0))
    st_spec = pl.BlockSpec((SCAN_SEQS, B_HEADS, HEAD_SIZE, HEAD_SIZE), lambda i, j: (i, 0, 0, 0))
    as_seqs = lambda z: z.reshape(bn, seq, D_MODEL)
    ys, s_new = pl.pallas_call(
        _rwkv_scan_kernel,
        grid=(bn // SCAN_SEQS, seq // SCAN_CHUNK),
        in_specs=[row_spec] * 6 + [st_spec],
        out_specs=[row_spec, st_spec],
        out_shape=[jax.ShapeDtypeStruct((bn, seq, D_MODEL), F32),
                   jax.ShapeDtypeStruct((bn, B_HEADS, HEAD_SIZE, HEAD_SIZE), F32)],
        scratch_shapes=[pltpu.VMEM((SCAN_SEQS, HEAD_PAIRS, LANES, LANES), F32)],
        compiler_params=_params(("parallel", "arbitrary")),
        name="rwkv_scan",
    )(*(as_seqs(z) for z in (r, lw, k, v, c, b)), s0)
    return ys.reshape(bn * seq, D_MODEL), s_new


def _rwkv_scan_lanes_kernel(seq, r_ref, lw_ref, k_ref, v_ref, c_ref, b_ref, s0_ref, y_ref, sfin_ref,
                            cols_ref, yt_ref):
    nb = s0_ref.shape[-1]
    for a, ref in enumerate((r_ref, lw_ref, k_ref, v_ref, c_ref, b_ref)):
        for t in range(seq):
            col = ref[pl.ds(t, nb, stride=seq), :].T
            cols_ref[a, t] = jnp.exp(col) if ref is lw_ref else col
    for hh in range(2):
        feat = slice(hh * HEAD_SIZE, (hh + 1) * HEAD_SIZE)
        for vi in range(HEAD_SIZE):
            row = hh * HEAD_SIZE + vi
            s = s0_ref[hh, vi]
            for t in range(seq):
                r, w, k, c, b = (cols_ref[a, t, feat, :] for a in (0, 1, 2, 4, 5))
                sa = jnp.sum(s * c, axis=0, keepdims=True)
                s = s * w - sa * b + cols_ref[3, t, row:row + 1, :] * k
                yt_ref[t, row:row + 1, :] = jnp.sum(s * r, axis=0, keepdims=True)
            sfin_ref[hh, vi] = s
    for t in range(seq):
        y_ref[pl.ds(t, nb, stride=seq), :] = yt_ref[t].T


def _rwkv_scan_lanes(r, lw, k, v, c, b, s0, bn, seq):
    assert bn == LANES
    n = bn * seq
    row_spec = pl.BlockSpec((n, LANES), lambda p: (0, p))
    st_spec = pl.BlockSpec((2, HEAD_SIZE, HEAD_SIZE, bn), lambda p: (p, 0, 0, 0))
    ys, s_new = pl.pallas_call(
        functools.partial(_rwkv_scan_lanes_kernel, seq),
        grid=(HEAD_PAIRS,),
        in_specs=[row_spec] * 6 + [st_spec],
        out_specs=[row_spec, st_spec],
        out_shape=[jax.ShapeDtypeStruct((n, D_MODEL), F32),
                   jax.ShapeDtypeStruct((B_HEADS, HEAD_SIZE, HEAD_SIZE, bn), F32)],
        scratch_shapes=[pltpu.VMEM((6, seq, LANES, bn), F32), pltpu.VMEM((seq, LANES, bn), F32)],
        compiler_params=_params(("parallel",)),
        name="rwkv_scan_lanes",
    )(r, lw, k, v, c, b, jnp.transpose(s0, (1, 2, 3, 0)))
    return ys, jnp.transpose(s_new, (3, 0, 1, 2))


def _rwkv_post_kernel(x_ref, y_ref, bonus_ref, gate_ref, lg_ref, lb_ref, wo_ref, e_ref, et_ref,
                      g_ref, b_ref, o_ref):
    hm = x_ref.shape[0] // 2
    halves = [slice(h * hm, (h + 1) * hm) for h in range(2)]
    inv = 1.0 / HEAD_SIZE
    ycs = [y_ref[rows, :] for rows in halves]
    ycs = [ys - _head_sum_bcast(ys, e_ref, et_ref) * inv for ys in ycs]
    variances = [_head_sum_bcast(yc * yc, e_ref, et_ref) * inv for yc in ycs]
    gated = [((yc * lax.rsqrt(var + GN_EPS) * lg_ref[...] + lb_ref[...] + bonus_ref[rows, :])
              * gate_ref[rows, :]).astype(BF16) for rows, yc, var in zip(halves, ycs, variances)]
    for rows, gt in zip(halves, gated):
        o_ref[rows, :] = _layer_norm(ALPHA * x_ref[rows, :] + _dot(gt, wo_ref[...]), g_ref[...], b_ref[...])


def _rwkv_post(x, ys, bonus, gate, consts):
    n = x.shape[0]
    tm = min(ROW_TILE, n)
    return pl.pallas_call(
        _rwkv_post_kernel,
        grid=(n // tm,),
        in_specs=[_row_spec(D_MODEL, tm)] * 4 + [c.const_spec() for c in consts],
        out_specs=_row_spec(D_MODEL, tm),
        out_shape=jax.ShapeDtypeStruct((n, D_MODEL), F32),
        compiler_params=_params(("parallel",)),
        name="rwkv_post",
    )(x, ys, bonus, gate, *[c.arr for c in consts])


def _rwkv_mixer(x, shift_prev, s0, seq, w, g, b):
    n = x.shape[0]
    bn = n // seq
    r, lw, k, v, c, bb, bonus, gate = _rwkv_proj(x, shift_prev, seq, w["proj"])
    scan = _rwkv_scan if seq % SCAN_CHUNK == 0 else _rwkv_scan_lanes
    ys, s_new = scan(r, lw, k, v, c, bb, s0, bn, seq)
    out = _rwkv_post(x, ys, bonus, gate, w["post"] + [g, b])
    return out, x.reshape(bn, seq, D_MODEL)[:, -1, :], s_new


def _run_trunks(x_long, x_short, p_long, p_short, states_long, states_short, W):
    shapes = [x_long.shape, x_short.shape]
    seqs = [s[1] for s in shapes]
    assert seqs[0] >= CHUNK > seqs[1]
    xs = [x_long.reshape(-1, D_MODEL), x_short.reshape(-1, D_MODEL)]
    ps = [p_long.reshape(DEPTH, -1, PLE_DIM), p_short.reshape(DEPTH, -1, PLE_DIM)]
    states = [states_long, states_short]
    new_v, new_wkv, new_shift, new_conv = [], ([], []), ([], []), ([], [])
    ffn_w = [_sel(W["ffn_w_in"][0, 0].astype(BF16)), _sel(W["ffn_w_out"][0, 0].astype(BF16))]

    def mixer_mats(i):
        j, kind = divmod(i, N_MIXERS)
        if kind == 0:
            return [_sel(W["a_w_in"], j), _sel(W["a_w_out"], j)]
        if kind == 1:
            return [_sel(W["b_w_rkv"], j, m) for m in range(3)] + [_sel(W["b_w_o"], j)]
        return [_sel(W["c_w_in"], j), _sel(W["c_w_out"], j)]

    def ffn_pair(xs, ffn_w, i, s, cast_next, ple=None):
        norm = (_sel(W["ln_g"], i, 2 * s), _sel(W["ln_b"], i, 2 * s))
        if ple is not None:
            ple = ((_sel(ps[0], i), _sel(ps[1], i)),) + tuple(ple)
        res = _ffn(xs[0], xs[1], *ffn_w, *norm, ple=ple, cast_next=cast_next)
        return list(res[:2]), [_sel(a) for a in res[2:]]

    for i in range(DEPTH):
        j, kind = divmod(i, N_MIXERS)
        ln_g = lambda s: _sel(W["ln_g"], i, s)
        ln_b = lambda s: _sel(W["ln_b"], i, s)
        second = [_sel(W[name], i, 1) for name in ("ffn_w_in", "ffn_w_out")] + \
                 [_sel(W[name], i) for name in ("ple_w_gate", "ple_w_proj")]
        xs, cast = ffn_pair(xs, ffn_w, i, 0, second + mixer_mats(i))
        ffn_w, ple_w, mats = cast[:2], cast[2:4], cast[4:]
        if kind == 0:
            consts = [mats[0]] + [_sel(W[name], j) for name in ("a_b_in", "a_ln_g", "a_ln_b")] + [mats[1]]
            x_l, x_s, v = _gmlp(*xs, consts + [ln_g(1), ln_b(1)], [_sel(m, j) for m in W["a_mix_long"]],
                                [_sel(m, j) for m in W["a_mix_short"]])
            xs = [x_l, x_s]
            new_v.append(v.reshape(shapes[1][0], seqs[1], A_INNER))
        elif kind == 1:
            proj = W["b_proj"]
            w = dict(proj=[_sel(proj[0], j)] + mats[:3] + [_sel(a, j) for a in proj[1:]] + [_sel(W["e"]), _sel(W["et"])],
                     post=[_sel(a, j) for a in W["b_post"]] + [mats[3], _sel(W["e"]), _sel(W["et"])])
            for t in range(2):
                wkv, shift, _ = states[t]
                xs[t], sh, s = _rwkv_mixer(xs[t], shift[j], wkv[j], seqs[t], w, ln_g(1), ln_b(1))
                new_shift[t].append(sh)
                new_wkv[t].append(s)
        else:
            consts = [mats[0], _sel(W["c_conv_w"], j), mats[1], ln_g(1), ln_b(1)]
            for t in range(2):
                xs[t], buf = _conv_mixer(xs[t], states[t][2][j], seqs[t], consts)
                new_conv[t].append(buf)
        first_next = [_sel(W[name], i + 1, 0) for name in ("ffn_w_in", "ffn_w_out")] if i + 1 < DEPTH else []
        xs, ffn_w = ffn_pair(xs, ffn_w, i, 1, first_next, ple=ple_w)
    outs = [(xs[t].reshape(shapes[t]), jnp.stack(new_wkv[t]), jnp.stack(new_shift[t]), jnp.stack(new_conv[t]))
            for t in range(2)]
    return outs[0], outs[1], jnp.stack(new_v)


def _gmlp_mix_mats(w_s, b_s, seq):
    l = min(seq, CHUNK)
    ws = jnp.where(jnp.tril(jnp.ones((l, l), dtype=bool)), w_s[..., :l, :l], 0.0)
    reps = CHUNK // l
    if reps > 1:
        pos = jnp.arange(CHUNK) // l
        ws = jnp.where(pos[:, None] == pos[None, :], jnp.tile(ws, (1, 1, reps, reps)), 0.0)
    bias = jnp.tile(jnp.swapaxes(b_s[..., :l], -1, -2), (1, reps, 1))
    return ws.astype(BF16), bias


def kernel(x_prompt, x_sample, state_b_wkv, state_b_shift, state_c_conv, p_prompt, p_sample, ln_g, ln_b, ffn_w_in, ffn_w_out, ple_w_gate, ple_w_proj, a_w_in, a_b_in, a_ln_g, a_ln_b, a_w_s, a_b_s, a_w_out, b_mu, b_w_rkv, b_w0, b_w1, b_w2, b_a0, b_a1, b_a2, b_g1, b_g2, b_k_k, b_k_a, b_r_k, b_lnx_g, b_lnx_b, b_w_o, c_w_in, c_conv_w, c_w_out):
    bf = lambda w: w.astype(BF16)
    row = lambda w: w.reshape(w.shape[0], 1, -1)
    head_of_lane = jnp.arange(D_MODEL) // HEAD_SIZE
    e = (head_of_lane[:, None] == jnp.arange(LANES)[None, :]).astype(BF16)
    W = dict(
        ln_g=ln_g[:, :, None, :], ln_b=ln_b[:, :, None, :],
        ffn_w_in=ffn_w_in, ffn_w_out=ffn_w_out,
        ple_w_gate=ple_w_gate, ple_w_proj=ple_w_proj,
        a_w_in=a_w_in, a_b_in=row(a_b_in), a_ln_g=row(a_ln_g), a_ln_b=row(a_ln_b), a_w_out=a_w_out,
        a_mix_long=_gmlp_mix_mats(a_w_s, a_b_s, x_prompt.shape[1]),
        a_mix_short=_gmlp_mix_mats(a_w_s, a_b_s, x_sample.shape[1]),
        c_w_in=c_w_in, c_conv_w=c_conv_w, c_w_out=c_w_out,
        b_w_rkv=b_w_rkv, b_w_o=b_w_o,
        b_proj=[b_mu, row(b_w0), bf(b_w1), bf(b_w2), row(b_a0), bf(b_a1), bf(b_a2),
                bf(b_g1), bf(b_g2), row(b_k_k), row(b_k_a), row(b_r_k)],
        b_post=[row(b_lnx_g), row(b_lnx_b)],
        e=e, et=jnp.concatenate([e.T, e.T], axis=0),
    )
    bp = x_prompt.shape[0]
    n_b = b_mu.shape[0]
    n_c = c_w_in.shape[0]
    zero_wkv = jnp.zeros((n_b, bp) + state_b_wkv.shape[2:], state_b_wkv.dtype)
    zero_shift = jnp.zeros((n_b, bp, D_MODEL), state_b_shift.dtype)
    zero_conv = jnp.zeros((n_c, bp, CONV_W - 1, D_MODEL), state_c_conv.dtype)
    (y_p, wkv_p, shift_p, conv_p), (y_s, wkv_s, shift_s, conv_s), a_v_s = _run_trunks(
        x_prompt, x_sample, p_prompt, p_sample, (zero_wkv, zero_shift, zero_conv),
        (state_b_wkv, state_b_shift, state_c_conv), W)
    return (y_p, y_s, a_v_s, wkv_p, shift_p, conv_p, wkv_s, shift_s, conv_s)
```

```python
import collections
import functools
import math

import jax
import jax.numpy as jnp
from jax import lax
from jax.experimental import pallas as pl
from jax.experimental.pallas import tpu as pltpu

F32 = jnp.float32
BF16 = jnp.bfloat16

D_MODEL = 1024
DEPTH = 4
N_MIXERS = 3
CHUNK = 128
A_INNER = 2 * D_MODEL
A_HEADS = 8
A_GROUP = A_INNER // A_HEADS
HEAD_SIZE = 64
B_HEADS = D_MODEL // HEAD_SIZE
CONV_W = 3
D_FF = 2816
PLE_DIM = 256
ALPHA = (2 * DEPTH) ** 0.25
LN_EPS = 1e-5
GN_EPS = 64e-5

LANES = 128
MXU_DIM = 256
ROW_TILE = 512
SCAN_CHUNK = 64
HEAD_PAIRS = B_HEADS // 2
VMEM_LIMIT = 56 * 1024 * 1024

NN = (((1,), (0,)), ((), ()))
NT = (((1,), (1,)), ((), ()))
TN = (((0,), (0,)), ((), ()))


def _dot(a, b, dims=NN):
    return lax.dot_general(a, b, dims, preferred_element_type=F32)


def _split2(x):
    hi = x.astype(BF16)
    lo = (x - hi.astype(F32)).astype(BF16)
    return hi, lo


def _split3(x):
    hi = x.astype(BF16)
    r1 = x - hi.astype(F32)
    mid = r1.astype(BF16)
    lo = (r1 - mid.astype(F32)).astype(BF16)
    return hi, mid, lo


def _layer_norm(x, g, b, eps=LN_EPS):
    mu = jnp.mean(x, axis=-1, keepdims=True)
    xc = x - mu
    var = jnp.mean(xc * xc, axis=-1, keepdims=True)
    return xc * lax.rsqrt(var + eps) * g + b


def _sigmoid(x):
    return 1.0 / (1.0 + jnp.exp(-x))


class _Sel(collections.namedtuple("_Sel", ["arr", "idx"])):
    def const_spec(self):
        k = len(self.idx)
        rest = self.arr.shape[k:]
        idx = self.idx
        return pl.BlockSpec((None,) * k + rest, lambda *_: idx + (0,) * len(rest),
                            pipeline_mode=pl.Buffered(1))

    def row_spec(self, tm):
        k = len(self.idx)
        idx = self.idx
        return pl.BlockSpec((None,) * k + (tm, self.arr.shape[-1]), lambda i: idx + (i, 0))


def _sel(arr, *idx):
    return _Sel(arr, tuple(idx))


def _row_spec(width, tm=ROW_TILE):
    return pl.BlockSpec((tm, width), lambda i: (i, 0))


def _params(sem):
    return pltpu.CompilerParams(dimension_semantics=sem, vmem_limit_bytes=VMEM_LIMIT)


FF_BLOCKS = ((0, 6 * MXU_DIM), (6 * MXU_DIM, D_FF))
assert D_FF % MXU_DIM == 0


class _TwoStreams:
    def __init__(self, n_long, n_short, tm=ROW_TILE):
        assert n_long % tm == 0 and n_short % tm == 0
        self.tm = tm
        self.long_steps = n_long // tm
        self.grid = (self.long_steps + n_short // tm,)

    def long_step(self, t):
        return jnp.minimum(t, self.long_steps - 1)

    def short_step(self, t):
        return jnp.maximum(t - self.long_steps, 0)

    def row_specs(self, width, idx_long=(), idx_short=()):
        lead_l, lead_s = (None,) * len(idx_long), (None,) * len(idx_short)
        return [pl.BlockSpec(lead_l + (self.tm, width), lambda t: idx_long + (self.long_step(t), 0)),
                pl.BlockSpec(lead_s + (self.tm, width), lambda t: idx_short + (self.short_step(t), 0))]

    def per_stream(self, body):
        is_long = pl.program_id(0) < self.long_steps
        pl.when(is_long)(lambda: body(0))
        pl.when(jnp.logical_not(is_long))(lambda: body(1))


def _ffn_kernel(with_ple, n_cast, streams, xl_ref, xs_ref, wi_ref, wo_ref, g_ref, b_ref, *rest):
    rest = list(rest)
    if with_ple:
        pl_ref, ps_ref, wg_ref, wp_ref = rest[:4]
        del rest[:4]
    cast_in = rest[:n_cast]
    del rest[:n_cast]
    o_refs, cast_out = rest[:2], rest[2:]

    def body(stream):
        x = (xl_ref, xs_ref)[stream][...]
        xb = x.astype(BF16)
        acc = None
        for lo, hi in FF_BLOCKS:
            gate = _dot(xb, wi_ref[:, lo:hi])
            up = _dot(xb, wi_ref[:, D_FF + lo:D_FF + hi])
            act = (gate * _sigmoid(gate) * up).astype(BF16)
            part = _dot(act, wo_ref[lo:hi, :])
            acc = part if acc is None else acc + part
        y = _layer_norm(ALPHA * x + 0.5 * acc, g_ref[...], b_ref[...])
        if with_ple:
            gate = _sigmoid(_dot(y.astype(BF16), wg_ref[...]))
            y = y + gate * _dot((pl_ref, ps_ref)[stream][...].astype(BF16), wp_ref[...])
        o_refs[stream][...] = y
        if stream == 0:
            for src_ref, dst_ref in zip(cast_in, cast_out):
                dst_ref[...] = src_ref[...].astype(BF16)

    streams.per_stream(body)


def _cast_slab_rows(rows, steps):
    for r in range(16, rows + 1, 16):
        if rows % r == 0 and rows // r <= steps:
            return r
    raise ValueError((rows, steps))


def _ffn(x_long, x_short, wi, wo, g, b, ple=None, cast_next=()):
    streams = _TwoStreams(x_long.shape[0], x_short.shape[0])
    consts = [wi, wo, g, b]
    args = [x_long, x_short] + [c.arr for c in consts]
    specs = streams.row_specs(D_MODEL) + [c.const_spec() for c in consts]
    out_specs = streams.row_specs(D_MODEL)
    out_shape = [jax.ShapeDtypeStruct(x_long.shape, F32), jax.ShapeDtypeStruct(x_short.shape, F32)]
    if ple is not None:
        (p_long, p_short), wg, wp = ple
        args += [p_long.arr, p_short.arr, wg.arr, wp.arr]
        specs += streams.row_specs(PLE_DIM, p_long.idx, p_short.idx) + [wg.const_spec(), wp.const_spec()]
    cast_specs = []
    for w in cast_next:
        rows, cols = w.arr.shape[len(w.idx):]
        slab = _cast_slab_rows(rows, streams.long_steps)
        spec = lambda lead, idx, last=rows // slab - 1, slab=slab, cols=cols: pl.BlockSpec(
            (None,) * lead + (slab, cols), lambda t: idx + (jnp.minimum(t, last), 0))
        args.append(w.arr)
        specs.append(spec(len(w.idx), w.idx))
        cast_specs.append(spec(0, ()))
        out_shape.append(jax.ShapeDtypeStruct((rows, cols), BF16))
    out_specs += cast_specs
    return pl.pallas_call(
        functools.partial(_ffn_kernel, ple is not None, len(cast_next), streams),
        grid=streams.grid,
        in_specs=specs,
        out_specs=out_specs,
        out_shape=out_shape,
        compiler_params=_params(("arbitrary",)),
        name="ffn_ple" if ple is not None else "ffn",
    )(*args)


def _gmlp_kernel(streams, xl_ref, xs_ref, wi_ref, bi_ref, lg_ref, lb_ref, wo_ref, g_ref, b_ref,
                 wsl_ref, bsl_ref, wss_ref, bss_ref, ol_ref, os_ref, v_ref, y_ref):
    def body(stream):
        x_ref, ws_ref, bs_ref, o_ref = ((xl_ref, wsl_ref, bsl_ref, ol_ref), (xs_ref, wss_ref, bss_ref, os_ref))[stream]
        hm = streams.tm // 2
        halves = [slice(h * hm, (h + 1) * hm) for h in range(2)]
        xs = [x_ref[rows, :] for rows in halves]
        zs = [_dot(x.astype(BF16), wi_ref[...]) + bi_ref[...] for x in xs]
        us, vbs = [], []
        for rows, z in zip(halves, zs):
            z = 0.5 * z * (1.0 + lax.erf(z * (1.0 / math.sqrt(2.0))))
            v = _layer_norm(z[:, A_INNER:], lg_ref[...], lb_ref[...])
            if stream == 1:
                v_ref[rows, :] = v
            us.append(z[:, :A_INNER])
            vbs.append(v.astype(BF16))
        for half, u, vb in zip(halves, us, vbs):
            for c in range(hm // CHUNK):
                rows = slice(c * CHUNK, (c + 1) * CHUNK)
                out_rows = slice(half.start + c * CHUNK, half.start + (c + 1) * CHUNK)
                for h in range(A_HEADS):
                    cols = slice(h * A_GROUP, (h + 1) * A_GROUP)
                    mixed = _dot(ws_ref[h], vb[rows, cols]) + bs_ref[:, h:h + 1]
                    y_ref[out_rows, cols] = (u[rows, cols] * mixed).astype(BF16)
        for rows, x in zip(halves, xs):
            out = _dot(y_ref[rows, :], wo_ref[...])
            o_ref[rows, :] = _layer_norm(ALPHA * x + out, g_ref[...], b_ref[...])

    streams.per_stream(body)


def _gmlp(x_long, x_short, consts, mix_long, mix_short):
    streams = _TwoStreams(x_long.shape[0], x_short.shape[0])
    consts = consts + list(mix_long) + list(mix_short)
    return pl.pallas_call(
        functools.partial(_gmlp_kernel, streams),
        grid=streams.grid,
        in_specs=streams.row_specs(D_MODEL) + [c.const_spec() for c in consts],
        out_specs=streams.row_specs(D_MODEL) + [streams.row_specs(A_INNER)[1]],
        out_shape=[jax.ShapeDtypeStruct(x_long.shape, F32), jax.ShapeDtypeStruct(x_short.shape, F32),
                   jax.ShapeDtypeStruct((x_short.shape[0], A_INNER), F32)],
        scratch_shapes=[pltpu.VMEM((streams.tm, A_INNER), BF16)],
        compiler_params=_params(("arbitrary",)),
        name="gmlp",
    )(x_long, x_short, *[c.arr for c in consts])


def _conv_tail(x, bg, conv, wo_ref, g_ref, b_ref, o_ref):
    out = _dot((bg * conv).astype(BF16), wo_ref[...])
    o_ref[...] = _layer_norm(ALPHA * x + out, g_ref[...], b_ref[...])


def _conv_long_kernel(tiles_per_seq, x_ref, buf_ref, wi_ref, cw_ref, wo_ref, g_ref, b_ref,
                      o_ref, tail_ref, carry_ref):
    x = x_ref[...]
    tm = x.shape[0]
    h3 = _dot(x.astype(BF16), wi_ref[...])
    bg = h3[:, :D_MODEL]
    z = h3[:, D_MODEL:2 * D_MODEL] * h3[:, 2 * D_MODEL:]

    @pl.when(pl.program_id(0) % tiles_per_seq == 0)
    def _():
        carry_ref[...] = buf_ref[0]

    row = lax.broadcasted_iota(jnp.int32, (tm, D_MODEL), 0)
    prev1 = carry_ref[7:8, :]
    prev2 = carry_ref[6:7, :]
    z1 = jnp.where(row == 0, prev1, pltpu.roll(z, 1, 0))
    z2 = jnp.where(row == 0, prev2, jnp.where(row == 1, prev1, pltpu.roll(z, 2, 0)))
    conv = cw_ref[0:1, :] * z2 + cw_ref[1:2, :] * z1 + cw_ref[2:3, :] * z
    tail = z[tm - 8:, :]
    carry_ref[...] = tail
    tail_ref[0] = tail
    _conv_tail(x, bg, conv, wo_ref, g_ref, b_ref, o_ref)


def _conv_short_kernel(seq, x_ref, h1_ref, h2_ref, wi_ref, cw_ref, wo_ref, g_ref, b_ref,
                       o_ref, z_ref):
    x = x_ref[...]
    tm = x.shape[0]
    h3 = _dot(x.astype(BF16), wi_ref[...])
    bg = h3[:, :D_MODEL]
    z = h3[:, D_MODEL:2 * D_MODEL] * h3[:, 2 * D_MODEL:]
    t = lax.broadcasted_iota(jnp.int32, (tm, D_MODEL), 0) % seq
    z1 = jnp.where(t >= 1, pltpu.roll(z, 1, 0), h1_ref[...])
    z2 = jnp.where(t >= 2, pltpu.roll(z, 2, 0), h2_ref[...])
    conv = cw_ref[0:1, :] * z2 + cw_ref[1:2, :] * z1 + cw_ref[2:3, :] * z
    z_ref[...] = z
    _conv_tail(x, bg, conv, wo_ref, g_ref, b_ref, o_ref)


def _first_rows(state_rows, seq):
    bn = state_rows.shape[0]
    out = jnp.zeros((bn, seq, D_MODEL), F32).at[:, 0, :].set(state_rows)
    return out.reshape(bn * seq, D_MODEL)


def _conv_mixer(x, buf_prev, seq, consts):
    n = x.shape[0]
    bn = n // seq
    tm = min(ROW_TILE, n)
    const_specs = [c.const_spec() for c in consts]
    const_args = [c.arr for c in consts]
    if seq >= tm:
        tiles_per_seq = seq // tm
        buf8 = jnp.concatenate([jnp.zeros((bn, 6, D_MODEL), F32), buf_prev], axis=1)
        out, tails = pl.pallas_call(
            functools.partial(_conv_long_kernel, tiles_per_seq),
            grid=(n // tm,),
            in_specs=[_row_spec(D_MODEL, tm),
                      pl.BlockSpec((1, 8, D_MODEL), lambda i: (i // tiles_per_seq, 0, 0))] + const_specs,
            out_specs=[_row_spec(D_MODEL, tm), pl.BlockSpec((1, 8, D_MODEL), lambda i: (i, 0, 0))],
            out_shape=[jax.ShapeDtypeStruct((n, D_MODEL), F32),
                       jax.ShapeDtypeStruct((n // tm, 8, D_MODEL), F32)],
            scratch_shapes=[pltpu.VMEM((8, D_MODEL), F32)],
            compiler_params=_params(("arbitrary",)),
            name="conv_long",
        )(x, buf8, *const_args)
        new_buf = tails[tiles_per_seq - 1::tiles_per_seq, 6:8, :]
        return out, new_buf
    assert tm % seq == 0 and seq >= CONV_W - 1
    h1 = _first_rows(buf_prev[:, 1, :], seq)
    h2 = _first_rows(buf_prev[:, 0, :], seq) + jnp.roll(h1, 1, axis=0)
    out, z = pl.pallas_call(
        functools.partial(_conv_short_kernel, seq),
        grid=(n // tm,),
        in_specs=[_row_spec(D_MODEL, tm)] * 3 + const_specs,
        out_specs=[_row_spec(D_MODEL, tm)] * 2,
        out_shape=[jax.ShapeDtypeStruct((n, D_MODEL), F32)] * 2,
        compiler_params=_params(("parallel",)),
        name="conv_short",
    )(x, h1, h2, *const_args)
    new_buf = z.reshape(bn, seq, D_MODEL)[:, seq - (CONV_W - 1):, :]
    return out, new_buf


def _head_sum_bcast(x, e_ref, et_ref):
    s = _dot(x.astype(BF16), e_ref[...])
    return _dot(jnp.concatenate(_split2(s), axis=1), et_ref[...])


def _rwkv_proj_body(x, xprev, mu_ref, wr_ref, wk_ref, wv_ref, w0_ref, w1_ref, w2_ref, a0_ref, a1_ref,
                    a2_ref, g1_ref, g2_ref, kk_ref, ka_ref, rk_ref, e_ref, et_ref,
                    r_out, lw_out, k_out, v_out, c_out, b_out, bonus_out, g_out):
    xx = xprev - x
    mix = lambda i: (x + xx * mu_ref[i:i + 1, :]).astype(BF16)
    r = _dot(mix(0), wr_ref[...])
    k = _dot(mix(2), wk_ref[...])
    v = _dot(mix(3), wv_ref[...])
    zw = w0_ref[...] + _dot(jnp.tanh(_dot(mix(1), w1_ref[...])).astype(BF16), w2_ref[...])
    lw_out[...] = -_sigmoid(zw) * math.exp(-0.5)
    a = _sigmoid(a0_ref[...] + _dot(_dot(mix(4), a1_ref[...]).astype(BF16), a2_ref[...]))
    g_out[...] = _dot(_sigmoid(_dot(mix(5), g1_ref[...])).astype(BF16), g2_ref[...]).astype(g_out.dtype)
    kk = k * kk_ref[...]
    norm = jnp.sqrt(_head_sum_bcast(kk * kk, e_ref, et_ref))
    c = kk / jnp.maximum(norm, 1e-12)
    kmod = k * (1.0 + (a - 1.0) * ka_ref[...])
    r_out[...] = r
    k_out[...] = kmod
    v_out[...] = v
    c_out[...] = c
    b_out[...] = c * a
    bonus_out[...] = (_head_sum_bcast(r * kmod * rk_ref[...], e_ref, et_ref) * v).astype(bonus_out.dtype)


def _rwkv_proj_long_kernel(tiles_per_seq, x_ref, shift_ref, *rest):
    carry_ref = rest[-1]
    x = x_ref[...]
    tm = x.shape[0]

    @pl.when(pl.program_id(0) % tiles_per_seq == 0)
    def _():
        carry_ref[...] = shift_ref[...]

    row = lax.broadcasted_iota(jnp.int32, (tm, D_MODEL), 0)
    xprev = jnp.where(row == 0, carry_ref[...], pltpu.roll(x, 1, 0))
    carry_ref[...] = x[tm - 1:tm, :]
    _rwkv_proj_body(x, xprev, *rest[:-1])


def _rwkv_proj_short_kernel(seq, x_ref, h1_ref, *rest):
    x = x_ref[...]
    t = lax.broadcasted_iota(jnp.int32, x.shape, 0) % seq
    xprev = jnp.where(t >= 1, pltpu.roll(x, 1, 0), h1_ref[...])
    _rwkv_proj_body(x, xprev, *rest)


def _rwkv_proj(x, shift_prev, seq, consts):
    n = x.shape[0]
    tm = min(ROW_TILE // 2, n)
    common = dict(
        grid=(n // tm,),
        out_specs=[_row_spec(D_MODEL, tm)] * 8,
        out_shape=[jax.ShapeDtypeStruct((n, D_MODEL), F32)] * 6 + [jax.ShapeDtypeStruct((n, D_MODEL), BF16)] * 2,
    )
    const_specs = [c.const_spec() for c in consts]
    const_args = [c.arr for c in consts]
    if seq >= tm:
        tiles_per_seq = seq // tm
        return pl.pallas_call(
            functools.partial(_rwkv_proj_long_kernel, tiles_per_seq),
            in_specs=[_row_spec(D_MODEL, tm),
                      pl.BlockSpec((None, 1, D_MODEL), lambda i: (i // tiles_per_seq, 0, 0))] + const_specs,
            scratch_shapes=[pltpu.VMEM((1, D_MODEL), F32)],
            compiler_params=_params(("arbitrary",)),
            name="rwkv_proj_long", **common,
        )(x, shift_prev[:, None, :], *const_args)
    assert tm % seq == 0
    return pl.pallas_call(
        functools.partial(_rwkv_proj_short_kernel, seq),
        in_specs=[_row_spec(D_MODEL, tm)] * 2 + const_specs,
        compiler_params=_params(("parallel",)),
        name="rwkv_proj_short", **common,
    )(x, _first_rows(shift_prev, seq), *const_args)


def _pair_rows(x, lane_lo):
    zero = jnp.zeros_like(x)
    return jnp.concatenate([jnp.where(lane_lo, x, zero), jnp.where(lane_lo, zero, x)], axis=0)


def _block_diag(a, b):
    zero = jnp.zeros_like(a)
    return jnp.concatenate([jnp.concatenate([a, zero], axis=1), jnp.concatenate([zero, b], axis=1)], axis=0)


def _scan_chunk(load, store_y, states, state_fn):
    C = SCAN_CHUNK
    pairs = range(len(states))
    row = lax.broadcasted_iota(jnp.int32, (C, 2 * C), 0)
    col = lax.broadcasted_iota(jnp.int32, (C, 2 * C), 1) & (C - 1)
    strict = row > col
    incl = row >= col
    lane_lo = lax.broadcasted_iota(jnp.int32, (C, LANES), 1) < HEAD_SIZE
    tri = jnp.where(lax.broadcasted_iota(jnp.int32, (C, C), 0) >= lax.broadcasted_iota(jnp.int32, (C, C), 1),
                    1.0, 0.0).astype(BF16)
    r, lw, k, v, c, b = zip(*[load(p) for p in pairs])

    cum = []
    for p in pairs:
        parts = _dot(tri, jnp.concatenate(_split3(lw[p]), axis=1))
        cum.append(parts[:, :LANES] + (parts[:, LANES:2 * LANES] + parts[:, 2 * LANES:]))
    p_inc = [jnp.exp(cum[p]) for p in pairs]
    p_inv = [jnp.exp(-cum[p]) for p in pairs]
    p_exc = [jnp.exp(cum[p] - lw[p]) for p in pairs]
    q = [jnp.concatenate([c[p] * p_exc[p], r[p] * p_inc[p]], axis=0).astype(BF16) for p in pairs]
    bt = [(b[p] * p_inv[p]).astype(BF16) for p in pairs]
    kt = [(k[p] * p_inv[p]).astype(BF16) for p in pairs]
    vb = [v[p].astype(BF16) for p in pairs]
    kb = [jnp.concatenate([_pair_rows(bt[p], lane_lo), _pair_rows(kt[p], lane_lo)], axis=0) for p in pairs]
    gram = [_dot(q[p], jnp.concatenate([kb[p], states[p]], axis=0), NT) for p in pairs]
    l_cb = [jnp.where(strict, gram[p][:C, :2 * C], 0.0) for p in pairs]
    l_ck = [jnp.where(strict, gram[p][:C, 2 * C:4 * C], 0.0).astype(BF16) for p in pairs]
    a_rb = [jnp.where(incl, gram[p][C:, :2 * C], 0.0) for p in pairs]
    a_rk = [jnp.where(incl, gram[p][C:, 2 * C:4 * C], 0.0) for p in pairs]
    qs = [gram[p][:, 4 * C:] for p in pairs]
    v_rows = [_pair_rows(vb[p], lane_lo) for p in pairs]
    u = [-(qs[p][:C] + _dot(l_ck[p], v_rows[p])) for p in pairs]
    m = [l_cb[p].astype(BF16) for p in pairs]
    for level in range(int(math.log2(C)) - 1):
        both = [_dot(m[p], jnp.concatenate([_pair_rows(m[p], lane_lo), _pair_rows(u[p].astype(BF16), lane_lo)],
                                           axis=1)) for p in pairs]
        if level == 0:
            u = [u[p] - both[p][:, 2 * C:] for p in pairs]
        else:
            u = [u[p] + both[p][:, 2 * C:] for p in pairs]
        m = [both[p][:, :2 * C].astype(BF16) for p in pairs]
    u = [u[p] + _dot(m[p], _pair_rows(u[p].astype(BF16), lane_lo)) for p in pairs]
    ub = [u[p].astype(BF16) for p in pairs]
    for p in pairs:
        a = jnp.concatenate([a_rb[p], a_rk[p]], axis=1).astype(BF16)
        uv_rows = jnp.concatenate([_pair_rows(ub[p], lane_lo), v_rows[p]], axis=0)
        store_y(p, qs[p][C:] + _dot(a, uv_rows))
    state_fn(u, v, ub, vb, bt, kt, p_inc)


def _pair_lanes(p):
    return slice(p * LANES, (p + 1) * LANES)


def _diag_blocks_mask():
    sq_row = lax.broadcasted_iota(jnp.int32, (LANES, LANES), 0) < HEAD_SIZE
    sq_col = lax.broadcasted_iota(jnp.int32, (LANES, LANES), 1) < HEAD_SIZE
    return sq_row == sq_col


def _store_pair_state(out_ref, i, p, s_pair):
    out_ref[i, 2 * p] = s_pair[:HEAD_SIZE, :HEAD_SIZE]
    out_ref[i, 2 * p + 1] = s_pair[HEAD_SIZE:, HEAD_SIZE:]


SCAN_SEQS = 8


def _rwkv_scan_kernel(r_ref, lw_ref, k_ref, v_ref, c_ref, b_ref, s0_ref, y_ref, sfin_ref, s_ref):
    C = SCAN_CHUNK
    j = pl.program_id(1)
    chains = [(s, p) for s in range(SCAN_SEQS) for p in range(HEAD_PAIRS)]

    @pl.when(j == 0)
    def _():
        for s, p in chains:
            s_ref[s, p] = _block_diag(s0_ref[s, 2 * p], s0_ref[s, 2 * p + 1])

    diag = _diag_blocks_mask()
    refs = (r_ref, lw_ref, k_ref, v_ref, c_ref, b_ref)

    def load(ch):
        s, p = chains[ch]
        return tuple(ref[s, :, _pair_lanes(p)] for ref in refs)

    def store_y(ch, y):
        s, p = chains[ch]
        y_ref[s, :, _pair_lanes(p)] = y

    def state_fn(u, v, ub, vb, bt, kt, p_inc):
        for ch, (s, p) in enumerate(chains):
            upd = _dot(jnp.concatenate([ub[ch], vb[ch]], axis=0), jnp.concatenate([bt[ch], kt[ch]], axis=0), TN)
            s_ref[s, p] = (s_ref[s, p] + jnp.where(diag, upd, 0.0)) * p_inc[ch][C - 1:C, :]

    _scan_chunk(load, store_y, [s_ref[s, p].astype(BF16) for s, p in chains], state_fn)

    @pl.when(j == pl.num_programs(1) - 1)
    def _():
        for s, p in chains:
            _store_pair_state(sfin_ref, s, p, s_ref[s, p])


def _rwkv_scan(r, lw, k, v, c, b, s0, bn, seq):
    assert bn % SCAN_SEQS == 0
    row_spec = pl.BlockSpec((SCAN_SEQS, SCAN_CHUNK, D_MODEL), lambda i, j: (i, j, 0))
    st_spec = pl.BlockSpec((SCAN_SEQS, B_HEADS, HEAD_SIZE, HEAD_SIZE), lambda i, j: (i, 0, 0, 0))
    as_seqs = lambda z: z.reshape(bn, seq, D_MODEL)
    ys, s_new = pl.pallas_call(
        _rwkv_scan_kernel,
        grid=(bn // SCAN_SEQS, seq // SCAN_CHUNK),
        in_specs=[row_spec] * 6 + [st_spec],
        out_specs=[row_spec, st_spec],
        out_shape=[jax.ShapeDtypeStruct((bn, seq, D_MODEL), F32),
                   jax.ShapeDtypeStruct((bn, B_HEADS, HEAD_SIZE, HEAD_SIZE), F32)],
        scratch_shapes=[pltpu.VMEM((SCAN_SEQS, HEAD_PAIRS, LANES, LANES), F32)],
        compiler_params=_params(("parallel", "arbitrary")),
        name="rwkv_scan",
    )(*(as_seqs(z) for z in (r, lw, k, v, c, b)), s0)
    return ys.reshape(bn * seq, D_MODEL), s_new


def _rwkv_scan_lanes_kernel(seq, r_ref, lw_ref, k_ref, v_ref, c_ref, b_ref, s0_ref, y_ref, sfin_ref,
                            cols_ref, yt_ref):
    nb = s0_ref.shape[-1]
    for a, ref in enumerate((r_ref, lw_ref, k_ref, v_ref, c_ref, b_ref)):
        for t in range(seq):
            col = ref[pl.ds(t, nb, stride=seq), :].T
            cols_ref[a, t] = jnp.exp(col) if ref is lw_ref else col
    for hh in range(2):
        feat = slice(hh * HEAD_SIZE, (hh + 1) * HEAD_SIZE)
        for vi in range(HEAD_SIZE):
            row = hh * HEAD_SIZE + vi
            s = s0_ref[hh, vi]
            for t in range(seq):
                r, w, k, c, b = (cols_ref[a, t, feat, :] for a in (0, 1, 2, 4, 5))
                sa = jnp.sum(s * c, axis=0, keepdims=True)
                s = s * w - sa * b + cols_ref[3, t, row:row + 1, :] * k
                yt_ref[t, row:row + 1, :] = jnp.sum(s * r, axis=0, keepdims=True)
            sfin_ref[hh, vi] = s
    for t in range(seq):
        y_ref[pl.ds(t, nb, stride=seq), :] = yt_ref[t].T


def _rwkv_scan_lanes(r, lw, k, v, c, b, s0, bn, seq):
    assert bn == LANES
    n = bn * seq
    row_spec = pl.BlockSpec((n, LANES), lambda p: (0, p))
    st_spec = pl.BlockSpec((2, HEAD_SIZE, HEAD_SIZE, bn), lambda p: (p, 0, 0, 0))
    ys, s_new = pl.pallas_call(
        functools.partial(_rwkv_scan_lanes_kernel, seq),
        grid=(HEAD_PAIRS,),
        in_specs=[row_spec] * 6 + [st_spec],
        out_specs=[row_spec, st_spec],
        out_shape=[jax.ShapeDtypeStruct((n, D_MODEL), F32),
                   jax.ShapeDtypeStruct((B_HEADS, HEAD_SIZE, HEAD_SIZE, bn), F32)],
        scratch_shapes=[pltpu.VMEM((6, seq, LANES, bn), F32), pltpu.VMEM((seq, LANES, bn), F32)],
        compiler_params=_params(("parallel",)),
        name="rwkv_scan_lanes",
    )(r, lw, k, v, c, b, jnp.transpose(s0, (1, 2, 3, 0)))
    return ys, jnp.transpose(s_new, (3, 0, 1, 2))


def _rwkv_post_kernel(x_ref, y_ref, bonus_ref, gate_ref, lg_ref, lb_ref, wo_ref, e_ref, et_ref,
                      g_ref, b_ref, o_ref):
    ys = y_ref[...]
    inv = 1.0 / HEAD_SIZE
    m = _head_sum_bcast(ys, e_ref, et_ref) * inv
    yc = ys - m
    var = _head_sum_bcast(yc * yc, e_ref, et_ref) * inv
    yn = yc * lax.rsqrt(var + GN_EPS) * lg_ref[...] + lb_ref[...]
    out = _dot(((yn + bonus_ref[...]) * gate_ref[...]).astype(BF16), wo_ref[...])
    o_ref[...] = _layer_norm(ALPHA * x_ref[...] + out, g_ref[...], b_ref[...])


def _rwkv_post(x, ys, bonus, gate, consts):
    n = x.shape[0]
    tm = min(ROW_TILE, n)
    return pl.pallas_call(
        _rwkv_post_kernel,
        grid=(n // tm,),
        in_specs=[_row_spec(D_MODEL, tm)] * 4 + [c.const_spec() for c in consts],
        out_specs=_row_spec(D_MODEL, tm),
        out_shape=jax.ShapeDtypeStruct((n, D_MODEL), F32),
        compiler_params=_params(("parallel",)),
        name="rwkv_post",
    )(x, ys, bonus, gate, *[c.arr for c in consts])


def _rwkv_mixer(x, shift_prev, s0, seq, w, g, b):
    n = x.shape[0]
    bn = n // seq
    r, lw, k, v, c, bb, bonus, gate = _rwkv_proj(x, shift_prev, seq, w["proj"])
    scan = _rwkv_scan if seq % SCAN_CHUNK == 0 else _rwkv_scan_lanes
    ys, s_new = scan(r, lw, k, v, c, bb, s0, bn, seq)
    out = _rwkv_post(x, ys, bonus, gate, w["post"] + [g, b])
    return out, x.reshape(bn, seq, D_MODEL)[:, -1, :], s_new


def _run_trunks(x_long, x_short, p_long, p_short, states_long, states_short, W):
    shapes = [x_long.shape, x_short.shape]
    seqs = [s[1] for s in shapes]
    assert seqs[0] >= CHUNK > seqs[1]
    xs = [x_long.reshape(-1, D_MODEL), x_short.reshape(-1, D_MODEL)]
    ps = [p_long.reshape(DEPTH, -1, PLE_DIM), p_short.reshape(DEPTH, -1, PLE_DIM)]
    states = [states_long, states_short]
    new_v, new_wkv, new_shift, new_conv = [], ([], []), ([], []), ([], [])
    ffn_w = [_sel(W["ffn_w_in"][0, 0].astype(BF16)), _sel(W["ffn_w_out"][0, 0].astype(BF16))]

    def mixer_mats(i):
        j, kind = divmod(i, N_MIXERS)
        if kind == 0:
            return [_sel(W["a_w_in"], j), _sel(W["a_w_out"], j)]
        if kind == 1:
            return [_sel(W["b_w_rkv"], j, m) for m in range(3)] + [_sel(W["b_w_o"], j)]
        return [_sel(W["c_w_in"], j), _sel(W["c_w_out"], j)]

    def ffn_pair(xs, ffn_w, i, s, cast_next, ple=None):
        norm = (_sel(W["ln_g"], i, 2 * s), _sel(W["ln_b"], i, 2 * s))
        if ple is not None:
            ple = ((_sel(ps[0], i), _sel(ps[1], i)),) + tuple(ple)
        res = _ffn(xs[0], xs[1], *ffn_w, *norm, ple=ple, cast_next=cast_next)
        return list(res[:2]), [_sel(a) for a in res[2:]]

    for i in range(DEPTH):
        j, kind = divmod(i, N_MIXERS)
        ln_g = lambda s: _sel(W["ln_g"], i, s)
        ln_b = lambda s: _sel(W["ln_b"], i, s)
        second = [_sel(W[name], i, 1) for name in ("ffn_w_in", "ffn_w_out")] + \
                 [_sel(W[name], i) for name in ("ple_w_gate", "ple_w_proj")]
        xs, cast = ffn_pair(xs, ffn_w, i, 0, second + mixer_mats(i))
        ffn_w, ple_w, mats = cast[:2], cast[2:4], cast[4:]
        if kind == 0:
            consts = [mats[0]] + [_sel(W[name], j) for name in ("a_b_in", "a_ln_g", "a_ln_b")] + [mats[1]]
            x_l, x_s, v = _gmlp(*xs, consts + [ln_g(1), ln_b(1)], [_sel(m, j) for m in W["a_mix_long"]],
                                [_sel(m, j) for m in W["a_mix_short"]])
            xs = [x_l, x_s]
            new_v.append(v.reshape(shapes[1][0], seqs[1], A_INNER))
        elif kind == 1:
            proj = W["b_proj"]
            w = dict(proj=[_sel(proj[0], j)] + mats[:3] + [_sel(a, j) for a in proj[1:]] + [_sel(W["e"]), _sel(W["et"])],
                     post=[_sel(a, j) for a in W["b_post"]] + [mats[3], _sel(W["e"]), _sel(W["et"])])
            for t in range(2):
                wkv, shift, _ = states[t]
                xs[t], sh, s = _rwkv_mixer(xs[t], shift[j], wkv[j], seqs[t], w, ln_g(1), ln_b(1))
                new_shift[t].append(sh)
                new_wkv[t].append(s)
        else:
            consts = [mats[0], _sel(W["c_conv_w"], j), mats[1], ln_g(1), ln_b(1)]
            for t in range(2):
                xs[t], buf = _conv_mixer(xs[t], states[t][2][j], seqs[t], consts)
                new_conv[t].append(buf)
        first_next = [_sel(W[name], i + 1, 0) for name in ("ffn_w_in", "ffn_w_out")] if i + 1 < DEPTH else []
        xs, ffn_w = ffn_pair(xs, ffn_w, i, 1, first_next, ple=ple_w)
    outs = [(xs[t].reshape(shapes[t]), jnp.stack(new_wkv[t]), jnp.stack(new_shift[t]), jnp.stack(new_conv[t]))
            for t in range(2)]
    return outs[0], outs[1], jnp.stack(new_v)


def _gmlp_mix_mats(w_s, b_s, seq):
    l = min(seq, CHUNK)
    ws = jnp.where(jnp.tril(jnp.ones((l, l), dtype=bool)), w_s[..., :l, :l], 0.0)
    reps = CHUNK // l
    if reps > 1:
        pos = jnp.arange(CHUNK) // l
        ws = jnp.where(pos[:, None] == pos[None, :], jnp.tile(ws, (1, 1, reps, reps)), 0.0)
    bias = jnp.tile(jnp.swapaxes(b_s[..., :l], -1, -2), (1, reps, 1))
    return ws.astype(BF16), bias


def kernel(x_prompt, x_sample, state_b_wkv, state_b_shift, state_c_conv, p_prompt, p_sample, ln_g, ln_b, ffn_w_in, ffn_w_out, ple_w_gate, ple_w_proj, a_w_in, a_b_in, a_ln_g, a_ln_b, a_w_s, a_b_s, a_w_out, b_mu, b_w_rkv, b_w0, b_w1, b_w2, b_a0, b_a1, b_a2, b_g1, b_g2, b_k_k, b_k_a, b_r_k, b_lnx_g, b_lnx_b, b_w_o, c_w_in, c_conv_w, c_w_out):
    bf = lambda w: w.astype(BF16)
    row = lambda w: w.reshape(w.shape[0], 1, -1)
    head_of_lane = jnp.arange(D_MODEL) // HEAD_SIZE
    e = (head_of_lane[:, None] == jnp.arange(LANES)[None, :]).astype(BF16)
    W = dict(
        ln_g=ln_g[:, :, None, :], ln_b=ln_b[:, :, None, :],
        ffn_w_in=ffn_w_in, ffn_w_out=ffn_w_out,
        ple_w_gate=ple_w_gate, ple_w_proj=ple_w_proj,
        a_w_in=a_w_in, a_b_in=row(a_b_in), a_ln_g=row(a_ln_g), a_ln_b=row(a_ln_b), a_w_out=a_w_out,
        a_mix_long=_gmlp_mix_mats(a_w_s, a_b_s, x_prompt.shape[1]),
        a_mix_short=_gmlp_mix_mats(a_w_s, a_b_s, x_sample.shape[1]),
        c_w_in=c_w_in, c_conv_w=c_conv_w, c_w_out=c_w_out,
        b_w_rkv=b_w_rkv, b_w_o=b_w_o,
        b_proj=[b_mu, row(b_w0), bf(b_w1), bf(b_w2), row(b_a0), bf(b_a1), bf(b_a2),
                bf(b_g1), bf(b_g2), row(b_k_k), row(b_k_a), row(b_r_k)],
        b_post=[row(b_lnx_g), row(b_lnx_b)],
        e=e, et=jnp.concatenate([e.T, e.T], axis=0),
    )
    bp = x_prompt.shape[0]
    n_b = b_mu.shape[0]
    n_c = c_w_in.shape[0]
    zero_wkv = jnp.zeros((n_b, bp) + state_b_wkv.shape[2:], state_b_wkv.dtype)
    zero_shift = jnp.zeros((n_b, bp, D_MODEL), state_b_shift.dtype)
    zero_conv = jnp.zeros((n_c, bp, CONV_W - 1, D_MODEL), state_c_conv.dtype)
    (y_p, wkv_p, shift_p, conv_p), (y_s, wkv_s, shift_s, conv_s), a_v_s = _run_trunks(
        x_prompt, x_sample, p_prompt, p_sample, (zero_wkv, zero_shift, zero_conv),
        (state_b_wkv, state_b_shift, state_c_conv), W)
    return (y_p, y_s, a_v_s, wkv_p, shift_p, conv_p, wkv_s, shift_s, conv_s)
```

```python
import collections
import functools
import math

import jax
import jax.numpy as jnp
from jax import lax
from jax.experimental import pallas as pl
from jax.experimental.pallas import tpu as pltpu

F32 = jnp.float32
BF16 = jnp.bfloat16

D_MODEL = 1024
DEPTH = 4
N_MIXERS = 3
CHUNK = 128
A_INNER = 2 * D_MODEL
A_HEADS = 8
A_GROUP = A_INNER // A_HEADS
HEAD_SIZE = 64
B_HEADS = D_MODEL // HEAD_SIZE
CONV_W = 3
D_FF = 2816
PLE_DIM = 256
ALPHA = (2 * DEPTH) ** 0.25
LN_EPS = 1e-5
GN_EPS = 64e-5

LANES = 128
MXU_DIM = 256
ROW_TILE = 512
SCAN_CHUNK = 64
HEAD_PAIRS = B_HEADS // 2
VMEM_LIMIT = 56 * 1024 * 1024

NN = (((1,), (0,)), ((), ()))
NT = (((1,), (1,)), ((), ()))
TN = (((0,), (0,)), ((), ()))


def _dot(a, b, dims=NN):
    return lax.dot_general(a, b, dims, preferred_element_type=F32)


def _split2(x):
    hi = x.astype(BF16)
    lo = (x - hi.astype(F32)).astype(BF16)
    return hi, lo


def _split3(x):
    hi = x.astype(BF16)
    r1 = x - hi.astype(F32)
    mid = r1.astype(BF16)
    lo = (r1 - mid.astype(F32)).astype(BF16)
    return hi, mid, lo


def _layer_norm(x, g, b, eps=LN_EPS):
    mu = jnp.mean(x, axis=-1, keepdims=True)
    xc = x - mu
    var = jnp.mean(xc * xc, axis=-1, keepdims=True)
    return xc * lax.rsqrt(var + eps) * g + b


def _sigmoid(x):
    return 1.0 / (1.0 + jnp.exp(-x))


class _Sel(collections.namedtuple("_Sel", ["arr", "idx"])):
    def const_spec(self):
        k = len(self.idx)
        rest = self.arr.shape[k:]
        idx = self.idx
        return pl.BlockSpec((None,) * k + rest, lambda *_: idx + (0,) * len(rest),
                            pipeline_mode=pl.Buffered(1))

    def row_spec(self, tm):
        k = len(self.idx)
        idx = self.idx
        return pl.BlockSpec((None,) * k + (tm, self.arr.shape[-1]), lambda i: idx + (i, 0))


def _sel(arr, *idx):
    return _Sel(arr, tuple(idx))


def _row_spec(width, tm=ROW_TILE):
    return pl.BlockSpec((tm, width), lambda i: (i, 0))


def _params(sem):
    return pltpu.CompilerParams(dimension_semantics=sem, vmem_limit_bytes=VMEM_LIMIT)


FF_BLOCKS = ((0, 6 * MXU_DIM), (6 * MXU_DIM, D_FF))
assert D_FF % MXU_DIM == 0


class _TwoStreams:
    def __init__(self, n_long, n_short, tm=ROW_TILE):
        assert n_long % tm == 0 and n_short % tm == 0
        self.tm = tm
        self.long_steps = n_long // tm
        self.grid = (self.long_steps + n_short // tm,)

    def long_step(self, t):
        return jnp.minimum(t, self.long_steps - 1)

    def short_step(self, t):
        return jnp.maximum(t - self.long_steps, 0)

    def row_specs(self, width, idx_long=(), idx_short=()):
        lead_l, lead_s = (None,) * len(idx_long), (None,) * len(idx_short)
        return [pl.BlockSpec(lead_l + (self.tm, width), lambda t: idx_long + (self.long_step(t), 0)),
                pl.BlockSpec(lead_s + (self.tm, width), lambda t: idx_short + (self.short_step(t), 0))]

    def per_stream(self, body):
        is_long = pl.program_id(0) < self.long_steps
        pl.when(is_long)(lambda: body(0))
        pl.when(jnp.logical_not(is_long))(lambda: body(1))


def _ffn_kernel(with_ple, n_cast, streams, xl_ref, xs_ref, wi_ref, wo_ref, g_ref, b_ref, *rest):
    rest = list(rest)
    if with_ple:
        pl_ref, ps_ref, wg_ref, wp_ref = rest[:4]
        del rest[:4]
    cast_in = rest[:n_cast]
    del rest[:n_cast]
    o_refs, cast_out = rest[:2], rest[2:]

    def body(stream):
        x = (xl_ref, xs_ref)[stream][...]
        xb = x.astype(BF16)
        acc = None
        for lo, hi in FF_BLOCKS:
            gate = _dot(xb, wi_ref[:, lo:hi])
            up = _dot(xb, wi_ref[:, D_FF + lo:D_FF + hi])
            act = (gate * _sigmoid(gate) * up).astype(BF16)
            part = _dot(act, wo_ref[lo:hi, :])
            acc = part if acc is None else acc + part
        y = _layer_norm(ALPHA * x + 0.5 * acc, g_ref[...], b_ref[...])
        if with_ple:
            gate = _sigmoid(_dot(y.astype(BF16), wg_ref[...]))
            y = y + gate * _dot((pl_ref, ps_ref)[stream][...].astype(BF16), wp_ref[...])
        o_refs[stream][...] = y
        if stream == 0:
            for src_ref, dst_ref in zip(cast_in, cast_out):
                dst_ref[...] = src_ref[...].astype(BF16)

    streams.per_stream(body)


def _cast_slab_rows(rows, steps):
    for r in range(16, rows + 1, 16):
        if rows % r == 0 and rows // r <= steps:
            return r
    raise ValueError((rows, steps))


def _ffn(x_long, x_short, wi, wo, g, b, ple=None, cast_next=()):
    streams = _TwoStreams(x_long.shape[0], x_short.shape[0])
    consts = [wi, wo, g, b]
    args = [x_long, x_short] + [c.arr for c in consts]
    specs = streams.row_specs(D_MODEL) + [c.const_spec() for c in consts]
    out_specs = streams.row_specs(D_MODEL)
    out_shape = [jax.ShapeDtypeStruct(x_long.shape, F32), jax.ShapeDtypeStruct(x_short.shape, F32)]
    if ple is not None:
        (p_long, p_short), wg, wp = ple
        args += [p_long.arr, p_short.arr, wg.arr, wp.arr]
        specs += streams.row_specs(PLE_DIM, p_long.idx, p_short.idx) + [wg.const_spec(), wp.const_spec()]
    cast_specs = []
    for w in cast_next:
        rows, cols = w.arr.shape[len(w.idx):]
        slab = _cast_slab_rows(rows, streams.long_steps)
        spec = lambda lead, idx, last=rows // slab - 1, slab=slab, cols=cols: pl.BlockSpec(
            (None,) * lead + (slab, cols), lambda t: idx + (jnp.minimum(t, last), 0))
        args.append(w.arr)
        specs.append(spec(len(w.idx), w.idx))
        cast_specs.append(spec(0, ()))
        out_shape.append(jax.ShapeDtypeStruct((rows, cols), BF16))
    out_specs += cast_specs
    return pl.pallas_call(
        functools.partial(_ffn_kernel, ple is not None, len(cast_next), streams),
        grid=streams.grid,
        in_specs=specs,
        out_specs=out_specs,
        out_shape=out_shape,
        compiler_params=_params(("arbitrary",)),
        name="ffn_ple" if ple is not None else "ffn",
    )(*args)


def _gmlp_kernel(streams, xl_ref, xs_ref, wi_ref, bi_ref, lg_ref, lb_ref, wo_ref, g_ref, b_ref,
                 wsl_ref, bsl_ref, wss_ref, bss_ref, ol_ref, os_ref, v_ref, y_ref):
    def body(stream):
        x_ref, ws_ref, bs_ref, o_ref = ((xl_ref, wsl_ref, bsl_ref, ol_ref), (xs_ref, wss_ref, bss_ref, os_ref))[stream]
        hm = streams.tm // 2
        halves = [slice(h * hm, (h + 1) * hm) for h in range(2)]
        xs = [x_ref[rows, :] for rows in halves]
        zs = [_dot(x.astype(BF16), wi_ref[...]) + bi_ref[...] for x in xs]
        us, vbs = [], []
        for rows, z in zip(halves, zs):
            z = 0.5 * z * (1.0 + lax.erf(z * (1.0 / math.sqrt(2.0))))
            v = _layer_norm(z[:, A_INNER:], lg_ref[...], lb_ref[...])
            if stream == 1:
                v_ref[rows, :] = v
            us.append(z[:, :A_INNER])
            vbs.append(v.astype(BF16))
        for half, u, vb in zip(halves, us, vbs):
            for c in range(hm // CHUNK):
                rows = slice(c * CHUNK, (c + 1) * CHUNK)
                out_rows = slice(half.start + c * CHUNK, half.start + (c + 1) * CHUNK)
                for h in range(A_HEADS):
                    cols = slice(h * A_GROUP, (h + 1) * A_GROUP)
                    mixed = _dot(ws_ref[h], vb[rows, cols]) + bs_ref[:, h:h + 1]
                    y_ref[out_rows, cols] = (u[rows, cols] * mixed).astype(BF16)
        for rows, x in zip(halves, xs):
            out = _dot(y_ref[rows, :], wo_ref[...])
            o_ref[rows, :] = _layer_norm(ALPHA * x + out, g_ref[...], b_ref[...])

    streams.per_stream(body)


def _gmlp(x_long, x_short, consts, mix_long, mix_short):
    streams = _TwoStreams(x_long.shape[0], x_short.shape[0])
    consts = consts + list(mix_long) + list(mix_short)
    return pl.pallas_call(
        functools.partial(_gmlp_kernel, streams),
        grid=streams.grid,
        in_specs=streams.row_specs(D_MODEL) + [c.const_spec() for c in consts],
        out_specs=streams.row_specs(D_MODEL) + [streams.row_specs(A_INNER)[1]],
        out_shape=[jax.ShapeDtypeStruct(x_long.shape, F32), jax.ShapeDtypeStruct(x_short.shape, F32),
                   jax.ShapeDtypeStruct((x_short.shape[0], A_INNER), F32)],
        scratch_shapes=[pltpu.VMEM((streams.tm, A_INNER), BF16)],
        compiler_params=_params(("arbitrary",)),
        name="gmlp",
    )(x_long, x_short, *[c.arr for c in consts])


def _conv_tail(x, bg, conv, wo_ref, g_ref, b_ref, o_ref):
    out = _dot((bg * conv).astype(BF16), wo_ref[...])
    o_ref[...] = _layer_norm(ALPHA * x + out, g_ref[...], b_ref[...])


def _conv_long_kernel(tiles_per_seq, x_ref, buf_ref, wi_ref, cw_ref, wo_ref, g_ref, b_ref,
                      o_ref, tail_ref, carry_ref):
    x = x_ref[...]
    tm = x.shape[0]
    h3 = _dot(x.astype(BF16), wi_ref[...])
    bg = h3[:, :D_MODEL]
    z = h3[:, D_MODEL:2 * D_MODEL] * h3[:, 2 * D_MODEL:]

    @pl.when(pl.program_id(0) % tiles_per_seq == 0)
    def _():
        carry_ref[...] = buf_ref[0]

    row = lax.broadcasted_iota(jnp.int32, (tm, D_MODEL), 0)
    prev1 = carry_ref[7:8, :]
    prev2 = carry_ref[6:7, :]
    z1 = jnp.where(row == 0, prev1, pltpu.roll(z, 1, 0))
    z2 = jnp.where(row == 0, prev2, jnp.where(row == 1, prev1, pltpu.roll(z, 2, 0)))
    conv = cw_ref[0:1, :] * z2 + cw_ref[1:2, :] * z1 + cw_ref[2:3, :] * z
    tail = z[tm - 8:, :]
    carry_ref[...] = tail
    tail_ref[0] = tail
    _conv_tail(x, bg, conv, wo_ref, g_ref, b_ref, o_ref)


def _conv_short_kernel(seq, x_ref, h1_ref, h2_ref, wi_ref, cw_ref, wo_ref, g_ref, b_ref,
                       o_ref, z_ref):
    x = x_ref[...]
    tm = x.shape[0]
    h3 = _dot(x.astype(BF16), wi_ref[...])
    bg = h3[:, :D_MODEL]
    z = h3[:, D_MODEL:2 * D_MODEL] * h3[:, 2 * D_MODEL:]
    t = lax.broadcasted_iota(jnp.int32, (tm, D_MODEL), 0) % seq
    z1 = jnp.where(t >= 1, pltpu.roll(z, 1, 0), h1_ref[...])
    z2 = jnp.where(t >= 2, pltpu.roll(z, 2, 0), h2_ref[...])
    conv = cw_ref[0:1, :] * z2 + cw_ref[1:2, :] * z1 + cw_ref[2:3, :] * z
    z_ref[...] = z
    _conv_tail(x, bg, conv, wo_ref, g_ref, b_ref, o_ref)


def _first_rows(state_rows, seq):
    bn = state_rows.shape[0]
    out = jnp.zeros((bn, seq, D_MODEL), F32).at[:, 0, :].set(state_rows)
    return out.reshape(bn * seq, D_MODEL)


def _conv_mixer(x, buf_prev, seq, consts):
    n = x.shape[0]
    bn = n // seq
    tm = min(ROW_TILE, n)
    const_specs = [c.const_spec() for c in consts]
    const_args = [c.arr for c in consts]
    if seq >= tm:
        tiles_per_seq = seq // tm
        buf8 = jnp.concatenate([jnp.zeros((bn, 6, D_MODEL), F32), buf_prev], axis=1)
        out, tails = pl.pallas_call(
            functools.partial(_conv_long_kernel, tiles_per_seq),
            grid=(n // tm,),
            in_specs=[_row_spec(D_MODEL, tm),
                      pl.BlockSpec((1, 8, D_MODEL), lambda i: (i // tiles_per_seq, 0, 0))] + const_specs,
            out_specs=[_row_spec(D_MODEL, tm), pl.BlockSpec((1, 8, D_MODEL), lambda i: (i, 0, 0))],
            out_shape=[jax.ShapeDtypeStruct((n, D_MODEL), F32),
                       jax.ShapeDtypeStruct((n // tm, 8, D_MODEL), F32)],
            scratch_shapes=[pltpu.VMEM((8, D_MODEL), F32)],
            compiler_params=_params(("arbitrary",)),
            name="conv_long",
        )(x, buf8, *const_args)
        new_buf = tails[tiles_per_seq - 1::tiles_per_seq, 6:8, :]
        return out, new_buf
    assert tm % seq == 0 and seq >= CONV_W - 1
    h1 = _first_rows(buf_prev[:, 1, :], seq)
    h2 = _first_rows(buf_prev[:, 0, :], seq) + jnp.roll(h1, 1, axis=0)
    out, z = pl.pallas_call(
        functools.partial(_conv_short_kernel, seq),
        grid=(n // tm,),
        in_specs=[_row_spec(D_MODEL, tm)] * 3 + const_specs,
        out_specs=[_row_spec(D_MODEL, tm)] * 2,
        out_shape=[jax.ShapeDtypeStruct((n, D_MODEL), F32)] * 2,
        compiler_params=_params(("parallel",)),
        name="conv_short",
    )(x, h1, h2, *const_args)
    new_buf = z.reshape(bn, seq, D_MODEL)[:, seq - (CONV_W - 1):, :]
    return out, new_buf


def _head_sum_bcast(x, e_ref, et_ref):
    s = _dot(x.astype(BF16), e_ref[...])
    return _dot(jnp.concatenate(_split2(s), axis=1), et_ref[...])


def _rwkv_proj_body(x, xprev, mu_ref, wr_ref, wk_ref, wv_ref, w0_ref, w1_ref, w2_ref, a0_ref, a1_ref,
                    a2_ref, g1_ref, g2_ref, kk_ref, ka_ref, rk_ref, e_ref, et_ref,
                    r_out, lw_out, k_out, v_out, c_out, b_out, bonus_out, g_out):
    xx = xprev - x
    mix = lambda i: (x + xx * mu_ref[i:i + 1, :]).astype(BF16)
    r = _dot(mix(0), wr_ref[...])
    k = _dot(mix(2), wk_ref[...])
    v = _dot(mix(3), wv_ref[...])
    zw = w0_ref[...] + _dot(jnp.tanh(_dot(mix(1), w1_ref[...])).astype(BF16), w2_ref[...])
    lw_out[...] = -_sigmoid(zw) * math.exp(-0.5)
    a = _sigmoid(a0_ref[...] + _dot(_dot(mix(4), a1_ref[...]).astype(BF16), a2_ref[...]))
    g_out[...] = _dot(_sigmoid(_dot(mix(5), g1_ref[...])).astype(BF16), g2_ref[...])
    kk = k * kk_ref[...]
    norm = jnp.sqrt(_head_sum_bcast(kk * kk, e_ref, et_ref))
    c = kk / jnp.maximum(norm, 1e-12)
    kmod = k * (1.0 + (a - 1.0) * ka_ref[...])
    r_out[...] = r
    k_out[...] = kmod
    v_out[...] = v
    c_out[...] = c
    b_out[...] = c * a
    bonus_out[...] = _head_sum_bcast(r * kmod * rk_ref[...], e_ref, et_ref) * v


def _rwkv_proj_long_kernel(tiles_per_seq, x_ref, shift_ref, *rest):
    carry_ref = rest[-1]
    x = x_ref[...]
    tm = x.shape[0]

    @pl.when(pl.program_id(0) % tiles_per_seq == 0)
    def _():
        carry_ref[...] = shift_ref[...]

    row = lax.broadcasted_iota(jnp.int32, (tm, D_MODEL), 0)
    xprev = jnp.where(row == 0, carry_ref[...], pltpu.roll(x, 1, 0))
    carry_ref[...] = x[tm - 1:tm, :]
    _rwkv_proj_body(x, xprev, *rest[:-1])


def _rwkv_proj_short_kernel(seq, x_ref, h1_ref, *rest):
    x = x_ref[...]
    t = lax.broadcasted_iota(jnp.int32, x.shape, 0) % seq
    xprev = jnp.where(t >= 1, pltpu.roll(x, 1, 0), h1_ref[...])
    _rwkv_proj_body(x, xprev, *rest)


def _rwkv_proj(x, shift_prev, seq, consts):
    n = x.shape[0]
    tm = min(ROW_TILE // 2, n)
    common = dict(
        grid=(n // tm,),
        out_specs=[_row_spec(D_MODEL, tm)] * 8,
        out_shape=[jax.ShapeDtypeStruct((n, D_MODEL), F32)] * 8,
    )
    const_specs = [c.const_spec() for c in consts]
    const_args = [c.arr for c in consts]
    if seq >= tm:
        tiles_per_seq = seq // tm
        return pl.pallas_call(
            functools.partial(_rwkv_proj_long_kernel, tiles_per_seq),
            in_specs=[_row_spec(D_MODEL, tm),
                      pl.BlockSpec((None, 1, D_MODEL), lambda i: (i // tiles_per_seq, 0, 0))] + const_specs,
            scratch_shapes=[pltpu.VMEM((1, D_MODEL), F32)],
            compiler_params=_params(("arbitrary",)),
            name="rwkv_proj_long", **common,
        )(x, shift_prev[:, None, :], *const_args)
    assert tm % seq == 0
    return pl.pallas_call(
        functools.partial(_rwkv_proj_short_kernel, seq),
        in_specs=[_row_spec(D_MODEL, tm)] * 2 + const_specs,
        compiler_params=_params(("parallel",)),
        name="rwkv_proj_short", **common,
    )(x, _first_rows(shift_prev, seq), *const_args)


def _pair_rows(x, lane_lo):
    zero = jnp.zeros_like(x)
    return jnp.concatenate([jnp.where(lane_lo, x, zero), jnp.where(lane_lo, zero, x)], axis=0)


def _block_diag(a, b):
    zero = jnp.zeros_like(a)
    return jnp.concatenate([jnp.concatenate([a, zero], axis=1), jnp.concatenate([zero, b], axis=1)], axis=0)


def _scan_chunk(load, store_y, states, state_fn):
    C = SCAN_CHUNK
    pairs = range(len(states))
    row = lax.broadcasted_iota(jnp.int32, (C, 2 * C), 0)
    col = lax.broadcasted_iota(jnp.int32, (C, 2 * C), 1) & (C - 1)
    strict = row > col
    incl = row >= col
    lane_lo = lax.broadcasted_iota(jnp.int32, (C, LANES), 1) < HEAD_SIZE
    tri = jnp.where(lax.broadcasted_iota(jnp.int32, (C, C), 0) >= lax.broadcasted_iota(jnp.int32, (C, C), 1),
                    1.0, 0.0).astype(BF16)
    r, lw, k, v, c, b = zip(*[load(p) for p in pairs])

    cum = []
    for p in pairs:
        parts = _dot(tri, jnp.concatenate(_split3(lw[p]), axis=1))
        cum.append(parts[:, :LANES] + (parts[:, LANES:2 * LANES] + parts[:, 2 * LANES:]))
    p_inc = [jnp.exp(cum[p]) for p in pairs]
    p_inv = [jnp.exp(-cum[p]) for p in pairs]
    p_exc = [jnp.exp(cum[p] - lw[p]) for p in pairs]
    q = [jnp.concatenate([c[p] * p_exc[p], r[p] * p_inc[p]], axis=0).astype(BF16) for p in pairs]
    bt = [(b[p] * p_inv[p]).astype(BF16) for p in pairs]
    kt = [(k[p] * p_inv[p]).astype(BF16) for p in pairs]
    vb = [v[p].astype(BF16) for p in pairs]
    kb = [jnp.concatenate([_pair_rows(bt[p], lane_lo), _pair_rows(kt[p], lane_lo)], axis=0) for p in pairs]
    gram = [_dot(q[p], jnp.concatenate([kb[p], states[p]], axis=0), NT) for p in pairs]
    l_cb = [jnp.where(strict, gram[p][:C, :2 * C], 0.0) for p in pairs]
    l_ck = [jnp.where(strict, gram[p][:C, 2 * C:4 * C], 0.0).astype(BF16) for p in pairs]
    a_rb = [jnp.where(incl, gram[p][C:, :2 * C], 0.0) for p in pairs]
    a_rk = [jnp.where(incl, gram[p][C:, 2 * C:4 * C], 0.0) for p in pairs]
    qs = [gram[p][:, 4 * C:] for p in pairs]
    v_rows = [_pair_rows(vb[p], lane_lo) for p in pairs]
    u = [-(qs[p][:C] + _dot(l_ck[p], v_rows[p])) for p in pairs]
    m = [l_cb[p].astype(BF16) for p in pairs]
    for level in range(int(math.log2(C)) - 1):
        both = [_dot(m[p], jnp.concatenate([_pair_rows(m[p], lane_lo), _pair_rows(u[p].astype(BF16), lane_lo)],
                                           axis=1)) for p in pairs]
        if level == 0:
            u = [u[p] - both[p][:, 2 * C:] for p in pairs]
        else:
            u = [u[p] + both[p][:, 2 * C:] for p in pairs]
        m = [both[p][:, :2 * C].astype(BF16) for p in pairs]
    u = [u[p] + _dot(m[p], _pair_rows(u[p].astype(BF16), lane_lo)) for p in pairs]
    ub = [u[p].astype(BF16) for p in pairs]
    for p in pairs:
        a = jnp.concatenate([a_rb[p], a_rk[p]], axis=1).astype(BF16)
        uv_rows = jnp.concatenate([_pair_rows(ub[p], lane_lo), v_rows[p]], axis=0)
        store_y(p, qs[p][C:] + _dot(a, uv_rows))
    state_fn(u, v, ub, vb, bt, kt, p_inc)


def _pair_lanes(p):
    return slice(p * LANES, (p + 1) * LANES)


def _diag_blocks_mask():
    sq_row = lax.broadcasted_iota(jnp.int32, (LANES, LANES), 0) < HEAD_SIZE
    sq_col = lax.broadcasted_iota(jnp.int32, (LANES, LANES), 1) < HEAD_SIZE
    return sq_row == sq_col


def _store_pair_state(out_ref, i, p, s_pair):
    out_ref[i, 2 * p] = s_pair[:HEAD_SIZE, :HEAD_SIZE]
    out_ref[i, 2 * p + 1] = s_pair[HEAD_SIZE:, HEAD_SIZE:]


SCAN_SEQS = 8


def _rwkv_scan_kernel(r_ref, lw_ref, k_ref, v_ref, c_ref, b_ref, s0_ref, y_ref, sfin_ref, s_ref):
    C = SCAN_CHUNK
    j = pl.program_id(1)
    chains = [(s, p) for s in range(SCAN_SEQS) for p in range(HEAD_PAIRS)]

    @pl.when(j == 0)
    def _():
        for s, p in chains:
            s_ref[s, p] = _block_diag(s0_ref[s, 2 * p], s0_ref[s, 2 * p + 1])

    diag = _diag_blocks_mask()
    refs = (r_ref, lw_ref, k_ref, v_ref, c_ref, b_ref)

    def load(ch):
        s, p = chains[ch]
        return tuple(ref[s, :, _pair_lanes(p)] for ref in refs)

    def store_y(ch, y):
        s, p = chains[ch]
        y_ref[s, :, _pair_lanes(p)] = y

    def state_fn(u, v, ub, vb, bt, kt, p_inc):
        for ch, (s, p) in enumerate(chains):
            upd = _dot(jnp.concatenate([ub[ch], vb[ch]], axis=0), jnp.concatenate([bt[ch], kt[ch]], axis=0), TN)
            s_ref[s, p] = (s_ref[s, p] + jnp.where(diag, upd, 0.0)) * p_inc[ch][C - 1:C, :]

    _scan_chunk(load, store_y, [s_ref[s, p].astype(BF16) for s, p in chains], state_fn)

    @pl.when(j == pl.num_programs(1) - 1)
    def _():
        for s, p in chains:
            _store_pair_state(sfin_ref, s, p, s_ref[s, p])


def _rwkv_scan(r, lw, k, v, c, b, s0, bn, seq):
    assert bn % SCAN_SEQS == 0
    row_spec = pl.BlockSpec((SCAN_SEQS, SCAN_CHUNK, D_MODEL), lambda i, j: (i, j, 0))
    st_spec = pl.BlockSpec((SCAN_SEQS, B_HEADS, HEAD_SIZE, HEAD_SIZE), lambda i, j: (i, 0, 0, 0))
    as_seqs = lambda z: z.reshape(bn, seq, D_MODEL)
    ys, s_new = pl.pallas_call(
        _rwkv_scan_kernel,
        grid=(bn // SCAN_SEQS, seq // SCAN_CHUNK),
        in_specs=[row_spec] * 6 + [st_spec],
        out_specs=[row_spec, st_spec],
        out_shape=[jax.ShapeDtypeStruct((bn, seq, D_MODEL), F32),
                   jax.ShapeDtypeStruct((bn, B_HEADS, HEAD_SIZE, HEAD_SIZE), F32)],
        scratch_shapes=[pltpu.VMEM((SCAN_SEQS, HEAD_PAIRS, LANES, LANES), F32)],
        compiler_params=_params(("parallel", "arbitrary")),
        name="rwkv_scan",
    )(*(as_seqs(z) for z in (r, lw, k, v, c, b)), s0)
    return ys.reshape(bn * seq, D_MODEL), s_new


def _rwkv_scan_lanes_kernel(seq, r_ref, lw_ref, k_ref, v_ref, c_ref, b_ref, s0_ref, y_ref, sfin_ref,
                            cols_ref, yt_ref):
    nb = s0_ref.shape[-1]
    for a, ref in enumerate((r_ref, lw_ref, k_ref, v_ref, c_ref, b_ref)):
        for t in range(seq):
            col = ref[pl.ds(t, nb, stride=seq), :].T
            cols_ref[a, t] = jnp.exp(col) if ref is lw_ref else col
    for hh in range(2):
        feat = slice(hh * HEAD_SIZE, (hh + 1) * HEAD_SIZE)
        for vi in range(HEAD_SIZE):
            row = hh * HEAD_SIZE + vi
            s = s0_ref[hh, vi]
            for t in range(seq):
                r, w, k, c, b = (cols_ref[a, t, feat, :] for a in (0, 1, 2, 4, 5))
                sa = jnp.sum(s * c, axis=0, keepdims=True)
                s = s * w - sa * b + cols_ref[3, t, row:row + 1, :] * k
                yt_ref[t, row:row + 1, :] = jnp.sum(s * r, axis=0, keepdims=True)
            sfin_ref[hh, vi] = s
    for t in range(seq):
        y_ref[pl.ds(t, nb, stride=seq), :] = yt_ref[t].T


def _rwkv_scan_lanes(r, lw, k, v, c, b, s0, bn, seq):
    assert bn == LANES
    n = bn * seq
    row_spec = pl.BlockSpec((n, LANES), lambda p: (0, p))
    st_spec = pl.BlockSpec((2, HEAD_SIZE, HEAD_SIZE, bn), lambda p: (p, 0, 0, 0))
    ys, s_new = pl.pallas_call(
        functools.partial(_rwkv_scan_lanes_kernel, seq),
        grid=(HEAD_PAIRS,),
        in_specs=[row_spec] * 6 + [st_spec],
        out_specs=[row_spec, st_spec],
        out_shape=[jax.ShapeDtypeStruct((n, D_MODEL), F32),
                   jax.ShapeDtypeStruct((B_HEADS, HEAD_SIZE, HEAD_SIZE, bn), F32)],
        scratch_shapes=[pltpu.VMEM((6, seq, LANES, bn), F32), pltpu.VMEM((seq, LANES, bn), F32)],
        compiler_params=_params(("parallel",)),
        name="rwkv_scan_lanes",
    )(r, lw, k, v, c, b, jnp.transpose(s0, (1, 2, 3, 0)))
    return ys, jnp.transpose(s_new, (3, 0, 1, 2))


def _rwkv_post_kernel(streams, *refs):
    rows, (lg_ref, lb_ref, wo_ref, e_ref, et_ref, g_ref, b_ref), o_refs = refs[:8], refs[8:15], refs[15:]

    def body(stream):
        x_ref, y_ref, bonus_ref, gate_ref = rows[stream::2]
        ys = y_ref[...]
        inv = 1.0 / HEAD_SIZE
        m = _head_sum_bcast(ys, e_ref, et_ref) * inv
        yc = ys - m
        var = _head_sum_bcast(yc * yc, e_ref, et_ref) * inv
        yn = yc * lax.rsqrt(var + GN_EPS) * lg_ref[...] + lb_ref[...]
        out = _dot(((yn + bonus_ref[...]) * gate_ref[...]).astype(BF16), wo_ref[...])
        o_refs[stream][...] = _layer_norm(ALPHA * x_ref[...] + out, g_ref[...], b_ref[...])

    streams.per_stream(body)


def _rwkv_post(rows_long, rows_short, consts):
    streams = _TwoStreams(rows_long[0].shape[0], rows_short[0].shape[0])
    args, specs = [], []
    for a_long, a_short in zip(rows_long, rows_short):
        args += [a_long, a_short]
        specs += streams.row_specs(D_MODEL)
    return pl.pallas_call(
        functools.partial(_rwkv_post_kernel, streams),
        grid=streams.grid,
        in_specs=specs + [c.const_spec() for c in consts],
        out_specs=streams.row_specs(D_MODEL),
        out_shape=[jax.ShapeDtypeStruct(rows_long[0].shape, F32), jax.ShapeDtypeStruct(rows_short[0].shape, F32)],
        compiler_params=_params(("arbitrary",)),
        name="rwkv_post",
    )(*args, *[c.arr for c in consts])


def _rwkv_mixer(x, shift_prev, s0, seq, proj_consts):
    n = x.shape[0]
    bn = n // seq
    r, lw, k, v, c, bb, bonus, gate = _rwkv_proj(x, shift_prev, seq, proj_consts)
    scan = _rwkv_scan if seq % SCAN_CHUNK == 0 else _rwkv_scan_lanes
    ys, s_new = scan(r, lw, k, v, c, bb, s0, bn, seq)
    return (x, ys, bonus, gate), x.reshape(bn, seq, D_MODEL)[:, -1, :], s_new


def _run_trunks(x_long, x_short, p_long, p_short, states_long, states_short, W):
    shapes = [x_long.shape, x_short.shape]
    seqs = [s[1] for s in shapes]
    assert seqs[0] >= CHUNK > seqs[1]
    xs = [x_long.reshape(-1, D_MODEL), x_short.reshape(-1, D_MODEL)]
    ps = [p_long.reshape(DEPTH, -1, PLE_DIM), p_short.reshape(DEPTH, -1, PLE_DIM)]
    states = [states_long, states_short]
    new_v, new_wkv, new_shift, new_conv = [], ([], []), ([], []), ([], [])
    ffn_w = [_sel(W["ffn_w_in"][0, 0].astype(BF16)), _sel(W["ffn_w_out"][0, 0].astype(BF16))]

    def mixer_mats(i):
        j, kind = divmod(i, N_MIXERS)
        if kind == 0:
            return [_sel(W["a_w_in"], j), _sel(W["a_w_out"], j)]
        if kind == 1:
            return [_sel(W["b_w_rkv"], j, m) for m in range(3)] + [_sel(W["b_w_o"], j)]
        return [_sel(W["c_w_in"], j), _sel(W["c_w_out"], j)]

    def ffn_pair(xs, ffn_w, i, s, cast_next, ple=None):
        norm = (_sel(W["ln_g"], i, 2 * s), _sel(W["ln_b"], i, 2 * s))
        if ple is not None:
            ple = ((_sel(ps[0], i), _sel(ps[1], i)),) + tuple(ple)
        res = _ffn(xs[0], xs[1], *ffn_w, *norm, ple=ple, cast_next=cast_next)
        return list(res[:2]), [_sel(a) for a in res[2:]]

    for i in range(DEPTH):
        j, kind = divmod(i, N_MIXERS)
        ln_g = lambda s: _sel(W["ln_g"], i, s)
        ln_b = lambda s: _sel(W["ln_b"], i, s)
        second = [_sel(W[name], i, 1) for name in ("ffn_w_in", "ffn_w_out")] + \
                 [_sel(W[name], i) for name in ("ple_w_gate", "ple_w_proj")]
        xs, cast = ffn_pair(xs, ffn_w, i, 0, second + mixer_mats(i))
        ffn_w, ple_w, mats = cast[:2], cast[2:4], cast[4:]
        if kind == 0:
            consts = [mats[0]] + [_sel(W[name], j) for name in ("a_b_in", "a_ln_g", "a_ln_b")] + [mats[1]]
            x_l, x_s, v = _gmlp(*xs, consts + [ln_g(1), ln_b(1)], [_sel(m, j) for m in W["a_mix_long"]],
                                [_sel(m, j) for m in W["a_mix_short"]])
            xs = [x_l, x_s]
            new_v.append(v.reshape(shapes[1][0], seqs[1], A_INNER))
        elif kind == 1:
            proj = W["b_proj"]
            w = dict(proj=[_sel(proj[0], j)] + mats[:3] + [_sel(a, j) for a in proj[1:]] + [_sel(W["e"]), _sel(W["et"])],
                     post=[_sel(a, j) for a in W["b_post"]] + [mats[3], _sel(W["e"]), _sel(W["et"])])
            post_rows = []
            for t in range(2):
                wkv, shift, _ = states[t]
                rows, sh, s = _rwkv_mixer(xs[t], shift[j], wkv[j], seqs[t], w["proj"])
                post_rows.append(rows)
                new_shift[t].append(sh)
                new_wkv[t].append(s)
            xs = list(_rwkv_post(*post_rows, w["post"] + [ln_g(1), ln_b(1)]))
        else:
            consts = [mats[0], _sel(W["c_conv_w"], j), mats[1], ln_g(1), ln_b(1)]
            for t in range(2):
                xs[t], buf = _conv_mixer(xs[t], states[t][2][j], seqs[t], consts)
                new_conv[t].append(buf)
        first_next = [_sel(W[name], i + 1, 0) for name in ("ffn_w_in", "ffn_w_out")] if i + 1 < DEPTH else []
        xs, ffn_w = ffn_pair(xs, ffn_w, i, 1, first_next, ple=ple_w)
    outs = [(xs[t].reshape(shapes[t]), jnp.stack(new_wkv[t]), jnp.stack(new_shift[t]), jnp.stack(new_conv[t]))
            for t in range(2)]
    return outs[0], outs[1], jnp.stack(new_v)


def _gmlp_mix_mats(w_s, b_s, seq):
    l = min(seq, CHUNK)
    ws = jnp.where(jnp.tril(jnp.ones((l, l), dtype=bool)), w_s[..., :l, :l], 0.0)
    reps = CHUNK // l
    if reps > 1:
        pos = jnp.arange(CHUNK) // l
        ws = jnp.where(pos[:, None] == pos[None, :], jnp.tile(ws, (1, 1, reps, reps)), 0.0)
    bias = jnp.tile(jnp.swapaxes(b_s[..., :l], -1, -2), (1, reps, 1))
    return ws.astype(BF16), bias


def kernel(x_prompt, x_sample, state_b_wkv, state_b_shift, state_c_conv, p_prompt, p_sample, ln_g, ln_b, ffn_w_in, ffn_w_out, ple_w_gate, ple_w_proj, a_w_in, a_b_in, a_ln_g, a_ln_b, a_w_s, a_b_s, a_w_out, b_mu, b_w_rkv, b_w0, b_w1, b_w2, b_a0, b_a1, b_a2, b_g1, b_g2, b_k_k, b_k_a, b_r_k, b_lnx_g, b_lnx_b, b_w_o, c_w_in, c_conv_w, c_w_out):
    bf = lambda w: w.astype(BF16)
    row = lambda w: w.reshape(w.shape[0], 1, -1)
    head_of_lane = jnp.arange(D_MODEL) // HEAD_SIZE
    e = (head_of_lane[:, None] == jnp.arange(LANES)[None, :]).astype(BF16)
    W = dict(
        ln_g=ln_g[:, :, None, :], ln_b=ln_b[:, :, None, :],
        ffn_w_in=ffn_w_in, ffn_w_out=ffn_w_out,
        ple_w_gate=ple_w_gate, ple_w_proj=ple_w_proj,
        a_w_in=a_w_in, a_b_in=row(a_b_in), a_ln_g=row(a_ln_g), a_ln_b=row(a_ln_b), a_w_out=a_w_out,
        a_mix_long=_gmlp_mix_mats(a_w_s, a_b_s, x_prompt.shape[1]),
        a_mix_short=_gmlp_mix_mats(a_w_s, a_b_s, x_sample.shape[1]),
        c_w_in=c_w_in, c_conv_w=c_conv_w, c_w_out=c_w_out,
        b_w_rkv=b_w_rkv, b_w_o=b_w_o,
        b_proj=[b_mu, row(b_w0), bf(b_w1), bf(b_w2), row(b_a0), bf(b_a1), bf(b_a2),
                bf(b_g1), bf(b_g2), row(b_k_k), row(b_k_a), row(b_r_k)],
        b_post=[row(b_lnx_g), row(b_lnx_b)],
        e=e, et=jnp.concatenate([e.T, e.T], axis=0),
    )
    bp = x_prompt.shape[0]
    n_b = b_mu.shape[0]
    n_c = c_w_in.shape[0]
    zero_wkv = jnp.zeros((n_b, bp) + state_b_wkv.shape[2:], state_b_wkv.dtype)
    zero_shift = jnp.zeros((n_b, bp, D_MODEL), state_b_shift.dtype)
    zero_conv = jnp.zeros((n_c, bp, CONV_W - 1, D_MODEL), state_c_conv.dtype)
    (y_p, wkv_p, shift_p, conv_p), (y_s, wkv_s, shift_s, conv_s), a_v_s = _run_trunks(
        x_prompt, x_sample, p_prompt, p_sample, (zero_wkv, zero_shift, zero_conv),
        (state_b_wkv, state_b_shift, state_c_conv), W)
    return (y_p, y_s, a_v_s, wkv_p, shift_p, conv_p, wkv_s, shift_s, conv_s)
```
